```python
import numpy as np
import jax
import jax.numpy as jnp
from jax import lax

D_MODEL = 2048
BATCH = 4
SEQ = 2048
DEPTH = 2

H_A = 8
KV_A = 2
HPG_A = H_A // KV_A
HD_A = 128
L_CMP = 32
STRIDE_CMP = 16
L_SEL = 64
N_SEL = 8
WIN_A = 512
H_B = 8
KV_B = 2
HPG_B = H_B // KV_B
HD_B = 64
WIN_B = 128
H_C = 4
HD_C = 128
BLK_C = 256
TOPK_C = 3
Q_BLOCK = 128
GATHER_CHUNK = 64
N_GROUPS = 4
EXP_PER_GROUP = 8
N_EXPERTS = N_GROUPS * EXP_PER_GROUP
D_EXPERT = 256
TOPK_EXPERT = 2

LN_EPS = 1e-5
NEG_INF = -1e30

Q_A_W = H_A * HD_A
KV_A_W = 3 * 2 * KV_A * HD_A
GATE_A_W = 3 * H_A
Q_B_W = H_B * HD_B
KV_B_W = KV_B * HD_B
C_W = H_C * HD_C
MERGE_W = 3 * D_MODEL
IN_WIDTHS = (Q_A_W, KV_A_W, GATE_A_W, Q_B_W, KV_B_W, KV_B_W, C_W, C_W, C_W, MERGE_W)
D_IN = sum(IN_WIDTHS)

kernel_name = 'hybrid_nsa_swa_moba_hmoe'


def alibi_slopes(n):
    return jnp.asarray([2.0 ** (-8.0 * (i + 1) / n) for i in range(n)], jnp.float32)


def layer_norm(x, g, b):
    xf = x.astype(jnp.float32)
    mu = jnp.mean(xf, -1, keepdims=True)
    xc = xf - mu
    var = jnp.mean(xc * xc, -1, keepdims=True)
    return xc * lax.rsqrt(var + LN_EPS) * g.astype(jnp.float32) + b.astype(jnp.float32)


def banded_window_attention(q, k, v, slopes, window, sink=None):
    B, G, H, S, d = q.shape
    nqb = S // Q_BLOCK
    span = window + Q_BLOCK
    pad = ((0, 0), (0, 0), (window, 0), (0, 0))
    kidx = Q_BLOCK * np.arange(nqb)[:, None] + np.arange(span)[None, :]
    kw = jnp.pad(k, pad)[:, :, kidx]
    vw = jnp.pad(v, pad)[:, :, kidx]
    qb = q.reshape(B, G, H, nqb, Q_BLOCK, d)
    t_pos = Q_BLOCK * np.arange(nqb)[:, None] + np.arange(Q_BLOCK)[None, :]
    s_pos = kidx - window
    dist = t_pos[:, :, None] - s_pos[:, None, :]
    mask = jnp.asarray((dist >= 0) & (dist < window) & (s_pos[:, None, :] >= 0))
    s = jnp.einsum('bghcqd,bgckd->bghcqk', qb, kw, preferred_element_type=jnp.float32) * (d ** -0.5)
    s = s - slopes[:, :, None, None, None] * jnp.asarray(dist, jnp.float32)
    s = jnp.where(mask, s, NEG_INF)
    if sink is not None:
        sink_col = jnp.broadcast_to(sink.astype(jnp.float32)[:, :, None, None, None], s.shape[:-1] + (1,))
        p = jax.nn.softmax(jnp.concatenate([s, sink_col], -1), -1)[..., :-1]
    else:
        p = jax.nn.softmax(s, -1)
    o = jnp.einsum('bghcqk,bgckd->bghcqd', p, vw)
    return o.reshape(B, G, H, S, d)


def gathered_block_attention(q, k_blocks, v_blocks, blk_idx, blk_ok, slopes, with_own_block):
    B, G, H, S, d = q.shape
    L = k_blocks.shape[3]
    n = blk_idx.shape[-1]
    nch = S // GATHER_CHUNK
    scale = d ** -0.5
    offs = jnp.arange(L)
    sl = slopes.astype(jnp.float32)
    take = jax.vmap(jax.vmap(lambda blocks, idx: blocks[idx]))
    q_ch = jnp.moveaxis(q.reshape(B, G, H, nch, GATHER_CHUNK, d), 3, 0)
    i_ch = jnp.moveaxis(blk_idx.reshape(B, G, nch, GATHER_CHUNK, n), 2, 0)
    ok_ch = jnp.moveaxis(blk_ok.reshape(B, G, nch, GATHER_CHUNK, n), 2, 0)

    def chunk(args):
        q_c, idx_c, ok_c, c = args
        t = c * GATHER_CHUNK + jnp.arange(GATHER_CHUNK)
        kg = take(k_blocks, idx_c)
        vg = take(v_blocks, idx_c)
        dist = t[:, None, None] - (idx_c[..., None] * L + offs)
        mask = ok_c[..., None] & (dist >= 0)
        s = jnp.einsum('bghqd,bgqnld->bghqnl', q_c, kg, preferred_element_type=jnp.float32) * scale
        s = s - sl[:, :, None, None, None] * dist[:, :, None].astype(jnp.float32)
        s = jnp.where(mask[:, :, None], s, NEG_INF).reshape(B, G, H, GATHER_CHUNK, n * L)
        parts = [s]
        if with_own_block:
            own = (c * GATHER_CHUNK) // L
            k_own = lax.dynamic_index_in_dim(k_blocks, own, axis=2, keepdims=False)
            v_own = lax.dynamic_index_in_dim(v_blocks, own, axis=2, keepdims=False)
            dist_o = t[:, None] - (own * L + offs)[None, :]
            s_o = jnp.einsum('bghqd,bgld->bghql', q_c, k_own, preferred_element_type=jnp.float32) * scale
            s_o = jnp.where(dist_o >= 0, s_o - sl[:, :, None, None] * dist_o.astype(jnp.float32), NEG_INF)
            parts.append(s_o)
        p = jax.nn.softmax(jnp.concatenate(parts, -1), -1)
        o = jnp.einsum('bghqnl,bgqnld->bghqd', p[..., :n * L].reshape(B, G, H, GATHER_CHUNK, n, L), vg)
        if with_own_block:
            o = o + jnp.einsum('bghql,bgld->bghqd', p[..., n * L:], v_own)
        return o

    out = lax.map(chunk, (q_ch, i_ch, ok_ch, jnp.arange(nch)))
    return jnp.moveaxis(out, 0, 3).reshape(B, G, H, S, d)


def nsa_mixer(q, kv, gate_logits, cmp_pos, cmp_w):
    B, S, _ = q.shape
    f32 = jnp.float32
    q = q.reshape(B, S, KV_A, HPG_A, HD_A).transpose(0, 2, 3, 1, 4)
    kv = kv.reshape(B, S, 3, 2, KV_A, HD_A).transpose(2, 3, 0, 4, 1, 5)
    slopes = alibi_slopes(H_A).reshape(KV_A, HPG_A)
    pos = np.arange(S)

    n_cmp = (S - L_CMP) // STRIDE_CMP + 1
    c_start = STRIDE_CMP * np.arange(n_cmp)
    c_idx = c_start[:, None] + np.arange(L_CMP)[None, :]
    k_cmp = jnp.einsum('bgnld,lde->bgne', kv[0, 0][:, :, c_idx] + cmp_pos[0], cmp_w[0])
    v_cmp = jnp.einsum('bgnld,lde->bgne', kv[0, 1][:, :, c_idx] + cmp_pos[1], cmp_w[1])
    dist = pos[:, None] - (c_start + L_CMP - 1)[None, :]
    ok = jnp.asarray(dist >= 0)
    s = jnp.einsum('bghtd,bgnd->bghtn', q, k_cmp, preferred_element_type=f32) * (HD_A ** -0.5)
    s = s - slopes[:, :, None, None] * jnp.asarray(dist, f32)
    p_cmp = jax.nn.softmax(jnp.where(ok, s, NEG_INF), -1) * ok
    o_cmp = jnp.einsum('bghtn,bgnd->bghtd', p_cmp, v_cmp)

    n_sel = S // L_SEL
    s_start = L_SEL * np.arange(n_sel)
    inter = np.clip(np.minimum(c_start[:, None] + L_CMP, s_start[None, :] + L_SEL)
                    - np.maximum(c_start[:, None], s_start[None, :]), 0, None) / L_CMP
    imp = jnp.einsum('bghtn,nj->bgtj', p_cmp, jnp.asarray(inter, f32))
    blk_t = pos // L_SEL
    j = np.arange(n_sel)
    valid = j[None, :] <= blk_t[:, None]
    forced = (j[None, :] == 0) | (j[None, :] == blk_t[:, None]) | (j[None, :] == blk_t[:, None] - 1)
    imp = jnp.where(jnp.asarray(forced), jnp.inf, jnp.where(jnp.asarray(valid), imp, -jnp.inf))
    _, sel = lax.top_k(imp, min(N_SEL, n_sel))
    sel_ok = sel <= jnp.asarray(blk_t)[:, None]
    kb = kv[1, 0].reshape(B, KV_A, n_sel, L_SEL, HD_A)
    vb = kv[1, 1].reshape(B, KV_A, n_sel, L_SEL, HD_A)
    o_slc = gathered_block_attention(q, kb, vb, sel, sel_ok, slopes, False)

    o_win = banded_window_attention(q, kv[2, 0], kv[2, 1], slopes, WIN_A)

    g = jax.nn.sigmoid(gate_logits.astype(f32)).reshape(B, S, KV_A, HPG_A, 3).transpose(0, 2, 3, 1, 4)[..., None]
    o = g[..., 0, :] * o_cmp + g[..., 1, :] * o_slc + g[..., 2, :] * o_win
    return o.transpose(0, 3, 1, 2, 4).reshape(B, S, Q_A_W)


def swa_sink_mixer(q, k, v, sink):
    B, S, _ = q.shape
    q = q.reshape(B, S, KV_B, HPG_B, HD_B).transpose(0, 2, 3, 1, 4)
    k = k.reshape(B, S, KV_B, HD_B).transpose(0, 2, 1, 3)
    v = v.reshape(B, S, KV_B, HD_B).transpose(0, 2, 1, 3)
    slopes = alibi_slopes(H_B).reshape(KV_B, HPG_B)
    o = banded_window_attention(q, k, v, slopes, WIN_B, sink.reshape(KV_B, HPG_B))
    return o.transpose(0, 3, 1, 2, 4).reshape(B, S, Q_B_W)


def moba_mixer(q, k, v):
    B, S, _ = q.shape
    f32 = jnp.float32
    s_pad = -(-S // BLK_C) * BLK_C
    nb = s_pad // BLK_C
    padw = ((0, 0), (0, 0), (0, s_pad - S), (0, 0))
    q = jnp.pad(q.reshape(B, S, H_C, HD_C).transpose(0, 2, 1, 3), padw)
    k = jnp.pad(k.reshape(B, S, H_C, HD_C).transpose(0, 2, 1, 3), padw)
    v = jnp.pad(v.reshape(B, S, H_C, HD_C).transpose(0, 2, 1, 3), padw)
    kb = k.reshape(B, H_C, nb, BLK_C, HD_C)
    vb = v.reshape(B, H_C, nb, BLK_C, HD_C)
    k_mean = jnp.mean(kb.astype(f32), axis=3)
    score = jnp.einsum('bhtd,bhnd->bhtn', q, k_mean, preferred_element_type=f32)
    blk_t = np.arange(s_pad) // BLK_C
    past = jnp.asarray(np.arange(nb)[None, :] < blk_t[:, None])
    _, sel = lax.top_k(jnp.where(past, score, -jnp.inf), min(TOPK_C, nb))
    sel_ok = sel < jnp.asarray(blk_t)[:, None]
    o = gathered_block_attention(q[:, :, None], kb, vb, sel, sel_ok, alibi_slopes(H_C).reshape(H_C, 1), True)
    return o[:, :, 0, :S].transpose(0, 2, 1, 3).reshape(B, S, C_W)


def mixer_sublayer(x, w_in, cmp_pos, cmp_w, sink, w_br_a, w_br_b, w_br_c, w_out):
    B, S, D = x.shape
    z = jnp.einsum('bsd,dc->bsc', x, w_in)
    offsets = np.cumsum(IN_WIDTHS)[:-1].tolist()
    q_a, kv_a, gate_a, q_b, k_b, v_b, q_c, k_c, v_c, gate_m = jnp.split(z, offsets, axis=-1)
    o_a = nsa_mixer(q_a, kv_a, gate_a, cmp_pos, cmp_w)
    o_b = swa_sink_mixer(q_b, k_b, v_b, sink)
    o_c = moba_mixer(q_c, k_c, v_c)
    gm = jax.nn.sigmoid(gate_m.astype(jnp.float32)).reshape(B, S, 3, D)
    merged = gm[:, :, 0] * (o_a @ w_br_a) + gm[:, :, 1] * (o_b @ w_br_b) + gm[:, :, 2] * (o_c @ w_br_c)
    return merged.astype(x.dtype) @ w_out


def hierarchical_moe(h, w_group, b_group, w_router, b_router, w_gate, w_up, w_down):
    B, S, D = h.shape
    f32 = jnp.float32
    t = h.reshape(B * S, D)
    g_prob = jax.nn.softmax((t @ w_group).astype(f32) + b_group.astype(f32), -1)
    g_w, g_idx = lax.top_k(g_prob, 1)
    e_logits = ((t @ w_router).astype(f32) + b_router.astype(f32)).reshape(-1, N_GROUPS, EXP_PER_GROUP)
    e_in = jnp.take_along_axis(e_logits, g_idx[:, :, None], axis=1)[:, 0]
    e_top, e_idx = lax.top_k(e_in, TOPK_EXPERT)
    w = jax.nn.softmax(e_top, -1) * g_w
    combine = jnp.einsum('tk,tke->te', w, jax.nn.one_hot(g_idx * EXP_PER_GROUP + e_idx, N_EXPERTS, dtype=f32))
    a = jax.nn.silu(jnp.einsum('td,edf->tef', t, w_gate)) * jnp.einsum('td,edf->tef', t, w_up)
    y = jnp.einsum('tef,efd->td', a * combine[:, :, None].astype(a.dtype), w_down)
    return y.reshape(B, S, D).astype(h.dtype)


def setup_inputs(seed: int = 0) -> dict:
    key = jax.random.key(seed)
    ks = jax.random.split(key, 32)
    f32 = jnp.float32
    beta = (8.0 * DEPTH) ** -0.25
    s_d = D_MODEL ** -0.5

    def nrm(k, shape, scale):
        return jax.random.normal(k, shape, f32) * scale

    kv_a = nrm(ks[2], (DEPTH, D_MODEL, 3, 2, KV_A * HD_A), s_d) * jnp.asarray([1.0, beta], f32)[:, None]
    w_in = jnp.concatenate([
        nrm(ks[1], (DEPTH, D_MODEL, Q_A_W), s_d),
        kv_a.reshape(DEPTH, D_MODEL, KV_A_W),
        nrm(ks[3], (DEPTH, D_MODEL, GATE_A_W), s_d),
        nrm(ks[4], (DEPTH, D_MODEL, Q_B_W), s_d),
        nrm(ks[5], (DEPTH, D_MODEL, KV_B_W), s_d),
        nrm(ks[6], (DEPTH, D_MODEL, KV_B_W), s_d * beta),
        nrm(ks[7], (DEPTH, D_MODEL, C_W), s_d),
        nrm(ks[8], (DEPTH, D_MODEL, C_W), s_d),
        nrm(ks[9], (DEPTH, D_MODEL, C_W), s_d * beta),
        nrm(ks[10], (DEPTH, D_MODEL, MERGE_W), s_d)], axis=-1)
    return {
        'x': jax.random.normal(ks[0], (BATCH, SEQ, D_MODEL), f32),
        'w_in': w_in,
        'nsa_cmp_pos': nrm(ks[11], (DEPTH, 2, L_CMP, HD_A), 0.1),
        'nsa_cmp_w': nrm(ks[12], (DEPTH, 2, L_CMP, HD_A, HD_A), (L_CMP * HD_A) ** -0.5),
        'sink_b': nrm(ks[13], (DEPTH, H_B), 0.5),
        'w_br_a': nrm(ks[14], (DEPTH, Q_A_W, D_MODEL), Q_A_W ** -0.5),
        'w_br_b': nrm(ks[15], (DEPTH, Q_B_W, D_MODEL), Q_B_W ** -0.5),
        'w_br_c': nrm(ks[16], (DEPTH, C_W, D_MODEL), C_W ** -0.5),
        'w_out': nrm(ks[17], (DEPTH, D_MODEL, D_MODEL), s_d * beta),
        'ln1_g': 1.0 + nrm(ks[18], (DEPTH, D_MODEL), 0.02),
        'ln1_b': nrm(ks[19], (DEPTH, D_MODEL), 0.02),
        'w_group': nrm(ks[20], (DEPTH, D_MODEL, N_GROUPS), s_d),
        'b_group': nrm(ks[21], (DEPTH, N_GROUPS), 0.01),
        'w_router': nrm(ks[22], (DEPTH, D_MODEL, N_EXPERTS), s_d),
        'b_router': nrm(ks[23], (DEPTH, N_EXPERTS), 0.01),
        'w_gate': nrm(ks[24], (DEPTH, N_EXPERTS, D_MODEL, D_EXPERT), s_d),
        'w_up': nrm(ks[25], (DEPTH, N_EXPERTS, D_MODEL, D_EXPERT), s_d),
        'w_down': nrm(ks[26], (DEPTH, N_EXPERTS, D_EXPERT, D_MODEL), D_EXPERT ** -0.5 * beta),
        'ln2_g': 1.0 + nrm(ks[27], (DEPTH, D_MODEL), 0.02),
        'ln2_b': nrm(ks[28], (DEPTH, D_MODEL), 0.02),
    }


def reference(x, w_in, nsa_cmp_pos, nsa_cmp_w, sink_b, w_br_a, w_br_b, w_br_c, w_out,
              ln1_g, ln1_b, w_group, b_group, w_router, b_router, w_gate, w_up, w_down,
              ln2_g, ln2_b):
    alpha = (2.0 * DEPTH) ** 0.25
    for l in range(DEPTH):
        y = mixer_sublayer(x, w_in[l], nsa_cmp_pos[l], nsa_cmp_w[l], sink_b[l],
                           w_br_a[l], w_br_b[l], w_br_c[l], w_out[l])
        x = layer_norm(alpha * x + y, ln1_g[l], ln1_b[l]).astype(x.dtype)
        y = hierarchical_moe(x, w_group[l], b_group[l], w_router[l], b_router[l],
                             w_gate[l], w_up[l], w_down[l])
        x = layer_norm(alpha * x + y, ln2_g[l], ln2_b[l]).astype(x.dtype)
    return x
```

```python
import functools

import numpy as np
import jax
import jax.numpy as jnp
from jax import lax
from jax.experimental import pallas as pl
from jax.experimental.pallas import tpu as pltpu

F32 = jnp.float32
BF16 = jnp.bfloat16
HIGHEST = lax.Precision.HIGHEST

D_MODEL = 2048
DEPTH = 2
H_A, KV_A, HD_A = 8, 2, 128
HPG_A = H_A // KV_A
L_CMP, STRIDE_CMP, L_SEL, N_SEL, WIN_A = 32, 16, 64, 8, 512
H_B, KV_B, HD_B, WIN_B = 8, 2, 64, 128
HPG_B = H_B // KV_B
H_C, HD_C, BLK_C, TOPK_C = 4, 128, 256, 3
N_GROUPS, EXP_PER_GROUP, D_EXPERT, TOPK_EXPERT = 4, 8, 256, 2
N_EXPERTS = N_GROUPS * EXP_PER_GROUP
LN_EPS = 1e-5
NEG_INF = -1e30
ALPHA = (2.0 * DEPTH) ** 0.25

LANES = 128
VMEM_LIMIT = 48 * 1024 * 1024

Q_A_W = H_A * HD_A
KV_A_W = 3 * 2 * KV_A * HD_A
GATE_A_W = 3 * H_A
Q_B_W = H_B * HD_B
KV_B_W = KV_B * HD_B
C_W = H_C * HD_C
MERGE_W = 3 * D_MODEL
OFF_MERGE = 0
OFF_QA = OFF_MERGE + MERGE_W
OFF_KVA = OFF_QA + Q_A_W
OFF_QB = OFF_KVA + KV_A_W
OFF_KB = OFF_QB + Q_B_W
OFF_VB = OFF_KB + KV_B_W
OFF_QC = OFF_VB + KV_B_W
OFF_KC = OFF_QC + C_W
OFF_VC = OFF_KC + C_W
OFF_GATE_A = OFF_VC + C_W
GATE_A_PAD = 256
Z_W = OFF_GATE_A + GATE_A_PAD
assert OFF_QA % (HPG_A * HD_A) == 0 and OFF_QB % Q_B_W == 0 and OFF_MERGE % D_MODEL == 0

MM_TM, MM_TN = 1024, 512
ROW_TM = 256
ROUTE_TM = 512
EXP_TM = 256
CMP_TQ = 256
ATT_TQ = 128
SLC_TK = 256


def _cparams(*sem):
    return pltpu.CompilerParams(dimension_semantics=sem, vmem_limit_bytes=VMEM_LIMIT)


def _dot(a, b, **kw):
    return jnp.dot(a, b, preferred_element_type=F32, **kw)


def _dot_nt(a, b, **kw):
    return lax.dot_general(a, b, (((1,), (1,)), ((), ())), preferred_element_type=F32, **kw)


def _iota(shape, dim):
    return lax.broadcasted_iota(jnp.int32, shape, dim)


def _pick_lane(x, lane_idx):
    lane = _iota(x.shape, 1)
    return jnp.sum(jnp.where(lane == lane_idx, x, 0.0), axis=-1, keepdims=True)


def _select_topk_mask(vals, k):
    lane = _iota(vals.shape, 1).astype(F32)
    sel = jnp.zeros(vals.shape, F32)
    for _ in range(k):
        m = jnp.max(vals, axis=-1, keepdims=True)
        idx = jnp.min(jnp.where(vals == m, lane, float(LANES)), axis=-1, keepdims=True)
        pick = lane == idx
        sel = jnp.where(pick, 1.0, sel)
        vals = jnp.where(pick, -jnp.inf, vals)
    return sel


def _mm_kernel(a_ref, b_ref, o_ref):
    o_ref[...] = _dot(a_ref[...], b_ref[...]).astype(o_ref.dtype)


def _matmul(a, b, out_dtype):
    m, k = a.shape
    n = b.shape[1]
    return pl.pallas_call(
        _mm_kernel,
        grid=(m // MM_TM, n // MM_TN),
        in_specs=[pl.BlockSpec((MM_TM, k), lambda i, j: (i, 0)),
                  pl.BlockSpec((k, MM_TN), lambda i, j: (0, j))],
        out_specs=pl.BlockSpec((MM_TM, MM_TN), lambda i, j: (i, j)),
        out_shape=jax.ShapeDtypeStruct((m, n), out_dtype),
        compiler_params=_cparams("parallel", "parallel"),
        name="in_proj",
    )(a, b)


def _cmp_kv_kernel(k_ref, pos_ref, w_ref, o_ref, kf_ref, *, seq):
    kf_ref[0:seq, :] = k_ref[...].astype(F32)
    kf_ref[seq:seq + LANES, :] = jnp.zeros((LANES, HD_A), F32)
    acc = jnp.zeros((LANES, HD_A), F32)
    for l in range(L_CMP):
        rows = kf_ref[pl.ds(l, LANES, stride=STRIDE_CMP), :] + pos_ref[0, l:l + 1, :]
        acc = acc + _dot(rows.astype(BF16), w_ref[0, l].astype(BF16))
    o_ref[0, 0] = acc.astype(o_ref.dtype)


def _cmp_kv(z, cmp_pos, cmp_w, batch, seq):
    blk0 = OFF_KVA // LANES
    return pl.pallas_call(
        functools.partial(_cmp_kv_kernel, seq=seq),
        grid=(batch, 2 * KV_A),
        in_specs=[pl.BlockSpec((seq, LANES), lambda b, j: (b, blk0 + j)),
                  pl.BlockSpec((1, L_CMP, HD_A), lambda b, j: (j // KV_A, 0, 0)),
                  pl.BlockSpec((1, L_CMP, HD_A, HD_A), lambda b, j: (j // KV_A, 0, 0, 0))],
        out_specs=pl.BlockSpec((1, 1, LANES, HD_A), lambda b, j: (b, j, 0, 0)),
        out_shape=jax.ShapeDtypeStruct((batch, 2 * KV_A, LANES, HD_A), BF16),
        scratch_shapes=[pltpu.VMEM((seq + LANES, HD_A), F32)],
        compiler_params=_cparams("parallel", "parallel"),
        name="nsa_cmp_kv",
    )(z, cmp_pos, cmp_w)


def _nsa_cmp_kernel(slopes_ref, q_ref, kc_ref, vc_ref, gate_ref, inter_ref, o_ref, sel_ref, *, tq):
    g = pl.program_id(1)
    i = pl.program_id(2)
    shape = (tq, LANES)
    t = i * tq + _iota(shape, 0)
    lane = _iota(shape, 1)
    dist_i = t - (lane * STRIDE_CMP + (L_CMP - 1))
    ok = dist_i >= 0
    okf = ok.astype(F32)
    dist = dist_i.astype(F32)
    kc = kc_ref[0, 0]
    vc = vc_ref[0, 0]
    sig = jax.nn.sigmoid(gate_ref[...].astype(F32))
    scale = HD_A ** -0.5
    psum = jnp.zeros(shape, F32)
    outs = []
    for h in range(HPG_A):
        q = q_ref[:, h * HD_A:(h + 1) * HD_A]
        s = _dot_nt(q, kc) * scale - slopes_ref[g * HPG_A + h] * dist
        s = jnp.where(ok, s, NEG_INF)
        e = jnp.exp(s - jnp.max(s, axis=-1, keepdims=True))
        p = e / jnp.sum(e, axis=-1, keepdims=True) * okf
        psum = psum + p
        o = _dot(p.astype(BF16), vc)
        outs.append(o * _pick_lane(sig, (g * HPG_A + h) * 3 + 0))
    o_ref[...] = jnp.concatenate(outs, axis=1).astype(o_ref.dtype)

    imp = _dot(psum, inter_ref[...], precision=HIGHEST)
    blk_t = t // L_SEL
    valid = lane <= blk_t
    forced = (lane == 0) | (lane == blk_t) | (lane == blk_t - 1)
    vals = jnp.where(forced, jnp.inf, jnp.where(valid, imp, -jnp.inf))
    sel = _select_topk_mask(vals, N_SEL)
    sel_ref[0, 0] = jnp.where(valid, sel, 0.0).astype(sel_ref.dtype)


def _nsa_cmp(z, kvc, slopes, inter, batch, seq):
    tq = CMP_TQ
    nq = seq // tq
    qw = HPG_A * HD_A
    gate_blk = OFF_GATE_A // LANES
    return pl.pallas_call(
        functools.partial(_nsa_cmp_kernel, tq=tq),
        grid=(batch, KV_A, nq),
        in_specs=[pl.BlockSpec(memory_space=pltpu.SMEM),
                  pl.BlockSpec((tq, qw), lambda b, g, i: (b * nq + i, OFF_QA // qw + g)),
                  pl.BlockSpec((1, 1, LANES, HD_A), lambda b, g, i: (b, g, 0, 0)),
                  pl.BlockSpec((1, 1, LANES, HD_A), lambda b, g, i: (b, KV_A + g, 0, 0)),
                  pl.BlockSpec((tq, LANES), lambda b, g, i: (b * nq + i, gate_blk)),
                  pl.BlockSpec((LANES, LANES), lambda b, g, i: (0, 0))],
        out_specs=[pl.BlockSpec((tq, qw), lambda b, g, i: (b * nq + i, g)),
                   pl.BlockSpec((1, 1, tq, LANES), lambda b, g, i: (b, g, i, 0))],
        out_shape=[jax.ShapeDtypeStruct((batch * seq, Q_A_W), BF16),
                   jax.ShapeDtypeStruct((batch, KV_A, seq, LANES), BF16)],
        compiler_params=_cparams("parallel", "parallel", "parallel"),
        name="nsa_cmp_attn",
    )(slopes, z, kvc, kvc, z, inter)


def _stack_heads(q_ref, n_heads, hd, col0=0):
    return jnp.concatenate([q_ref[:, col0 + h * hd:col0 + (h + 1) * hd] for h in range(n_heads)], axis=0)


def _head_column(values, n_heads, tq):
    hh = _iota((n_heads * tq, 1), 0) // tq
    col = jnp.full((n_heads * tq, 1), values[n_heads - 1], F32)
    for h in range(n_heads - 2, -1, -1):
        col = jnp.where(hh == h, values[h], col)
    return col


def _row_positions(n_heads, tq, t0):
    r = _iota((n_heads * tq, 1), 0)
    return t0 + (r - (r // tq) * tq)


def _nsa_slc_kernel(slopes_ref, q_ref, k_ref, v_ref, sel_ref, gate_ref, o_ref, *, tq, tk):
    g = pl.program_id(1)
    i = pl.program_id(2)
    rows = HPG_A * tq
    qs = _stack_heads(q_ref, HPG_A, HD_A)
    sel = sel_ref[0, 0]
    slope_col = _head_column([slopes_ref[g * HPG_A + h] for h in range(HPG_A)], HPG_A, tq)
    tpos = _row_positions(HPG_A, tq, i * tq)
    scale = HD_A ** -0.5
    blocks_per_tile = tk // L_SEL
    n_kt = (i * tq + tq + tk - 1) // tk

    def body(kt, carry):
        m, l, acc = carry
        k0 = pl.multiple_of(kt * tk, tk)
        k = k_ref[pl.ds(k0, tk), :]
        v = v_ref[pl.ds(k0, tk), :]
        s = _dot_nt(qs, k) * scale
        expand = (_iota((LANES, tk), 0) == kt * blocks_per_tile + _iota((LANES, tk), 1) // L_SEL)
        chosen = _dot(sel, expand.astype(BF16))
        chosen = jnp.concatenate([chosen] * HPG_A, axis=0)
        dist = tpos - (k0 + _iota((1, tk), 1))
        allowed = (chosen > 0.5) & (dist >= 0)
        s = jnp.where(allowed, s - slope_col * dist.astype(F32), NEG_INF)
        m_new = jnp.maximum(m, jnp.max(s, axis=-1, keepdims=True))
        alpha = jnp.exp(m - m_new)
        p = jnp.exp(s - m_new)
        l = alpha * l + jnp.sum(p, axis=-1, keepdims=True)
        acc = alpha * acc + _dot(p.astype(BF16), v)
        return m_new, l, acc

    init = (jnp.full((rows, 1), NEG_INF, F32), jnp.zeros((rows, 1), F32), jnp.zeros((rows, HD_A), F32))
    _, l, acc = lax.fori_loop(0, n_kt, body, init)
    o = acc / l
    sig = jax.nn.sigmoid(gate_ref[...].astype(F32))
    outs = [o[h * tq:(h + 1) * tq] * _pick_lane(sig, (g * HPG_A + h) * 3 + 1) for h in range(HPG_A)]
    o_ref[...] = jnp.concatenate(outs, axis=1).astype(o_ref.dtype)


def _nsa_slc(z, selmask, slopes, batch, seq):
    tq = ATT_TQ
    nq = seq // tq
    qw = HPG_A * HD_A
    kblk = OFF_KVA // LANES + 1 * 2 * KV_A
    vblk = kblk + KV_A
    gate_blk = OFF_GATE_A // LANES
    return pl.pallas_call(
        functools.partial(_nsa_slc_kernel, tq=tq, tk=SLC_TK),
        grid=(batch, KV_A, nq),
        in_specs=[pl.BlockSpec(memory_space=pltpu.SMEM),
                  pl.BlockSpec((tq, qw), lambda b, g, i: (b * nq + i, OFF_QA // qw + g)),
                  pl.BlockSpec((seq, LANES), lambda b, g, i: (b, kblk + g)),
                  pl.BlockSpec((seq, LANES), lambda b, g, i: (b, vblk + g)),
                  pl.BlockSpec((1, 1, tq, LANES), lambda b, g, i: (b, g, i, 0)),
                  pl.BlockSpec((tq, LANES), lambda b, g, i: (b * nq + i, gate_blk))],
        out_specs=pl.BlockSpec((tq, qw), lambda b, g, i: (b * nq + i, g)),
        out_shape=jax.ShapeDtypeStruct((batch * seq, Q_A_W), BF16),
        compiler_params=_cparams("parallel", "parallel", "parallel"),
        name="nsa_slc_attn",
    )(slopes, z, z, z, selmask, z)


def _nsa_win_kernel(slopes_ref, q_ref, k_ref, v_ref, gate_ref, o_ref, *, tq, window):
    g = pl.program_id(1)
    i = pl.program_id(2)
    span = window + tq
    start = pl.multiple_of(jnp.maximum(i * tq - window, 0), tq)
    qs = _stack_heads(q_ref, HPG_A, HD_A)
    slope_col = _head_column([slopes_ref[g * HPG_A + h] for h in range(HPG_A)], HPG_A, tq)
    tpos = _row_positions(HPG_A, tq, i * tq)
    k = k_ref[pl.ds(start, span), :]
    v = v_ref[pl.ds(start, span), :]
    dist = tpos - (start + _iota((1, span), 1))
    allowed = (dist >= 0) & (dist < window)
    s = _dot_nt(qs, k) * (HD_A ** -0.5)
    s = jnp.where(allowed, s - slope_col * dist.astype(F32), NEG_INF)
    p = jnp.exp(s - jnp.max(s, axis=-1, keepdims=True))
    o = _dot(p.astype(BF16), v) / jnp.sum(p, axis=-1, keepdims=True)
    sig = jax.nn.sigmoid(gate_ref[...].astype(F32))
    outs = [o[h * tq:(h + 1) * tq] * _pick_lane(sig, (g * HPG_A + h) * 3 + 2) for h in range(HPG_A)]
    o_ref[...] = jnp.concatenate(outs, axis=1).astype(o_ref.dtype)


def _nsa_win(z, slopes, batch, seq):
    tq = ATT_TQ
    nq = seq // tq
    qw = HPG_A * HD_A
    kblk = OFF_KVA // LANES + 2 * 2 * KV_A
    vblk = kblk + KV_A
    gate_blk = OFF_GATE_A // LANES
    return pl.pallas_call(
        functools.partial(_nsa_win_kernel, tq=tq, window=WIN_A),
        grid=(batch, KV_A, nq),
        in_specs=[pl.BlockSpec(memory_space=pltpu.SMEM),
                  pl.BlockSpec((tq, qw), lambda b, g, i: (b * nq + i, OFF_QA // qw + g)),
                  pl.BlockSpec((seq, LANES), lambda b, g, i: (b, kblk + g)),
                  pl.BlockSpec((seq, LANES), lambda b, g, i: (b, vblk + g)),
                  pl.BlockSpec((tq, LANES), lambda b, g, i: (b * nq + i, gate_blk))],
        out_specs=pl.BlockSpec((tq, qw), lambda b, g, i: (b * nq + i, g)),
        out_shape=jax.ShapeDtypeStruct((batch * seq, Q_A_W), BF16),
        compiler_params=_cparams("parallel", "parallel", "parallel"),
        name="nsa_win_attn",
    )(slopes, z, z, z, z)


def _swa_kernel(sink_ref, q_ref, k_ref, v_ref, o_ref, *, tq, window):
    i = pl.program_id(1)
    span = window + tq
    start = pl.multiple_of(jnp.maximum(i * tq - window, 0), tq)
    tpos = _row_positions(HPG_B, tq, i * tq)
    dist = tpos - (start + _iota((1, span), 1))
    allowed = (dist >= 0) & (dist < window)
    distf = dist.astype(F32)
    outs = []
    for g in range(KV_B):
        qs = _stack_heads(q_ref, HPG_B, HD_B, col0=g * HPG_B * HD_B)
        k = k_ref[pl.ds(start, span), g * HD_B:(g + 1) * HD_B]
        v = v_ref[pl.ds(start, span), g * HD_B:(g + 1) * HD_B]
        slope_col = _head_column([2.0 ** (-8.0 * (g * HPG_B + h + 1) / H_B) for h in range(HPG_B)], HPG_B, tq)
        sink_col = _head_column([sink_ref[g * HPG_B + h] for h in range(HPG_B)], HPG_B, tq)
        s = _dot_nt(qs, k) * (HD_B ** -0.5)
        s = jnp.where(allowed, s - slope_col * distf, NEG_INF)
        m = jnp.maximum(jnp.max(s, axis=-1, keepdims=True), sink_col)
        p = jnp.exp(s - m)
        denom = jnp.sum(p, axis=-1, keepdims=True) + jnp.exp(sink_col - m)
        o = _dot(p.astype(BF16), v) / denom
        outs += [o[h * tq:(h + 1) * tq] for h in range(HPG_B)]
    o_ref[...] = jnp.concatenate(outs, axis=1).astype(o_ref.dtype)


def _swa(z, sink, batch, seq):
    tq = ATT_TQ
    nq = seq // tq
    return pl.pallas_call(
        functools.partial(_swa_kernel, tq=tq, window=WIN_B),
        grid=(batch, nq),
        in_specs=[pl.BlockSpec(memory_space=pltpu.SMEM),
                  pl.BlockSpec((tq, Q_B_W), lambda b, i: (b * nq + i, OFF_QB // Q_B_W)),
                  pl.BlockSpec((seq, KV_B_W), lambda b, i: (b, OFF_KB // KV_B_W)),
                  pl.BlockSpec((seq, KV_B_W), lambda b, i: (b, OFF_VB // KV_B_W))],
        out_specs=pl.BlockSpec((tq, Q_B_W), lambda b, i: (b * nq + i, 0)),
        out_shape=jax.ShapeDtypeStruct((batch * seq, Q_B_W), BF16),
        compiler_params=_cparams("parallel", "parallel"),
        name="swa_attn",
    )(sink, z, z, z)


def _moba_kernel(slopes_ref, q_ref, k_ref, v_ref, o_ref, *, seq):
    h = pl.program_id(1)
    i = pl.program_id(2)
    tq = BLK_C
    q = q_ref[...]
    slope = slopes_ref[h]
    scale = HD_C ** -0.5

    avg = jnp.where(_iota((LANES, seq), 1) // BLK_C == _iota((LANES, seq), 0), 1.0 / BLK_C, 0.0)
    k_mean = _dot(avg.astype(BF16), k_ref[...])
    score = _dot_nt(q.astype(F32), k_mean, precision=HIGHEST)
    lane = _iota((tq, LANES), 1)
    past = lane < i
    sel = _select_topk_mask(jnp.where(past, score, -jnp.inf), TOPK_C)
    sel = jnp.where(past, sel, 0.0)
    t_row = i * tq + _iota((tq, 1), 0)

    def body(kt, carry):
        m, l, acc = carry
        k0 = pl.multiple_of(kt * BLK_C, BLK_C)
        k = k_ref[pl.ds(k0, BLK_C), :]
        v = v_ref[pl.ds(k0, BLK_C), :]
        s = _dot_nt(q, k) * scale
        dist = t_row - (k0 + _iota((1, BLK_C), 1))
        chosen = _pick_lane(sel, kt)
        thresh = jnp.where(kt == i, 0, jnp.where(chosen > 0.5, -seq, 2 * seq))
        s = jnp.where(dist >= thresh, s - slope * dist.astype(F32), NEG_INF)
        m_new = jnp.maximum(m, jnp.max(s, axis=-1, keepdims=True))
        alpha = jnp.exp(m - m_new)
        p = jnp.exp(s - m_new)
        l = alpha * l + jnp.sum(p, axis=-1, keepdims=True)
        acc = alpha * acc + _dot(p.astype(BF16), v)
        return m_new, l, acc

    init = (jnp.full((tq, 1), NEG_INF, F32), jnp.zeros((tq, 1), F32), jnp.zeros((tq, HD_C), F32))
    _, l, acc = lax.fori_loop(0, i + 1, body, init)
    o_ref[...] = (acc / l).astype(o_ref.dtype)


def _moba(z, slopes, batch, seq):
    nq = seq // BLK_C
    qblk, kblk, vblk = OFF_QC // LANES, OFF_KC // LANES, OFF_VC // LANES
    return pl.pallas_call(
        functools.partial(_moba_kernel, seq=seq),
        grid=(batch, H_C, nq),
        in_specs=[pl.BlockSpec(memory_space=pltpu.SMEM),
                  pl.BlockSpec((BLK_C, HD_C), lambda b, h, i: (b * nq + i, qblk + h)),
                  pl.BlockSpec((seq, HD_C), lambda b, h, i: (b, kblk + h)),
                  pl.BlockSpec((seq, HD_C), lambda b, h, i: (b, vblk + h))],
        out_specs=pl.BlockSpec((BLK_C, HD_C), lambda b, h, i: (b * nq + i, h)),
        out_shape=jax.ShapeDtypeStruct((batch * seq, C_W), BF16),
        compiler_params=_cparams("parallel", "parallel", "parallel"),
        name="moba_attn",
    )(slopes, z, z, z)


def _merge_kernel(oc_ref, os_ref, ow_ref, ob_ref, om_ref, g0_ref, g1_ref, g2_ref, wa_ref, wb_ref, wc_ref, o_ref):
    o_a = (oc_ref[...].astype(F32) + os_ref[...].astype(F32) + ow_ref[...].astype(F32)).astype(BF16)
    merged = jax.nn.sigmoid(g0_ref[...].astype(F32)) * _dot(o_a, wa_ref[...])
    merged = merged + jax.nn.sigmoid(g1_ref[...].astype(F32)) * _dot(ob_ref[...], wb_ref[...])
    merged = merged + jax.nn.sigmoid(g2_ref[...].astype(F32)) * _dot(om_ref[...], wc_ref[...])
    o_ref[...] = merged.astype(o_ref.dtype)


def _merge(o_cmp, o_slc, o_win, o_b, o_c, z, wa, wb, wc):
    tm = ROW_TM
    t = z.shape[0]
    gblk = OFF_MERGE // D_MODEL
    row = lambda w: pl.BlockSpec((tm, w), lambda i: (i, 0))
    full = lambda a: pl.BlockSpec(a.shape, lambda i: (0, 0))
    gate = lambda r: pl.BlockSpec((tm, D_MODEL), lambda i: (i, gblk + r))
    return pl.pallas_call(
        _merge_kernel,
        grid=(t // tm,),
        in_specs=[row(Q_A_W), row(Q_A_W), row(Q_A_W), row(Q_B_W), row(C_W),
                  gate(0), gate(1), gate(2), full(wa), full(wb), full(wc)],
        out_specs=row(D_MODEL),
        out_shape=jax.ShapeDtypeStruct((t, D_MODEL), BF16),
        compiler_params=_cparams("parallel"),
        name="mixer_merge",
    )(o_cmp, o_slc, o_win, o_b, o_c, z, z, z, wa, wb, wc)


def _layer_norm(h, g_ref, b_ref):
    mu = jnp.mean(h, axis=-1, keepdims=True)
    xc = h - mu
    var = jnp.mean(xc * xc, axis=-1, keepdims=True)
    return xc * lax.rsqrt(var + LN_EPS) * g_ref[...] + b_ref[...]


def _proj_ln_kernel(m_ref, w_ref, x_ref, g_ref, b_ref, xo_ref, xb_ref):
    y = _dot(m_ref[...], w_ref[...])
    out = _layer_norm(ALPHA * x_ref[...] + y, g_ref, b_ref)
    xo_ref[...] = out
    xb_ref[...] = out.astype(BF16)


def _proj_ln(merged, w_out, x, g, b):
    tm = ROW_TM
    t = x.shape[0]
    row = pl.BlockSpec((tm, D_MODEL), lambda i: (i, 0))
    vec = pl.BlockSpec((1, D_MODEL), lambda i: (0, 0))
    return pl.pallas_call(
        _proj_ln_kernel,
        grid=(t // tm,),
        in_specs=[row, pl.BlockSpec((D_MODEL, D_MODEL), lambda i: (0, 0)), row, vec, vec],
        out_specs=[row, row],
        out_shape=[jax.ShapeDtypeStruct((t, D_MODEL), F32), jax.ShapeDtypeStruct((t, D_MODEL), BF16)],
        compiler_params=_cparams("parallel"),
        name="out_proj_ln",
    )(merged, w_out, x, g, b)


def _moe_ln_kernel(x_ref, ya_ref, yb_ref, w_ref, g_ref, b_ref, xo_ref, xb_ref):
    w = w_ref[...]
    y = w[:, 0:1] * ya_ref[...].astype(F32) + w[:, 1:2] * yb_ref[...].astype(F32)
    out = _layer_norm(ALPHA * x_ref[...] + y, g_ref, b_ref)
    xo_ref[...] = out
    xb_ref[...] = out.astype(BF16)


def _moe_ln(x, ya, yb, wts, g, b):
    tm = ROW_TM
    t = x.shape[0]
    row = pl.BlockSpec((tm, D_MODEL), lambda i: (i, 0))
    vec = pl.BlockSpec((1, D_MODEL), lambda i: (0, 0))
    return pl.pallas_call(
        _moe_ln_kernel,
        grid=(t // tm,),
        in_specs=[row, row, row, pl.BlockSpec((tm, LANES), lambda i: (i, 0)), vec, vec],
        out_specs=[row, row],
        out_shape=[jax.ShapeDtypeStruct((t, D_MODEL), F32), jax.ShapeDtypeStruct((t, D_MODEL), BF16)],
        compiler_params=_cparams("parallel"),
        name="moe_combine_ln",
    )(x, ya, yb, wts, g, b)


def _router_kernel(x_ref, w_ref, b_ref, id_ref, wt_ref):
    logits = _dot(x_ref[...], w_ref[...], precision=HIGHEST) + b_ref[...]
    lane = _iota(logits.shape, 1).astype(F32)
    first = lambda hit: jnp.min(jnp.where(hit, lane, float(LANES)), axis=-1, keepdims=True)
    gl = jnp.where(lane < N_GROUPS, logits, -jnp.inf)
    gm = jnp.max(gl, axis=-1, keepdims=True)
    g_w = 1.0 / jnp.sum(jnp.exp(gl - gm), axis=-1, keepdims=True)
    lo = N_GROUPS + first(gl == gm) * EXP_PER_GROUP
    el = jnp.where((lane >= lo) & (lane < lo + EXP_PER_GROUP), logits, -jnp.inf)
    m1 = jnp.max(el, axis=-1, keepdims=True)
    i1 = first(el == m1)
    el2 = jnp.where(lane == i1, -jnp.inf, el)
    m2 = jnp.max(el2, axis=-1, keepdims=True)
    i2 = first(el2 == m2)
    e2 = jnp.exp(m2 - m1)
    w1 = g_w / (1.0 + e2)
    w2 = g_w * e2 / (1.0 + e2)
    ids = jnp.where(lane == 0.0, i1 - N_GROUPS, jnp.where(lane == 1.0, i2 - N_GROUPS, 0.0))
    id_ref[...] = ids.astype(jnp.int32)
    wt_ref[...] = jnp.where(lane == 0.0, w1, jnp.where(lane == 1.0, w2, 0.0))


def _router(x, w_rt, b_rt):
    tm = ROUTE_TM
    t = x.shape[0]
    return pl.pallas_call(
        _router_kernel,
        grid=(t // tm,),
        in_specs=[pl.BlockSpec((tm, D_MODEL), lambda i: (i, 0)),
                  pl.BlockSpec((D_MODEL, LANES), lambda i: (0, 0)),
                  pl.BlockSpec((1, LANES), lambda i: (0, 0))],
        out_specs=[pl.BlockSpec((tm, LANES), lambda i: (i, 0)), pl.BlockSpec((tm, LANES), lambda i: (i, 0))],
        out_shape=[jax.ShapeDtypeStruct((t, LANES), jnp.int32), jax.ShapeDtypeStruct((t, LANES), F32)],
        compiler_params=_cparams("parallel"),
        name="moe_router",
    )(x, w_rt, b_rt)


def _expert_kernel(te_ref, na_ref, x_ref, wg_ref, wu_ref, wd_ref, y_ref):
    j = pl.program_id(0)

    @pl.when(j < na_ref[0])
    def _():
        x = x_ref[...]
        hg = _dot(x, wg_ref[0])
        hu = _dot(x, wu_ref[0])
        a = hg * jax.nn.sigmoid(hg) * hu
        y_ref[...] = _dot(a.astype(BF16), wd_ref[0]).astype(y_ref.dtype)

    @pl.when(j >= na_ref[0])
    def _():
        y_ref[...] = jnp.zeros(y_ref.shape, y_ref.dtype)


def _experts(x_sorted, tile_expert, n_active, wg, wu, wd):
    tm = EXP_TM
    n_pad = x_sorted.shape[0]
    n_tiles = n_pad // tm
    grid_spec = pltpu.PrefetchScalarGridSpec(
        num_scalar_prefetch=2,
        grid=(n_tiles,),
        in_specs=[pl.BlockSpec((tm, D_MODEL), lambda j, te, na: (jnp.minimum(j, na[0] - 1), 0)),
                  pl.BlockSpec((1, D_MODEL, D_EXPERT), lambda j, te, na: (te[j], 0, 0)),
                  pl.BlockSpec((1, D_MODEL, D_EXPERT), lambda j, te, na: (te[j], 0, 0)),
                  pl.BlockSpec((1, D_EXPERT, D_MODEL), lambda j, te, na: (te[j], 0, 0))],
        out_specs=pl.BlockSpec((tm, D_MODEL), lambda j, te, na: (j, 0)),
    )
    return pl.pallas_call(
        _expert_kernel,
        grid_spec=grid_spec,
        out_shape=jax.ShapeDtypeStruct((n_pad, D_MODEL), BF16),
        compiler_params=_cparams("arbitrary"),
        name="moe_experts",
    )(tile_expert, n_active, x_sorted, wg, wu, wd)


def _dispatch_plan(expert_ids, tm, n_tiles):
    t = expert_ids.shape[0]
    e_flat = expert_ids.reshape(-1)
    onehot = (e_flat[:, None] == jnp.arange(N_EXPERTS, dtype=jnp.int32)[None, :]).astype(jnp.int32)
    csum = jnp.cumsum(onehot, axis=0)
    rank = jnp.take_along_axis(csum, e_flat[:, None], axis=1)[:, 0] - 1
    counts = csum[-1]
    padded = ((counts + tm - 1) // tm) * tm
    ends = jnp.cumsum(padded)
    dest = (ends - padded)[e_flat] + rank
    src_tok = jnp.zeros((n_tiles * tm,), jnp.int32).at[dest].set(jnp.arange(2 * t, dtype=jnp.int32) // 2)
    n_active = ends[-1] // tm
    tile_idx = jnp.arange(n_tiles, dtype=jnp.int32)
    te = jnp.searchsorted(ends, jnp.minimum(tile_idx, n_active - 1) * tm, side="right").astype(jnp.int32)
    te = jnp.minimum(te, N_EXPERTS - 1)
    return dest.reshape(t, 2), src_tok, te, n_active.reshape(1).astype(jnp.int32)


def _alibi(n):
    return jnp.asarray([2.0 ** (-8.0 * (i + 1) / n) for i in range(n)], F32)


def _selection_overlap(seq):
    n_cmp = (seq - L_CMP) // STRIDE_CMP + 1
    n_sel = seq // L_SEL
    c_start = STRIDE_CMP * np.arange(n_cmp)
    s_start = L_SEL * np.arange(n_sel)
    inter = np.clip(np.minimum(c_start[:, None] + L_CMP, s_start[None, :] + L_SEL)
                    - np.maximum(c_start[:, None], s_start[None, :]), 0, None) / L_CMP
    out = np.zeros((LANES, LANES), np.float32)
    out[:n_cmp, :n_sel] = inter
    return jnp.asarray(out)


def _permute_w_in(w):
    o_gate = Q_A_W + KV_A_W
    o_merge = w.shape[1] - MERGE_W
    pad = jnp.zeros((w.shape[0], GATE_A_PAD - GATE_A_W), w.dtype)
    return jnp.concatenate([w[:, o_merge:], w[:, :o_gate], w[:, o_gate + GATE_A_W:o_merge],
                            w[:, o_gate:o_gate + GATE_A_W], pad], axis=1).astype(BF16)


def kernel(x, w_in, nsa_cmp_pos, nsa_cmp_w, sink_b, w_br_a, w_br_b, w_br_c, w_out, ln1_g, ln1_b,
           w_group, b_group, w_router, b_router, w_gate, w_up, w_down, ln2_g, ln2_b):
    batch, seq, d = x.shape
    t = batch * seq
    assert d == D_MODEL and seq % BLK_C == 0 and seq // L_SEL <= LANES and t % MM_TM == 0
    assert (seq - L_CMP) // STRIDE_CMP + 1 < LANES and seq // BLK_C <= LANES
    n_tiles = (TOPK_EXPERT * t) // EXP_TM + N_EXPERTS

    slopes_a = _alibi(H_A)
    slopes_c = _alibi(H_C)
    inter = _selection_overlap(seq)

    xf = x.reshape(t, d)
    xb = xf.astype(BF16)
    for l in range(DEPTH):
        z = _matmul(xb, _permute_w_in(w_in[l]), BF16)
        kvc = _cmp_kv(z, nsa_cmp_pos[l], nsa_cmp_w[l], batch, seq)
        o_cmp, selmask = _nsa_cmp(z, kvc, slopes_a, inter, batch, seq)
        o_slc = _nsa_slc(z, selmask, slopes_a, batch, seq)
        o_win = _nsa_win(z, slopes_a, batch, seq)
        o_b = _swa(z, sink_b[l], batch, seq)
        o_c = _moba(z, slopes_c, batch, seq)
        merged = _merge(o_cmp, o_slc, o_win, o_b, o_c, z,
                        w_br_a[l].astype(BF16), w_br_b[l].astype(BF16), w_br_c[l].astype(BF16))
        xf, xb = _proj_ln(merged, w_out[l].astype(BF16), xf, ln1_g[l][None, :], ln1_b[l][None, :])

        w_rt = jnp.concatenate([w_group[l], w_router[l],
                                jnp.zeros((d, LANES - N_GROUPS - N_EXPERTS), F32)], axis=1)
        b_rt = jnp.concatenate([b_group[l], b_router[l],
                                jnp.zeros((LANES - N_GROUPS - N_EXPERTS,), F32)])[None, :]
        ids, wts = _router(xf, w_rt, b_rt)
        dest, src_tok, tile_expert, n_active = _dispatch_plan(ids[:, :TOPK_EXPERT], EXP_TM, n_tiles)
        x_sorted = jnp.take(xb, src_tok, axis=0)
        y = _experts(x_sorted, tile_expert, n_active,
                     w_gate[l].astype(BF16), w_up[l].astype(BF16), w_down[l].astype(BF16))
        ya = jnp.take(y, dest[:, 0], axis=0)
        yb = jnp.take(y, dest[:, 1], axis=0)
        xf, xb = _moe_ln(xf, ya, yb, wts, ln2_g[l][None, :], ln2_b[l][None, :])
    return xf.reshape(batch, seq, d)
```

```python
import functools

import numpy as np
import jax
import jax.numpy as jnp
from jax import lax
from jax.experimental import pallas as pl
from jax.experimental.pallas import tpu as pltpu
from jax.experimental.compute_on import compute_on

F32 = jnp.float32
BF16 = jnp.bfloat16
HIGHEST = lax.Precision.HIGHEST

D_MODEL = 2048
DEPTH = 2
H_A, KV_A, HD_A = 8, 2, 128
HPG_A = H_A // KV_A
L_CMP, STRIDE_CMP, L_SEL, N_SEL, WIN_A = 32, 16, 64, 8, 512
H_B, KV_B, HD_B, WIN_B = 8, 2, 64, 128
HPG_B = H_B // KV_B
H_C, HD_C, BLK_C, TOPK_C = 4, 128, 256, 3
N_GROUPS, EXP_PER_GROUP, D_EXPERT, TOPK_EXPERT = 4, 8, 256, 2
N_EXPERTS = N_GROUPS * EXP_PER_GROUP
LN_EPS = 1e-5
NEG_INF = -1e30
ALPHA = (2.0 * DEPTH) ** 0.25

LANES = 128
VMEM_LIMIT = 48 * 1024 * 1024

Q_A_W = H_A * HD_A
KV_A_W = 3 * 2 * KV_A * HD_A
GATE_A_W = 3 * H_A
Q_B_W = H_B * HD_B
KV_B_W = KV_B * HD_B
C_W = H_C * HD_C
MERGE_W = 3 * D_MODEL
OFF_MERGE = 0
OFF_QA = OFF_MERGE + MERGE_W
OFF_KVA = OFF_QA + Q_A_W
OFF_QB = OFF_KVA + KV_A_W
OFF_KB = OFF_QB + Q_B_W
OFF_VB = OFF_KB + KV_B_W
OFF_QC = OFF_VB + KV_B_W
OFF_KC = OFF_QC + C_W
OFF_VC = OFF_KC + C_W
OFF_GATE_A = OFF_VC + C_W
GATE_A_PAD = 256
Z_W = OFF_GATE_A + GATE_A_PAD
assert OFF_QA % (HPG_A * HD_A) == 0 and OFF_QB % Q_B_W == 0 and OFF_MERGE % D_MODEL == 0

MM_TM, MM_TN = 1024, 512
ROW_TM = 256
ROUTE_TM = 512
EXP_TM = 256
EXP_CHUNKS = 3
CMP_TQ = 256
ATT_TQ = 128
SLC_TK = 256


def _cparams(*sem):
    return pltpu.CompilerParams(dimension_semantics=sem, vmem_limit_bytes=VMEM_LIMIT)


def _dot(a, b, **kw):
    return jnp.dot(a, b, preferred_element_type=F32, **kw)


def _dot_nt(a, b, **kw):
    return lax.dot_general(a, b, (((1,), (1,)), ((), ())), preferred_element_type=F32, **kw)


def _iota(shape, dim):
    return lax.broadcasted_iota(jnp.int32, shape, dim)


def _pick_lane(x, lane_idx):
    lane = _iota(x.shape, 1)
    return jnp.sum(jnp.where(lane == lane_idx, x, 0.0), axis=-1, keepdims=True)


def _select_topk_mask(vals, k):
    lane = _iota(vals.shape, 1).astype(F32)
    sel = jnp.zeros(vals.shape, F32)
    for _ in range(k):
        m = jnp.max(vals, axis=-1, keepdims=True)
        idx = jnp.min(jnp.where(vals == m, lane, float(LANES)), axis=-1, keepdims=True)
        pick = lane == idx
        sel = jnp.where(pick, 1.0, sel)
        vals = jnp.where(pick, -jnp.inf, vals)
    return sel


def _mm_kernel(a_ref, b_ref, o_ref):
    o_ref[...] = _dot(a_ref[...], b_ref[...]).astype(o_ref.dtype)


def _matmul(a, b, out_dtype):
    m, k = a.shape
    n = b.shape[1]
    return pl.pallas_call(
        _mm_kernel,
        grid=(m // MM_TM, n // MM_TN),
        in_specs=[pl.BlockSpec((MM_TM, k), lambda i, j: (i, 0)),
                  pl.BlockSpec((k, MM_TN), lambda i, j: (0, j))],
        out_specs=pl.BlockSpec((MM_TM, MM_TN), lambda i, j: (i, j)),
        out_shape=jax.ShapeDtypeStruct((m, n), out_dtype),
        compiler_params=_cparams("parallel", "parallel"),
        name="in_proj",
    )(a, b)


def _cmp_kv_kernel(k_ref, pos_ref, w_ref, o_ref, kf_ref, *, seq):
    kf_ref[0:seq, :] = k_ref[...].astype(F32)
    kf_ref[seq:seq + LANES, :] = jnp.zeros((LANES, HD_A), F32)
    acc = jnp.zeros((LANES, HD_A), F32)
    for l in range(L_CMP):
        rows = kf_ref[pl.ds(l, LANES, stride=STRIDE_CMP), :] + pos_ref[0, l:l + 1, :]
        acc = acc + _dot(rows.astype(BF16), w_ref[0, l].astype(BF16))
    o_ref[0, 0] = acc.astype(o_ref.dtype)


def _cmp_kv(z, cmp_pos, cmp_w, batch, seq):
    blk0 = OFF_KVA // LANES
    return pl.pallas_call(
        functools.partial(_cmp_kv_kernel, seq=seq),
        grid=(batch, 2 * KV_A),
        in_specs=[pl.BlockSpec((seq, LANES), lambda b, j: (b, blk0 + j)),
                  pl.BlockSpec((1, L_CMP, HD_A), lambda b, j: (j // KV_A, 0, 0)),
                  pl.BlockSpec((1, L_CMP, HD_A, HD_A), lambda b, j: (j // KV_A, 0, 0, 0))],
        out_specs=pl.BlockSpec((1, 1, LANES, HD_A), lambda b, j: (b, j, 0, 0)),
        out_shape=jax.ShapeDtypeStruct((batch, 2 * KV_A, LANES, HD_A), BF16),
        scratch_shapes=[pltpu.VMEM((seq + LANES, HD_A), F32)],
        compiler_params=_cparams("parallel", "parallel"),
        name="nsa_cmp_kv",
    )(z, cmp_pos, cmp_w)


def _nsa_cmp_kernel(slopes_ref, q_ref, kc_ref, vc_ref, gate_ref, inter_ref, o_ref, sel_ref, *, tq):
    g = pl.program_id(1)
    i = pl.program_id(2)
    shape = (tq, LANES)
    t = i * tq + _iota(shape, 0)
    lane = _iota(shape, 1)
    dist_i = t - (lane * STRIDE_CMP + (L_CMP - 1))
    ok = dist_i >= 0
    okf = ok.astype(F32)
    dist = dist_i.astype(F32)
    kc = kc_ref[0, 0]
    vc = vc_ref[0, 0]
    sig = jax.nn.sigmoid(gate_ref[...].astype(F32))
    scale = HD_A ** -0.5
    psum = jnp.zeros(shape, F32)
    outs = []
    for h in range(HPG_A):
        q = q_ref[:, h * HD_A:(h + 1) * HD_A]
        s = _dot_nt(q, kc) * scale - slopes_ref[g * HPG_A + h] * dist
        s = jnp.where(ok, s, NEG_INF)
        e = jnp.exp(s - jnp.max(s, axis=-1, keepdims=True))
        p = e / jnp.sum(e, axis=-1, keepdims=True) * okf
        psum = psum + p
        o = _dot(p.astype(BF16), vc)
        outs.append(o * _pick_lane(sig, (g * HPG_A + h) * 3 + 0))
    o_ref[...] = jnp.concatenate(outs, axis=1).astype(o_ref.dtype)

    imp = _dot(psum, inter_ref[...], precision=HIGHEST)
    blk_t = t // L_SEL
    valid = lane <= blk_t
    forced = (lane == 0) | (lane == blk_t) | (lane == blk_t - 1)
    vals = jnp.where(forced, jnp.inf, jnp.where(valid, imp, -jnp.inf))
    sel = _select_topk_mask(vals, N_SEL)
    sel_ref[0, 0] = jnp.where(valid, sel, 0.0).astype(sel_ref.dtype)


def _nsa_cmp(z, kvc, slopes, inter, batch, seq):
    tq = CMP_TQ
    nq = seq // tq
    qw = HPG_A * HD_A
    gate_blk = OFF_GATE_A // LANES
    return pl.pallas_call(
        functools.partial(_nsa_cmp_kernel, tq=tq),
        grid=(batch, KV_A, nq),
        in_specs=[pl.BlockSpec(memory_space=pltpu.SMEM),
                  pl.BlockSpec((tq, qw), lambda b, g, i: (b * nq + i, OFF_QA // qw + g)),
                  pl.BlockSpec((1, 1, LANES, HD_A), lambda b, g, i: (b, g, 0, 0)),
                  pl.BlockSpec((1, 1, LANES, HD_A), lambda b, g, i: (b, KV_A + g, 0, 0)),
                  pl.BlockSpec((tq, LANES), lambda b, g, i: (b * nq + i, gate_blk)),
                  pl.BlockSpec((LANES, LANES), lambda b, g, i: (0, 0))],
        out_specs=[pl.BlockSpec((tq, qw), lambda b, g, i: (b * nq + i, g)),
                   pl.BlockSpec((1, 1, tq, LANES), lambda b, g, i: (b, g, i, 0))],
        out_shape=[jax.ShapeDtypeStruct((batch * seq, Q_A_W), BF16),
                   jax.ShapeDtypeStruct((batch, KV_A, seq, LANES), BF16)],
        compiler_params=_cparams("parallel", "parallel", "parallel"),
        name="nsa_cmp_attn",
    )(slopes, z, kvc, kvc, z, inter)


def _stack_heads(q_ref, n_heads, hd, col0=0):
    return jnp.concatenate([q_ref[:, col0 + h * hd:col0 + (h + 1) * hd] for h in range(n_heads)], axis=0)


def _head_column(values, n_heads, tq):
    hh = _iota((n_heads * tq, 1), 0) // tq
    col = jnp.full((n_heads * tq, 1), values[n_heads - 1], F32)
    for h in range(n_heads - 2, -1, -1):
        col = jnp.where(hh == h, values[h], col)
    return col


def _row_positions(n_heads, tq, t0):
    r = _iota((n_heads * tq, 1), 0)
    return t0 + (r - (r // tq) * tq)


def _nsa_slc_kernel(slopes_ref, q_ref, k_ref, v_ref, sel_ref, gate_ref, o_ref, *, tq, tk):
    g = pl.program_id(1)
    i = pl.program_id(2)
    rows = HPG_A * tq
    qs = _stack_heads(q_ref, HPG_A, HD_A)
    sel = sel_ref[0, 0]
    slope_col = _head_column([slopes_ref[g * HPG_A + h] for h in range(HPG_A)], HPG_A, tq)
    tpos = _row_positions(HPG_A, tq, i * tq)
    scale = HD_A ** -0.5
    blocks_per_tile = tk // L_SEL
    n_kt = (i * tq + tq + tk - 1) // tk

    def body(kt, carry):
        m, l, acc = carry
        k0 = pl.multiple_of(kt * tk, tk)
        k = k_ref[pl.ds(k0, tk), :]
        v = v_ref[pl.ds(k0, tk), :]
        s = _dot_nt(qs, k) * scale
        expand = (_iota((LANES, tk), 0) == kt * blocks_per_tile + _iota((LANES, tk), 1) // L_SEL)
        chosen = _dot(sel, expand.astype(BF16))
        chosen = jnp.concatenate([chosen] * HPG_A, axis=0)
        dist = tpos - (k0 + _iota((1, tk), 1))
        allowed = (chosen > 0.5) & (dist >= 0)
        s = jnp.where(allowed, s - slope_col * dist.astype(F32), NEG_INF)
        m_new = jnp.maximum(m, jnp.max(s, axis=-1, keepdims=True))
        alpha = jnp.exp(m - m_new)
        p = jnp.exp(s - m_new)
        l = alpha * l + jnp.sum(p, axis=-1, keepdims=True)
        acc = alpha * acc + _dot(p.astype(BF16), v)
        return m_new, l, acc

    init = (jnp.full((rows, 1), NEG_INF, F32), jnp.zeros((rows, 1), F32), jnp.zeros((rows, HD_A), F32))
    _, l, acc = lax.fori_loop(0, n_kt, body, init)
    o = acc / l
    sig = jax.nn.sigmoid(gate_ref[...].astype(F32))
    outs = [o[h * tq:(h + 1) * tq] * _pick_lane(sig, (g * HPG_A + h) * 3 + 1) for h in range(HPG_A)]
    o_ref[...] = jnp.concatenate(outs, axis=1).astype(o_ref.dtype)


def _nsa_slc(z, selmask, slopes, batch, seq):
    tq = ATT_TQ
    nq = seq // tq
    qw = HPG_A * HD_A
    kblk = OFF_KVA // LANES + 1 * 2 * KV_A
    vblk = kblk + KV_A
    gate_blk = OFF_GATE_A // LANES
    return pl.pallas_call(
        functools.partial(_nsa_slc_kernel, tq=tq, tk=SLC_TK),
        grid=(batch, KV_A, nq),
        in_specs=[pl.BlockSpec(memory_space=pltpu.SMEM),
                  pl.BlockSpec((tq, qw), lambda b, g, i: (b * nq + i, OFF_QA // qw + g)),
                  pl.BlockSpec((seq, LANES), lambda b, g, i: (b, kblk + g)),
                  pl.BlockSpec((seq, LANES), lambda b, g, i: (b, vblk + g)),
                  pl.BlockSpec((1, 1, tq, LANES), lambda b, g, i: (b, g, i, 0)),
                  pl.BlockSpec((tq, LANES), lambda b, g, i: (b * nq + i, gate_blk))],
        out_specs=pl.BlockSpec((tq, qw), lambda b, g, i: (b * nq + i, g)),
        out_shape=jax.ShapeDtypeStruct((batch * seq, Q_A_W), BF16),
        compiler_params=_cparams("parallel", "parallel", "parallel"),
        name="nsa_slc_attn",
    )(slopes, z, z, z, selmask, z)


def _nsa_win_kernel(slopes_ref, q_ref, k_ref, v_ref, gate_ref, o_ref, *, tq, window):
    g = pl.program_id(1)
    i = pl.program_id(2)
    span = window + tq
    start = pl.multiple_of(jnp.maximum(i * tq - window, 0), tq)
    qs = _stack_heads(q_ref, HPG_A, HD_A)
    slope_col = _head_column([slopes_ref[g * HPG_A + h] for h in range(HPG_A)], HPG_A, tq)
    tpos = _row_positions(HPG_A, tq, i * tq)
    k = k_ref[pl.ds(start, span), :]
    v = v_ref[pl.ds(start, span), :]
    dist = tpos - (start + _iota((1, span), 1))
    allowed = (dist >= 0) & (dist < window)
    s = _dot_nt(qs, k) * (HD_A ** -0.5)
    s = jnp.where(allowed, s - slope_col * dist.astype(F32), NEG_INF)
    p = jnp.exp(s - jnp.max(s, axis=-1, keepdims=True))
    o = _dot(p.astype(BF16), v) / jnp.sum(p, axis=-1, keepdims=True)
    sig = jax.nn.sigmoid(gate_ref[...].astype(F32))
    outs = [o[h * tq:(h + 1) * tq] * _pick_lane(sig, (g * HPG_A + h) * 3 + 2) for h in range(HPG_A)]
    o_ref[...] = jnp.concatenate(outs, axis=1).astype(o_ref.dtype)


def _nsa_win(z, slopes, batch, seq):
    tq = ATT_TQ
    nq = seq // tq
    qw = HPG_A * HD_A
    kblk = OFF_KVA // LANES + 2 * 2 * KV_A
    vblk = kblk + KV_A
    gate_blk = OFF_GATE_A // LANES
    return pl.pallas_call(
        functools.partial(_nsa_win_kernel, tq=tq, window=WIN_A),
        grid=(batch, KV_A, nq),
        in_specs=[pl.BlockSpec(memory_space=pltpu.SMEM),
                  pl.BlockSpec((tq, qw), lambda b, g, i: (b * nq + i, OFF_QA // qw + g)),
                  pl.BlockSpec((seq, LANES), lambda b, g, i: (b, kblk + g)),
                  pl.BlockSpec((seq, LANES), lambda b, g, i: (b, vblk + g)),
                  pl.BlockSpec((tq, LANES), lambda b, g, i: (b * nq + i, gate_blk))],
        out_specs=pl.BlockSpec((tq, qw), lambda b, g, i: (b * nq + i, g)),
        out_shape=jax.ShapeDtypeStruct((batch * seq, Q_A_W), BF16),
        compiler_params=_cparams("parallel", "parallel", "parallel"),
        name="nsa_win_attn",
    )(slopes, z, z, z, z)


def _swa_kernel(sink_ref, q_ref, k_ref, v_ref, o_ref, *, tq, window):
    i = pl.program_id(1)
    span = window + tq
    start = pl.multiple_of(jnp.maximum(i * tq - window, 0), tq)
    tpos = _row_positions(HPG_B, tq, i * tq)
    dist = tpos - (start + _iota((1, span), 1))
    allowed = (dist >= 0) & (dist < window)
    distf = dist.astype(F32)
    outs = []
    for g in range(KV_B):
        qs = _stack_heads(q_ref, HPG_B, HD_B, col0=g * HPG_B * HD_B)
        k = k_ref[pl.ds(start, span), g * HD_B:(g + 1) * HD_B]
        v = v_ref[pl.ds(start, span), g * HD_B:(g + 1) * HD_B]
        slope_col = _head_column([2.0 ** (-8.0 * (g * HPG_B + h + 1) / H_B) for h in range(HPG_B)], HPG_B, tq)
        sink_col = _head_column([sink_ref[g * HPG_B + h] for h in range(HPG_B)], HPG_B, tq)
        s = _dot_nt(qs, k) * (HD_B ** -0.5)
        s = jnp.where(allowed, s - slope_col * distf, NEG_INF)
        m = jnp.maximum(jnp.max(s, axis=-1, keepdims=True), sink_col)
        p = jnp.exp(s - m)
        denom = jnp.sum(p, axis=-1, keepdims=True) + jnp.exp(sink_col - m)
        o = _dot(p.astype(BF16), v) / denom
        outs += [o[h * tq:(h + 1) * tq] for h in range(HPG_B)]
    o_ref[...] = jnp.concatenate(outs, axis=1).astype(o_ref.dtype)


def _swa(z, sink, batch, seq):
    tq = ATT_TQ
    nq = seq // tq
    return pl.pallas_call(
        functools.partial(_swa_kernel, tq=tq, window=WIN_B),
        grid=(batch, nq),
        in_specs=[pl.BlockSpec(memory_space=pltpu.SMEM),
                  pl.BlockSpec((tq, Q_B_W), lambda b, i: (b * nq + i, OFF_QB // Q_B_W)),
                  pl.BlockSpec((seq, KV_B_W), lambda b, i: (b, OFF_KB // KV_B_W)),
                  pl.BlockSpec((seq, KV_B_W), lambda b, i: (b, OFF_VB // KV_B_W))],
        out_specs=pl.BlockSpec((tq, Q_B_W), lambda b, i: (b * nq + i, 0)),
        out_shape=jax.ShapeDtypeStruct((batch * seq, Q_B_W), BF16),
        compiler_params=_cparams("parallel", "parallel"),
        name="swa_attn",
    )(sink, z, z, z)


def _moba_kernel(slopes_ref, q_ref, k_ref, v_ref, o_ref, *, seq):
    h = pl.program_id(1)
    i = pl.program_id(2)
    tq = BLK_C
    q = q_ref[...]
    slope = slopes_ref[h]
    scale = HD_C ** -0.5

    avg = jnp.where(_iota((LANES, seq), 1) // BLK_C == _iota((LANES, seq), 0), 1.0 / BLK_C, 0.0)
    k_mean = _dot(avg.astype(BF16), k_ref[...])
    score = _dot_nt(q.astype(F32), k_mean, precision=HIGHEST)
    lane = _iota((tq, LANES), 1)
    past = lane < i
    sel = _select_topk_mask(jnp.where(past, score, -jnp.inf), TOPK_C)
    sel = jnp.where(past, sel, 0.0)
    t_row = i * tq + _iota((tq, 1), 0)

    def body(kt, carry):
        m, l, acc = carry
        k0 = pl.multiple_of(kt * BLK_C, BLK_C)
        k = k_ref[pl.ds(k0, BLK_C), :]
        v = v_ref[pl.ds(k0, BLK_C), :]
        s = _dot_nt(q, k) * scale
        dist = t_row - (k0 + _iota((1, BLK_C), 1))
        chosen = _pick_lane(sel, kt)
        thresh = jnp.where(kt == i, 0, jnp.where(chosen > 0.5, -seq, 2 * seq))
        s = jnp.where(dist >= thresh, s - slope * dist.astype(F32), NEG_INF)
        m_new = jnp.maximum(m, jnp.max(s, axis=-1, keepdims=True))
        alpha = jnp.exp(m - m_new)
        p = jnp.exp(s - m_new)
        l = alpha * l + jnp.sum(p, axis=-1, keepdims=True)
        acc = alpha * acc + _dot(p.astype(BF16), v)
        return m_new, l, acc

    init = (jnp.full((tq, 1), NEG_INF, F32), jnp.zeros((tq, 1), F32), jnp.zeros((tq, HD_C), F32))
    _, l, acc = lax.fori_loop(0, i + 1, body, init)
    o_ref[...] = (acc / l).astype(o_ref.dtype)


def _moba(z, slopes, batch, seq):
    nq = seq // BLK_C
    qblk, kblk, vblk = OFF_QC // LANES, OFF_KC // LANES, OFF_VC // LANES
    return pl.pallas_call(
        functools.partial(_moba_kernel, seq=seq),
        grid=(batch, H_C, nq),
        in_specs=[pl.BlockSpec(memory_space=pltpu.SMEM),
                  pl.BlockSpec((BLK_C, HD_C), lambda b, h, i: (b * nq + i, qblk + h)),
                  pl.BlockSpec((seq, HD_C), lambda b, h, i: (b, kblk + h)),
                  pl.BlockSpec((seq, HD_C), lambda b, h, i: (b, vblk + h))],
        out_specs=pl.BlockSpec((BLK_C, HD_C), lambda b, h, i: (b * nq + i, h)),
        out_shape=jax.ShapeDtypeStruct((batch * seq, C_W), BF16),
        compiler_params=_cparams("parallel", "parallel", "parallel"),
        name="moba_attn",
    )(slopes, z, z, z)


def _merge_kernel(oc_ref, os_ref, ow_ref, ob_ref, om_ref, g0_ref, g1_ref, g2_ref, wa_ref, wb_ref, wc_ref, o_ref):
    o_a = (oc_ref[...].astype(F32) + os_ref[...].astype(F32) + ow_ref[...].astype(F32)).astype(BF16)
    merged = jax.nn.sigmoid(g0_ref[...].astype(F32)) * _dot(o_a, wa_ref[...])
    merged = merged + jax.nn.sigmoid(g1_ref[...].astype(F32)) * _dot(ob_ref[...], wb_ref[...])
    merged = merged + jax.nn.sigmoid(g2_ref[...].astype(F32)) * _dot(om_ref[...], wc_ref[...])
    o_ref[...] = merged.astype(o_ref.dtype)


def _merge(o_cmp, o_slc, o_win, o_b, o_c, z, wa, wb, wc):
    tm = ROW_TM
    t = z.shape[0]
    gblk = OFF_MERGE // D_MODEL
    row = lambda w: pl.BlockSpec((tm, w), lambda i: (i, 0))
    full = lambda a: pl.BlockSpec(a.shape, lambda i: (0, 0))
    gate = lambda r: pl.BlockSpec((tm, D_MODEL), lambda i: (i, gblk + r))
    return pl.pallas_call(
        _merge_kernel,
        grid=(t // tm,),
        in_specs=[row(Q_A_W), row(Q_A_W), row(Q_A_W), row(Q_B_W), row(C_W),
                  gate(0), gate(1), gate(2), full(wa), full(wb), full(wc)],
        out_specs=row(D_MODEL),
        out_shape=jax.ShapeDtypeStruct((t, D_MODEL), BF16),
        compiler_params=_cparams("parallel"),
        name="mixer_merge",
    )(o_cmp, o_slc, o_win, o_b, o_c, z, z, z, wa, wb, wc)


def _layer_norm(h, g_ref, b_ref):
    mu = jnp.mean(h, axis=-1, keepdims=True)
    xc = h - mu
    var = jnp.mean(xc * xc, axis=-1, keepdims=True)
    return xc * lax.rsqrt(var + LN_EPS) * g_ref[...] + b_ref[...]


def _proj_ln_kernel(m_ref, w_ref, x_ref, g_ref, b_ref, xo_ref, xb_ref):
    y = _dot(m_ref[...], w_ref[...])
    out = _layer_norm(ALPHA * x_ref[...] + y, g_ref, b_ref)
    xo_ref[...] = out
    xb_ref[...] = out.astype(BF16)


def _proj_ln(merged, w_out, x, g, b):
    tm = ROW_TM
    t = x.shape[0]
    row = pl.BlockSpec((tm, D_MODEL), lambda i: (i, 0))
    vec = pl.BlockSpec((1, D_MODEL), lambda i: (0, 0))
    return pl.pallas_call(
        _proj_ln_kernel,
        grid=(t // tm,),
        in_specs=[row, pl.BlockSpec((D_MODEL, D_MODEL), lambda i: (0, 0)), row, vec, vec],
        out_specs=[row, row],
        out_shape=[jax.ShapeDtypeStruct((t, D_MODEL), F32), jax.ShapeDtypeStruct((t, D_MODEL), BF16)],
        compiler_params=_cparams("parallel"),
        name="out_proj_ln",
    )(merged, w_out, x, g, b)


def _moe_ln_kernel(x_ref, ya_ref, yb_ref, w_ref, g_ref, b_ref, xo_ref, xb_ref):
    w = w_ref[...]
    y = w[:, 0:1] * ya_ref[...].astype(F32) + w[:, 1:2] * yb_ref[...].astype(F32)
    out = _layer_norm(ALPHA * x_ref[...] + y, g_ref, b_ref)
    xo_ref[...] = out
    xb_ref[...] = out.astype(BF16)


def _moe_ln(x, ya, yb, wts, g, b):
    tm = ROW_TM
    t = x.shape[0]
    row = pl.BlockSpec((tm, D_MODEL), lambda i: (i, 0))
    vec = pl.BlockSpec((1, D_MODEL), lambda i: (0, 0))
    return pl.pallas_call(
        _moe_ln_kernel,
        grid=(t // tm,),
        in_specs=[row, row, row, pl.BlockSpec((tm, LANES), lambda i: (i, 0)), vec, vec],
        out_specs=[row, row],
        out_shape=[jax.ShapeDtypeStruct((t, D_MODEL), F32), jax.ShapeDtypeStruct((t, D_MODEL), BF16)],
        compiler_params=_cparams("parallel"),
        name="moe_combine_ln",
    )(x, ya, yb, wts, g, b)


def _router_kernel(x_ref, w_ref, b_ref, id_ref, wt_ref):
    logits = _dot(x_ref[...], w_ref[...], precision=HIGHEST) + b_ref[...]
    lane = _iota(logits.shape, 1).astype(F32)
    first = lambda hit: jnp.min(jnp.where(hit, lane, float(LANES)), axis=-1, keepdims=True)
    gl = jnp.where(lane < N_GROUPS, logits, -jnp.inf)
    gm = jnp.max(gl, axis=-1, keepdims=True)
    g_w = 1.0 / jnp.sum(jnp.exp(gl - gm), axis=-1, keepdims=True)
    lo = N_GROUPS + first(gl == gm) * EXP_PER_GROUP
    el = jnp.where((lane >= lo) & (lane < lo + EXP_PER_GROUP), logits, -jnp.inf)
    m1 = jnp.max(el, axis=-1, keepdims=True)
    i1 = first(el == m1)
    el2 = jnp.where(lane == i1, -jnp.inf, el)
    m2 = jnp.max(el2, axis=-1, keepdims=True)
    i2 = first(el2 == m2)
    e2 = jnp.exp(m2 - m1)
    w1 = g_w / (1.0 + e2)
    w2 = g_w * e2 / (1.0 + e2)
    ids = jnp.where(lane == 0.0, i1 - N_GROUPS, jnp.where(lane == 1.0, i2 - N_GROUPS, 0.0))
    id_ref[...] = ids.astype(jnp.int32)
    wt_ref[...] = jnp.where(lane == 0.0, w1, jnp.where(lane == 1.0, w2, 0.0))


def _router(x, w_rt, b_rt):
    tm = ROUTE_TM
    t = x.shape[0]
    return pl.pallas_call(
        _router_kernel,
        grid=(t // tm,),
        in_specs=[pl.BlockSpec((tm, D_MODEL), lambda i: (i, 0)),
                  pl.BlockSpec((D_MODEL, LANES), lambda i: (0, 0)),
                  pl.BlockSpec((1, LANES), lambda i: (0, 0))],
        out_specs=[pl.BlockSpec((tm, LANES), lambda i: (i, 0)), pl.BlockSpec((tm, LANES), lambda i: (i, 0))],
        out_shape=[jax.ShapeDtypeStruct((t, LANES), jnp.int32), jax.ShapeDtypeStruct((t, LANES), F32)],
        compiler_params=_cparams("parallel"),
        name="moe_router",
    )(x, w_rt, b_rt)


def _expert_kernel(te_ref, na_ref, *refs, n_chunks, tiles_per_chunk):
    x_refs = refs[:n_chunks]
    wg_ref, wu_ref, wd_ref, y_ref = refs[n_chunks:]
    j = pl.program_id(0)

    @pl.when(j < na_ref[0])
    def _():
        c = j // tiles_per_chunk
        x = x_refs[n_chunks - 1][...]
        for r in range(n_chunks - 2, -1, -1):
            x = jnp.where(c == r, x_refs[r][...], x)
        hg = _dot(x, wg_ref[0])
        hu = _dot(x, wu_ref[0])
        a = hg * jax.nn.sigmoid(hg) * hu
        y_ref[...] = _dot(a.astype(BF16), wd_ref[0]).astype(y_ref.dtype)

    @pl.when(j >= na_ref[0])
    def _():
        y_ref[...] = jnp.zeros(y_ref.shape, y_ref.dtype)


def _experts(x_chunks, tile_expert, n_active, wg, wu, wd):
    tm = EXP_TM
    n_chunks = len(x_chunks)
    tiles_per_chunk = x_chunks[0].shape[0] // tm
    n_tiles = n_chunks * tiles_per_chunk

    def x_spec(r):
        return pl.BlockSpec((tm, D_MODEL), lambda j, te, na: (
            jnp.clip(jnp.minimum(j, na[0] - 1) - r * tiles_per_chunk, 0, tiles_per_chunk - 1), 0))

    grid_spec = pltpu.PrefetchScalarGridSpec(
        num_scalar_prefetch=2,
        grid=(n_tiles,),
        in_specs=[x_spec(r) for r in range(n_chunks)] + [
            pl.BlockSpec((1, D_MODEL, D_EXPERT), lambda j, te, na: (te[j], 0, 0)),
            pl.BlockSpec((1, D_MODEL, D_EXPERT), lambda j, te, na: (te[j], 0, 0)),
            pl.BlockSpec((1, D_EXPERT, D_MODEL), lambda j, te, na: (te[j], 0, 0))],
        out_specs=pl.BlockSpec((tm, D_MODEL), lambda j, te, na: (j, 0)),
    )
    return pl.pallas_call(
        functools.partial(_expert_kernel, n_chunks=n_chunks, tiles_per_chunk=tiles_per_chunk),
        grid_spec=grid_spec,
        out_shape=jax.ShapeDtypeStruct((n_tiles * tm, D_MODEL), BF16),
        compiler_params=_cparams("arbitrary"),
        name="moe_experts",
    )(tile_expert, n_active, *x_chunks, wg, wu, wd)


def _dispatch_plan(expert_ids, tm, n_tiles):
    t = expert_ids.shape[0]
    e_flat = expert_ids.reshape(-1)
    onehot = (e_flat[:, None] == jnp.arange(N_EXPERTS, dtype=jnp.int32)[None, :]).astype(jnp.int32)
    csum = jnp.cumsum(onehot, axis=0)
    rank = jnp.sum(csum * onehot, axis=1) - 1
    counts = csum[-1]
    padded = ((counts + tm - 1) // tm) * tm
    ends = jnp.cumsum(padded)
    dest = jnp.sum((ends - padded)[None, :] * onehot, axis=1) + rank
    src_tok = jnp.zeros((n_tiles * tm,), jnp.int32).at[dest].set(
        jnp.arange(2 * t, dtype=jnp.int32) // 2, mode="promise_in_bounds", unique_indices=True)
    n_active = ends[-1] // tm
    tile_start = jnp.minimum(jnp.arange(n_tiles, dtype=jnp.int32), n_active - 1) * tm
    te = jnp.sum((ends[None, :] <= tile_start[:, None]).astype(jnp.int32), axis=1)
    te = jnp.minimum(te, N_EXPERTS - 1)
    return dest.reshape(t, 2), src_tok, te, n_active.reshape(1).astype(jnp.int32)


def _alibi(n):
    return jnp.asarray([2.0 ** (-8.0 * (i + 1) / n) for i in range(n)], F32)


def _selection_overlap(seq):
    n_cmp = (seq - L_CMP) // STRIDE_CMP + 1
    n_sel = seq // L_SEL
    c_start = STRIDE_CMP * np.arange(n_cmp)
    s_start = L_SEL * np.arange(n_sel)
    inter = np.clip(np.minimum(c_start[:, None] + L_CMP, s_start[None, :] + L_SEL)
                    - np.maximum(c_start[:, None], s_start[None, :]), 0, None) / L_CMP
    out = np.zeros((LANES, LANES), np.float32)
    out[:n_cmp, :n_sel] = inter
    return jnp.asarray(out)


def _permute_w_in(w):
    o_gate = Q_A_W + KV_A_W
    o_merge = w.shape[1] - MERGE_W
    pad = jnp.zeros((w.shape[0], GATE_A_PAD - GATE_A_W), w.dtype)
    return jnp.concatenate([w[:, o_merge:], w[:, :o_gate], w[:, o_gate + GATE_A_W:o_merge],
                            w[:, o_gate:o_gate + GATE_A_W], pad], axis=1).astype(BF16)


def kernel(x, w_in, nsa_cmp_pos, nsa_cmp_w, sink_b, w_br_a, w_br_b, w_br_c, w_out, ln1_g, ln1_b,
           w_group, b_group, w_router, b_router, w_gate, w_up, w_down, ln2_g, ln2_b):
    batch, seq, d = x.shape
    t = batch * seq
    assert d == D_MODEL and seq % BLK_C == 0 and seq // L_SEL <= LANES and t % MM_TM == 0
    assert (seq - L_CMP) // STRIDE_CMP + 1 < LANES and seq // BLK_C <= LANES
    n_tiles = (TOPK_EXPERT * t) // EXP_TM + N_EXPERTS
    assert n_tiles % EXP_CHUNKS == 0

    slopes_a = _alibi(H_A)
    slopes_c = _alibi(H_C)
    inter = _selection_overlap(seq)

    xf = x.reshape(t, d)
    xb = xf.astype(BF16)
    for l in range(DEPTH):
        z = _matmul(xb, _permute_w_in(w_in[l]), BF16)
        kvc = _cmp_kv(z, nsa_cmp_pos[l], nsa_cmp_w[l], batch, seq)
        o_cmp, selmask = _nsa_cmp(z, kvc, slopes_a, inter, batch, seq)
        o_slc = _nsa_slc(z, selmask, slopes_a, batch, seq)
        o_win = _nsa_win(z, slopes_a, batch, seq)
        o_b = _swa(z, sink_b[l], batch, seq)
        o_c = _moba(z, slopes_c, batch, seq)
        merged = _merge(o_cmp, o_slc, o_win, o_b, o_c, z,
                        w_br_a[l].astype(BF16), w_br_b[l].astype(BF16), w_br_c[l].astype(BF16))
        xf, xb = _proj_ln(merged, w_out[l].astype(BF16), xf, ln1_g[l][None, :], ln1_b[l][None, :])

        w_rt = jnp.concatenate([w_group[l], w_router[l],
                                jnp.zeros((d, LANES - N_GROUPS - N_EXPERTS), F32)], axis=1)
        b_rt = jnp.concatenate([b_group[l], b_router[l],
                                jnp.zeros((LANES - N_GROUPS - N_EXPERTS,), F32)])[None, :]
        ids, wts = _router(xf, w_rt, b_rt)
        dest, src_tok, tile_expert, n_active = _dispatch_plan(ids[:, :TOPK_EXPERT], EXP_TM, n_tiles)
        take = lambda a, idx: a.at[idx].get(mode="promise_in_bounds")
        x_chunks = [compute_on("tpu_sparsecore")(jax.jit(take))(xb, c) for c in jnp.split(src_tok, EXP_CHUNKS)]
        y = _experts(x_chunks, tile_expert, n_active,
                     w_gate[l].astype(BF16), w_up[l].astype(BF16), w_down[l].astype(BF16))
        ya = take(y, dest[:, 0])
        yb = take(y, dest[:, 1])
        xf, xb = _moe_ln(xf, ya, yb, wts, ln2_g[l][None, :], ln2_b[l][None, :])
    return xf.reshape(batch, seq, d)
```

```python
import functools

import numpy as np
import jax
import jax.numpy as jnp
from jax import lax
from jax.experimental import pallas as pl
from jax.experimental.pallas import tpu as pltpu
from jax.experimental.compute_on import compute_on

F32 = jnp.float32
BF16 = jnp.bfloat16

D_MODEL = 2048
DEPTH = 2
H_A, KV_A, HD_A = 8, 2, 128
HPG_A = H_A // KV_A
L_CMP, STRIDE_CMP, L_SEL, N_SEL, WIN_A = 32, 16, 64, 8, 512
H_B, KV_B, HD_B, WIN_B = 8, 2, 64, 128
HPG_B = H_B // KV_B
H_C, HD_C, BLK_C, TOPK_C = 4, 128, 256, 3
N_GROUPS, EXP_PER_GROUP, D_EXPERT, TOPK_EXPERT = 4, 8, 256, 2
N_EXPERTS = N_GROUPS * EXP_PER_GROUP
LN_EPS = 1e-5
NEG_INF = -1e30
ALPHA = (2.0 * DEPTH) ** 0.25

LANES = 128
SUBLANES = 8
VMEM_LIMIT = 48 * 1024 * 1024
BIG = 2.0 ** 100

Q_A_W = H_A * HD_A
KV_A_W = 3 * 2 * KV_A * HD_A
GATE_A_W = 3 * H_A
Q_B_W = H_B * HD_B
KV_B_W = KV_B * HD_B
C_W = H_C * HD_C
MERGE_W = 3 * D_MODEL
D_IN = Q_A_W + KV_A_W + GATE_A_W + Q_B_W + 2 * KV_B_W + 3 * C_W + MERGE_W
OFF_MERGE = 0
OFF_QA = OFF_MERGE + MERGE_W
OFF_QB = OFF_QA + Q_A_W
OFF_QC = OFF_QB + Q_B_W
OFF_KC = OFF_QC + C_W
OFF_VC = OFF_KC + C_W
OFF_KVA = OFF_VC + C_W
OFF_KB = OFF_KVA + KV_A_W
OFF_VB = OFF_KB + KV_B_W
OFF_GATE_A = OFF_VB + KV_B_W
GATE_A_PAD = 256
Z_W = OFF_GATE_A + GATE_A_PAD
assert OFF_QA % (HPG_A * HD_A) == 0 and OFF_QB % Q_B_W == 0 and OFF_MERGE % D_MODEL == 0
assert OFF_QC % C_W == 0 and OFF_KC % C_W == 0 and OFF_VC % C_W == 0

AUG_POS_HI, AUG_POS_LO, AUG_ONE, AUG_PAD = 32, 33, 34, 35
POS_SPLIT = 256

PREP_ROWS = 64
MM_TM, MM_TN = 1024, 512
ROW_TM = 256
ROUTE_TM = 512
EXP_TM = 256
EXP_CHUNKS = 3
ATT_TQ = 256
SWA_TQ = 128


def _cparams(*sem):
    return pltpu.CompilerParams(dimension_semantics=sem, vmem_limit_bytes=VMEM_LIMIT)


def _dot(a, b):
    return jnp.dot(a, b, preferred_element_type=F32)


def _dot_nt(a, b):
    return lax.dot_general(a, b, (((1,), (1,)), ((), ())), preferred_element_type=F32)


def _iota(shape, dim):
    return lax.broadcasted_iota(jnp.int32, shape, dim)


def _split3(x):
    hi = x.astype(BF16)
    r1 = x - hi.astype(F32)
    mid = r1.astype(BF16)
    lo = (r1 - mid.astype(F32)).astype(BF16)
    return hi, mid, lo


def _pick_lane(x, lane_idx):
    lane = _iota(x.shape, 1)
    return jnp.sum(jnp.where(lane == lane_idx, x, 0.0), axis=-1, keepdims=True)


def _topk_rows(vals, k):
    row = _iota(vals.shape, 0).astype(F32)
    sel = jnp.zeros(vals.shape, F32)
    for _ in range(k):
        m = jnp.max(vals, axis=0, keepdims=True)
        idx = jnp.min(jnp.where(vals == m, row, float(LANES)), axis=0, keepdims=True)
        pick = row == idx
        sel = jnp.where(pick, 1.0, sel)
        vals = jnp.where(pick, -jnp.inf, vals)
    return sel


def _rows_to_lanes(x_t, tq):
    pad = jnp.zeros((LANES - x_t.shape[0], tq), F32)
    return jnp.concatenate([x_t, pad], axis=0).T


def _query_aug(base, slope, center, pad_flag=False):
    lane = _iota(base.shape, 1)
    aug = jnp.where(lane == AUG_POS_HI, slope * float(POS_SPLIT), base)
    aug = jnp.where(lane == AUG_POS_LO, slope, aug)
    aug = jnp.where(lane == AUG_ONE, -slope * center, aug)
    if pad_flag:
        aug = jnp.where(lane == AUG_PAD, 1.0, aug)
    return aug.astype(BF16)


def _softmax_update(s, v, carry):
    m, l, acc = carry
    m_new = jnp.maximum(m, jnp.max(s, axis=-1, keepdims=True))
    alpha = jnp.exp(m - m_new)
    p = jnp.exp(s - m_new)
    l = alpha * l + jnp.sum(p, axis=-1, keepdims=True)
    acc = alpha * acc + _dot(p.astype(BF16), v)
    return m_new, l, acc


def _softmax_init(rows, hd):
    return (jnp.full((rows, 1), NEG_INF, F32), jnp.zeros((rows, 1), F32), jnp.zeros((rows, hd), F32))


def _prep_w_in_kernel(w_ref, o_ref):
    s_kva = Q_A_W
    s_gate = s_kva + KV_A_W
    s_qb = s_gate + GATE_A_W
    s_kb = s_qb + Q_B_W
    s_qc = s_kb + 2 * KV_B_W
    s_merge = s_qc + 3 * C_W
    rows = o_ref.shape[0]

    def move(dst, src, width):
        o_ref[:, dst:dst + width] = w_ref[0, :, src:src + width].astype(BF16)

    move(OFF_MERGE, s_merge, MERGE_W)
    move(OFF_QA, 0, Q_A_W)
    move(OFF_QB, s_qb, Q_B_W)
    move(OFF_QC, s_qc, 3 * C_W)
    move(OFF_KVA, s_kva, KV_A_W)
    move(OFF_KB, s_kb, 2 * KV_B_W)
    gate = jnp.concatenate([w_ref[0, :, s_gate:s_gate + GATE_A_W],
                            jnp.zeros((rows, GATE_A_PAD - GATE_A_W), F32)], axis=1)
    o_ref[:, OFF_GATE_A:Z_W] = gate.astype(BF16)


def _prep_w_in(w_in, layer):
    d = w_in.shape[1]
    return pl.pallas_call(
        _prep_w_in_kernel,
        grid=(d // PREP_ROWS,),
        in_specs=[pl.BlockSpec((1, PREP_ROWS, D_IN), lambda i: (layer, i, 0))],
        out_specs=pl.BlockSpec((PREP_ROWS, Z_W), lambda i: (i, 0)),
        out_shape=jax.ShapeDtypeStruct((d, Z_W), BF16),
        compiler_params=_cparams("parallel"),
        name="w_in_relayout",
    )(w_in)


def _mm_kernel(a_ref, b_ref, o_ref):
    o_ref[...] = _dot(a_ref[...], b_ref[...]).astype(o_ref.dtype)


def _matmul(a, b, out_dtype):
    m, k = a.shape
    n = b.shape[1]
    return pl.pallas_call(
        _mm_kernel,
        grid=(m // MM_TM, n // MM_TN),
        in_specs=[pl.BlockSpec((MM_TM, k), lambda i, j: (i, 0)),
                  pl.BlockSpec((k, MM_TN), lambda i, j: (0, j))],
        out_specs=pl.BlockSpec((MM_TM, MM_TN), lambda i, j: (i, j)),
        out_shape=jax.ShapeDtypeStruct((m, n), out_dtype),
        compiler_params=_cparams("parallel", "parallel"),
        name="in_proj",
    )(a, b)


def _cmp_kv_kernel(k_ref, pos_ref, w_ref, o_ref, kf_ref, *, seq):
    kf_ref[0:seq, :] = k_ref[...].astype(F32)
    kf_ref[seq:seq + LANES, :] = jnp.zeros((LANES, HD_A), F32)
    acc = jnp.zeros((LANES, HD_A), F32)
    for l in range(L_CMP):
        rows = kf_ref[pl.ds(l, LANES, stride=STRIDE_CMP), :] + pos_ref[0, 0, l:l + 1, :]
        acc = acc + _dot(rows.astype(BF16), w_ref[0, 0, l].astype(BF16))
    o_ref[0, 0] = acc.astype(o_ref.dtype)


def _cmp_kv(z, cmp_pos, cmp_w, layer, batch, seq):
    blk0 = OFF_KVA // LANES
    return pl.pallas_call(
        functools.partial(_cmp_kv_kernel, seq=seq),
        grid=(batch, 2 * KV_A),
        in_specs=[pl.BlockSpec((seq, LANES), lambda b, j: (b, blk0 + j)),
                  pl.BlockSpec((1, 1, L_CMP, HD_A), lambda b, j: (layer, j // KV_A, 0, 0)),
                  pl.BlockSpec((1, 1, L_CMP, HD_A, HD_A), lambda b, j: (layer, j // KV_A, 0, 0, 0))],
        out_specs=pl.BlockSpec((1, 1, LANES, HD_A), lambda b, j: (b, j, 0, 0)),
        out_shape=jax.ShapeDtypeStruct((batch, 2 * KV_A, LANES, HD_A), BF16),
        scratch_shapes=[pltpu.VMEM((seq + LANES, HD_A), F32)],
        compiler_params=_cparams("parallel", "parallel"),
        name="nsa_cmp_kv",
    )(z, cmp_pos, cmp_w)


def _nsa_cmp_kernel(slopes_ref, q_ref, kc_ref, vc_ref, gate_ref, inter_t_ref, o_ref, nsel_ref, *, tq):
    g = pl.program_id(1)
    i = pl.program_id(2)
    shape = (tq, LANES)
    t = i * tq + _iota(shape, 0)
    lane = _iota(shape, 1)
    dist_i = t - (lane * STRIDE_CMP + (L_CMP - 1))
    ok = dist_i >= 0
    okf = ok.astype(F32)
    dist = dist_i.astype(F32)
    kc = kc_ref[0, 0]
    vc = vc_ref[0, 0]
    sig = jax.nn.sigmoid(gate_ref[...].astype(F32))
    scale = HD_A ** -0.5
    psum = jnp.zeros(shape, F32)
    outs = []
    for h in range(HPG_A):
        q = q_ref[:, h * HD_A:(h + 1) * HD_A]
        s = _dot_nt(q, kc) * scale - slopes_ref[g * HPG_A + h] * dist
        s = jnp.where(ok, s, NEG_INF)
        e = jnp.exp(s - jnp.max(s, axis=-1, keepdims=True))
        p = e / jnp.sum(e, axis=-1, keepdims=True) * okf
        psum = psum + p
        o = _dot(p.astype(BF16), vc)
        outs.append(o * _pick_lane(sig, (g * HPG_A + h) * 3 + 0))
    o_ref[...] = jnp.concatenate(outs, axis=1).astype(o_ref.dtype)

    n_sel_rows = 32
    inter_t = inter_t_ref[...]
    imp_t = sum(_dot_nt(inter_t, part) for part in _split3(psum))[0:n_sel_rows]
    shape_t = (n_sel_rows, tq)
    j = _iota(shape_t, 0)
    blk_t = (i * tq + _iota(shape_t, 1)) // L_SEL
    valid = j <= blk_t
    forced = (j == 0) | (j == blk_t) | (j == blk_t - 1)
    vals = jnp.where(forced, jnp.inf, jnp.where(valid, imp_t, -jnp.inf))
    sel = _topk_rows(vals, N_SEL)
    not_selected = jnp.where(valid & (sel > 0.5), 0.0, 1.0)
    nsel_ref[0, 0] = _rows_to_lanes(not_selected, tq).astype(nsel_ref.dtype)


def _nsa_cmp(z, kvc, slopes, inter_t, batch, seq):
    tq = ATT_TQ
    nq = seq // tq
    qw = HPG_A * HD_A
    gate_blk = OFF_GATE_A // LANES
    return pl.pallas_call(
        functools.partial(_nsa_cmp_kernel, tq=tq),
        grid=(batch, KV_A, nq),
        in_specs=[pl.BlockSpec(memory_space=pltpu.SMEM),
                  pl.BlockSpec((tq, qw), lambda b, g, i: (b * nq + i, OFF_QA // qw + g)),
                  pl.BlockSpec((1, 1, LANES, HD_A), lambda b, g, i: (b, g, 0, 0)),
                  pl.BlockSpec((1, 1, LANES, HD_A), lambda b, g, i: (b, KV_A + g, 0, 0)),
                  pl.BlockSpec((tq, LANES), lambda b, g, i: (b * nq + i, gate_blk)),
                  pl.BlockSpec((LANES, LANES), lambda b, g, i: (0, 0))],
        out_specs=[pl.BlockSpec((tq, qw), lambda b, g, i: (b * nq + i, g)),
                   pl.BlockSpec((1, 1, tq, LANES), lambda b, g, i: (b, g, i, 0))],
        out_shape=[jax.ShapeDtypeStruct((batch * seq, Q_A_W), BF16),
                   jax.ShapeDtypeStruct((batch, KV_A, seq, LANES), BF16)],
        compiler_params=_cparams("parallel", "parallel", "parallel"),
        name="nsa_cmp_attn",
    )(slopes, z, kvc, kvc, z, inter_t)


def _scaled_q(q_ref, h, hd, scale):
    return (q_ref[:, h * hd:(h + 1) * hd].astype(F32) * scale).astype(BF16)


def _gated_heads(o, sig, g, branch, tq):
    outs = [o[h * tq:(h + 1) * tq] * _pick_lane(sig, (g * HPG_A + h) * 3 + branch) for h in range(HPG_A)]
    return jnp.concatenate(outs, axis=1)


def _nsa_slc_kernel(slopes_ref, q_ref, k_ref, v_ref, kaug_ref, nsel_ref, gate_ref, o_ref, *, tq):
    g = pl.program_id(1)
    i = pl.program_id(2)
    rows = HPG_A * tq
    scale = HD_A ** -0.5
    nsel = nsel_ref[0, 0].astype(F32)
    center = (i * tq).astype(F32)
    qx = jnp.concatenate(
        [jnp.concatenate([_scaled_q(q_ref, h, HD_A, scale),
                          _query_aug(nsel, slopes_ref[g * HPG_A + h], center)], axis=1)
         for h in range(HPG_A)], axis=0)

    def scores(kt):
        k0 = pl.multiple_of(kt * tq, tq)
        kx = jnp.concatenate([k_ref[pl.ds(k0, tq), :], kaug_ref[pl.ds(k0, tq), :]], axis=1)
        return _dot_nt(qx, kx)

    def values(kt):
        return v_ref[pl.ds(pl.multiple_of(kt * tq, tq), tq), :]

    def body(kt, carry):
        s, m, l, acc = carry
        s_next = scores(kt + 1)
        m, l, acc = _softmax_update(s, values(kt), (m, l, acc))
        return s_next, m, l, acc

    s, m, l, acc = lax.fori_loop(0, i, body, (scores(0),) + _softmax_init(rows, HD_A))
    r = _iota((rows, tq), 0)
    causal = (r - (r // tq) * tq) >= _iota((rows, tq), 1)
    _, l, acc = _softmax_update(jnp.where(causal, s, -BIG), values(i), (m, l, acc))
    sig = jax.nn.sigmoid(gate_ref[...].astype(F32))
    o_ref[...] = _gated_heads(acc / l, sig, g, 1, tq).astype(o_ref.dtype)


def _nsa_slc(z, nsel, kaug, slopes, batch, seq):
    tq = ATT_TQ
    nq = seq // tq
    qw = HPG_A * HD_A
    kblk = OFF_KVA // LANES + 1 * 2 * KV_A
    vblk = kblk + KV_A
    gate_blk = OFF_GATE_A // LANES
    return pl.pallas_call(
        functools.partial(_nsa_slc_kernel, tq=tq),
        grid=(batch, KV_A, nq),
        in_specs=[pl.BlockSpec(memory_space=pltpu.SMEM),
                  pl.BlockSpec((tq, qw), lambda b, g, i: (b * nq + i, OFF_QA // qw + g)),
                  pl.BlockSpec((seq, LANES), lambda b, g, i: (b, kblk + g)),
                  pl.BlockSpec((seq, LANES), lambda b, g, i: (b, vblk + g)),
                  pl.BlockSpec((seq, LANES), lambda b, g, i: (0, 0)),
                  pl.BlockSpec((1, 1, tq, LANES), lambda b, g, i: (b, g, i, 0)),
                  pl.BlockSpec((tq, LANES), lambda b, g, i: (b * nq + i, gate_blk))],
        out_specs=pl.BlockSpec((tq, qw), lambda b, g, i: (b * nq + i, g)),
        out_shape=jax.ShapeDtypeStruct((batch * seq, Q_A_W), BF16),
        compiler_params=_cparams("parallel", "parallel", "parallel"),
        name="nsa_slc_attn",
    )(slopes, z, z, z, kaug, nsel, z)


def _band_mask(tq, span, window, n_heads):
    r = _iota((n_heads * tq, span), 0)
    r = r - (r // tq) * tq
    c = _iota((n_heads * tq, span), 1)
    return (c > r) & (c <= r + window)


def _nsa_win_kernel(slopes_ref, q_ref, k_ref, v_ref, kaug_ref, gate_ref, o_ref, kx_ref, vx_ref, *, tq, window, seq):
    g = pl.program_id(1)
    i = pl.program_id(2)

    @pl.when(i == 0)
    def _():
        kx_ref[0:window, 0:HD_A] = jnp.zeros((window, HD_A), BF16)
        kx_ref[window:window + seq, 0:HD_A] = k_ref[...]
        kx_ref[:, HD_A:2 * HD_A] = kaug_ref[...]
        vx_ref[0:window, :] = jnp.zeros((window, HD_A), BF16)
        vx_ref[window:window + seq, :] = v_ref[...]

    span = window + tq
    r0 = pl.multiple_of(i * tq, tq)
    center = (i * tq + window).astype(F32)
    base = jnp.zeros((tq, LANES), F32)
    qx = jnp.concatenate(
        [jnp.concatenate([_scaled_q(q_ref, h, HD_A, HD_A ** -0.5),
                          _query_aug(base, slopes_ref[g * HPG_A + h], center, pad_flag=True)], axis=1)
         for h in range(HPG_A)], axis=0)
    s = _dot_nt(qx, kx_ref[pl.ds(r0, span), :])
    s = jnp.where(_band_mask(tq, span, window, HPG_A), s, -BIG)
    p = jnp.exp(s - jnp.max(s, axis=-1, keepdims=True))
    o = _dot(p.astype(BF16), vx_ref[pl.ds(r0, span), :]) / jnp.sum(p, axis=-1, keepdims=True)
    sig = jax.nn.sigmoid(gate_ref[...].astype(F32))
    o_ref[...] = _gated_heads(o, sig, g, 2, tq).astype(o_ref.dtype)


def _nsa_win(z, kaug_pad, slopes, batch, seq):
    tq = ATT_TQ
    nq = seq // tq
    qw = HPG_A * HD_A
    kblk = OFF_KVA // LANES + 2 * 2 * KV_A
    vblk = kblk + KV_A
    gate_blk = OFF_GATE_A // LANES
    return pl.pallas_call(
        functools.partial(_nsa_win_kernel, tq=tq, window=WIN_A, seq=seq),
        grid=(batch, KV_A, nq),
        in_specs=[pl.BlockSpec(memory_space=pltpu.SMEM),
                  pl.BlockSpec((tq, qw), lambda b, g, i: (b * nq + i, OFF_QA // qw + g)),
                  pl.BlockSpec((seq, LANES), lambda b, g, i: (b, kblk + g)),
                  pl.BlockSpec((seq, LANES), lambda b, g, i: (b, vblk + g)),
                  pl.BlockSpec((WIN_A + seq, LANES), lambda b, g, i: (0, 0)),
                  pl.BlockSpec((tq, LANES), lambda b, g, i: (b * nq + i, gate_blk))],
        out_specs=pl.BlockSpec((tq, qw), lambda b, g, i: (b * nq + i, g)),
        out_shape=jax.ShapeDtypeStruct((batch * seq, Q_A_W), BF16),
        scratch_shapes=[pltpu.VMEM((WIN_A + seq, 2 * HD_A), BF16), pltpu.VMEM((WIN_A + seq, HD_A), BF16)],
        compiler_params=_cparams("parallel", "parallel", "arbitrary"),
        name="nsa_win_attn",
    )(slopes, z, z, z, kaug_pad, z)


SW_POS_HI, SW_POS_LO, SW_ONE, SW_PAD = 0, 1, 2, 3


def _swa_kernel(sink_ref, q_ref, k_ref, v_ref, kaug_ref, o_ref, kx_ref, vx_ref, *, tq, window, seq):
    i = pl.program_id(1)

    @pl.when(i == 0)
    def _():
        for g in range(KV_B):
            kx_ref[g, 0:window, 0:HD_B] = jnp.zeros((window, HD_B), BF16)
            kx_ref[g, window:window + seq, 0:HD_B] = k_ref[:, g * HD_B:(g + 1) * HD_B]
            kx_ref[g, :, HD_B:2 * HD_B] = kaug_ref[...]
            vx_ref[g, 0:window, :] = jnp.zeros((window, HD_B), BF16)
            vx_ref[g, window:window + seq, :] = v_ref[:, g * HD_B:(g + 1) * HD_B]

    span = window + tq
    rows = HPG_B * tq
    r0 = pl.multiple_of(i * tq, tq)
    center = (i * tq + window).astype(F32)
    band = _band_mask(tq, span, window, HPG_B)
    lane = _iota((tq, HD_B), 1)
    hh = _iota((rows, 1), 0) // tq
    row_in_tile = (_iota((rows, 1), 0) - hh * tq).astype(F32)

    def head_column(values):
        col = jnp.full((rows, 1), values[HPG_B - 1], F32)
        for h in range(HPG_B - 2, -1, -1):
            col = jnp.where(hh == h, values[h], col)
        return col

    outs = []
    for g in range(KV_B):
        slopes = [2.0 ** (-8.0 * (g * HPG_B + h + 1) / H_B) for h in range(HPG_B)]
        parts = []
        for h in range(HPG_B):
            aug = jnp.where(lane == SW_POS_HI, slopes[h] * POS_SPLIT, jnp.where(lane == SW_POS_LO, slopes[h], 0.0))
            aug = jnp.where(lane == SW_ONE, -slopes[h] * center, jnp.where(lane == SW_PAD, 1.0, aug))
            parts.append(jnp.concatenate([_scaled_q(q_ref, g * HPG_B + h, HD_B, HD_B ** -0.5),
                                          aug.astype(BF16)], axis=1))
        qx = jnp.concatenate(parts, axis=0)
        s = _dot_nt(qx, kx_ref[g, pl.ds(r0, span), :])
        s = jnp.where(band, s, -BIG)
        sink_shifted = (head_column([sink_ref[g * HPG_B + h] for h in range(HPG_B)])
                        + head_column(slopes) * row_in_tile)
        m = jnp.maximum(jnp.max(s, axis=-1, keepdims=True), sink_shifted)
        p = jnp.exp(s - m)
        denom = jnp.sum(p, axis=-1, keepdims=True) + jnp.exp(sink_shifted - m)
        o = _dot(p.astype(BF16), vx_ref[g, pl.ds(r0, span), :]) / denom
        outs += [o[h * tq:(h + 1) * tq] for h in range(HPG_B)]
    o_ref[...] = jnp.concatenate(outs, axis=1).astype(o_ref.dtype)


def _swa(z, kaug_sw, sink, layer, batch, seq):
    tq = SWA_TQ
    nq = seq // tq
    return pl.pallas_call(
        functools.partial(_swa_kernel, tq=tq, window=WIN_B, seq=seq),
        grid=(batch, nq),
        in_specs=[pl.BlockSpec(memory_space=pltpu.SMEM),
                  pl.BlockSpec((tq, Q_B_W), lambda b, i: (b * nq + i, OFF_QB // Q_B_W)),
                  pl.BlockSpec((seq, KV_B_W), lambda b, i: (b, OFF_KB // KV_B_W)),
                  pl.BlockSpec((seq, KV_B_W), lambda b, i: (b, OFF_VB // KV_B_W)),
                  pl.BlockSpec((WIN_B + seq, HD_B), lambda b, i: (0, 0))],
        out_specs=pl.BlockSpec((tq, Q_B_W), lambda b, i: (b * nq + i, 0)),
        out_shape=jax.ShapeDtypeStruct((batch * seq, Q_B_W), BF16),
        scratch_shapes=[pltpu.VMEM((KV_B, WIN_B + seq, 2 * HD_B), BF16),
                        pltpu.VMEM((KV_B, WIN_B + seq, HD_B), BF16)],
        compiler_params=_cparams("parallel", "arbitrary"),
        name="swa_attn",
    )(sink[layer], z, z, z, kaug_sw)


def _moba_kernel(slopes_ref, q_ref, k_ref, v_ref, kaug_ref, avg_ref, o_ref, km_ref, *, seq):
    i = pl.program_id(1)
    tq = BLK_C
    n_blk = seq // BLK_C
    scale = HD_C ** -0.5

    @pl.when(i == 0)
    def _():
        for h in range(H_C):
            k_mean = _dot(avg_ref[...], k_ref[:, h * HD_C:(h + 1) * HD_C])[0:SUBLANES]
            terms = [t.astype(F32) for t in _split3(k_mean)] + [jnp.zeros((SUBLANES, HD_C), F32)]
            km_ref[h] = jnp.concatenate(terms, axis=0).astype(BF16)

    center = (i * tq).astype(F32)
    blk = _iota((SUBLANES, tq), 0)
    past = blk < i
    qx = []
    for h in range(H_C):
        q = q_ref[:, h * HD_C:(h + 1) * HD_C]
        sc = _dot_nt(km_ref[h], q)
        score_t = sc[0:SUBLANES] + sc[SUBLANES:2 * SUBLANES] + sc[2 * SUBLANES:3 * SUBLANES]
        sel = _topk_rows(jnp.where(past, score_t, -jnp.inf), TOPK_C)
        not_selected = jnp.where(past & (sel < 0.5), 1.0, 0.0)
        qx.append(jnp.concatenate([_scaled_q(q_ref, h, HD_C, scale),
                                   _query_aug(_rows_to_lanes(not_selected, tq), slopes_ref[h], center)], axis=1))
    assert n_blk <= SUBLANES

    def scores(h, kt):
        k0 = pl.multiple_of(kt * tq, tq)
        kx = jnp.concatenate([k_ref[pl.ds(k0, tq), h * HD_C:(h + 1) * HD_C], kaug_ref[pl.ds(k0, tq), :]], axis=1)
        return _dot_nt(qx[h], kx)

    def values(h, kt):
        return v_ref[pl.ds(pl.multiple_of(kt * tq, tq), tq), h * HD_C:(h + 1) * HD_C]

    def body(kt, carry):
        out = []
        for h in range(H_C):
            s, m, l, acc = carry[h]
            s_next = scores(h, kt + 1)
            out.append((s_next,) + _softmax_update(s, values(h, kt), (m, l, acc)))
        return tuple(out)

    init = tuple((scores(h, 0),) + _softmax_init(tq, HD_C) for h in range(H_C))
    carry = lax.fori_loop(0, i, body, init)
    causal = _iota((tq, tq), 0) >= _iota((tq, tq), 1)
    outs = []
    for h in range(H_C):
        s, m, l, acc = carry[h]
        _, l, acc = _softmax_update(jnp.where(causal, s, -BIG), values(h, i), (m, l, acc))
        outs.append(acc / l)
    o_ref[...] = jnp.concatenate(outs, axis=1).astype(o_ref.dtype)


def _moba(z, kaug, avg, slopes, batch, seq):
    nq = seq // BLK_C
    return pl.pallas_call(
        functools.partial(_moba_kernel, seq=seq),
        grid=(batch, nq),
        in_specs=[pl.BlockSpec(memory_space=pltpu.SMEM),
                  pl.BlockSpec((BLK_C, C_W), lambda b, i: (b * nq + i, OFF_QC // C_W)),
                  pl.BlockSpec((seq, C_W), lambda b, i: (b, OFF_KC // C_W)),
                  pl.BlockSpec((seq, C_W), lambda b, i: (b, OFF_VC // C_W)),
                  pl.BlockSpec((seq, LANES), lambda b, i: (0, 0)),
                  pl.BlockSpec((2 * SUBLANES, seq), lambda b, i: (0, 0))],
        out_specs=pl.BlockSpec((BLK_C, C_W), lambda b, i: (b * nq + i, 0)),
        out_shape=jax.ShapeDtypeStruct((batch * seq, C_W), BF16),
        scratch_shapes=[pltpu.VMEM((H_C, 4 * SUBLANES, HD_C), BF16)],
        compiler_params=_cparams("parallel", "arbitrary"),
        name="moba_attn",
    )(slopes, z, z, z, kaug, avg)


def _merge_kernel(oc_ref, os_ref, ow_ref, ob_ref, om_ref, g0_ref, g1_ref, g2_ref, wa_ref, wb_ref, wc_ref, o_ref):
    o_a = (oc_ref[...].astype(F32) + os_ref[...].astype(F32) + ow_ref[...].astype(F32)).astype(BF16)
    merged = jax.nn.sigmoid(g0_ref[...].astype(F32)) * _dot(o_a, wa_ref[...])
    merged = merged + jax.nn.sigmoid(g1_ref[...].astype(F32)) * _dot(ob_ref[...], wb_ref[...])
    merged = merged + jax.nn.sigmoid(g2_ref[...].astype(F32)) * _dot(om_ref[...], wc_ref[...])
    o_ref[...] = merged.astype(o_ref.dtype)


def _merge(o_cmp, o_slc, o_win, o_b, o_c, z, wa, wb, wc):
    tm = ROW_TM
    t = z.shape[0]
    gblk = OFF_MERGE // D_MODEL
    row = lambda w: pl.BlockSpec((tm, w), lambda i: (i, 0))
    full = lambda a: pl.BlockSpec(a.shape, lambda i: (0, 0))
    gate = lambda r: pl.BlockSpec((tm, D_MODEL), lambda i: (i, gblk + r))
    return pl.pallas_call(
        _merge_kernel,
        grid=(t // tm,),
        in_specs=[row(Q_A_W), row(Q_A_W), row(Q_A_W), row(Q_B_W), row(C_W),
                  gate(0), gate(1), gate(2), full(wa), full(wb), full(wc)],
        out_specs=row(D_MODEL),
        out_shape=jax.ShapeDtypeStruct((t, D_MODEL), BF16),
        compiler_params=_cparams("parallel"),
        name="mixer_merge",
    )(o_cmp, o_slc, o_win, o_b, o_c, z, z, z, wa, wb, wc)


def _layer_norm(h, g_ref, b_ref):
    mu = jnp.mean(h, axis=-1, keepdims=True)
    xc = h - mu
    var = jnp.mean(xc * xc, axis=-1, keepdims=True)
    return xc * lax.rsqrt(var + LN_EPS) * g_ref[...] + b_ref[...]


def _proj_ln_kernel(m_ref, w_ref, x_ref, g_ref, b_ref, xo_ref, xb_ref):
    y = _dot(m_ref[...], w_ref[...])
    out = _layer_norm(ALPHA * x_ref[...] + y, g_ref, b_ref)
    xo_ref[...] = out
    xb_ref[...] = out.astype(BF16)


def _proj_ln(merged, w_out, x, g, b):
    tm = ROW_TM
    t = x.shape[0]
    row = pl.BlockSpec((tm, D_MODEL), lambda i: (i, 0))
    vec = pl.BlockSpec((1, D_MODEL), lambda i: (0, 0))
    return pl.pallas_call(
        _proj_ln_kernel,
        grid=(t // tm,),
        in_specs=[row, pl.BlockSpec((D_MODEL, D_MODEL), lambda i: (0, 0)), row, vec, vec],
        out_specs=[row, row],
        out_shape=[jax.ShapeDtypeStruct((t, D_MODEL), F32), jax.ShapeDtypeStruct((t, D_MODEL), BF16)],
        compiler_params=_cparams("parallel"),
        name="out_proj_ln",
    )(merged, w_out, x, g, b)


def _moe_ln_kernel(x_ref, ya_ref, yb_ref, w_ref, g_ref, b_ref, xo_ref, xb_ref):
    w = w_ref[...]
    y = w[:, 0:1] * ya_ref[...].astype(F32) + w[:, 1:2] * yb_ref[...].astype(F32)
    out = _layer_norm(ALPHA * x_ref[...] + y, g_ref, b_ref)
    xo_ref[...] = out
    xb_ref[...] = out.astype(BF16)


def _moe_ln(x, ya, yb, wts, g, b):
    tm = ROW_TM
    t = x.shape[0]
    row = pl.BlockSpec((tm, D_MODEL), lambda i: (i, 0))
    vec = pl.BlockSpec((1, D_MODEL), lambda i: (0, 0))
    return pl.pallas_call(
        _moe_ln_kernel,
        grid=(t // tm,),
        in_specs=[row, row, row, pl.BlockSpec((tm, LANES), lambda i: (i, 0)), vec, vec],
        out_specs=[row, row],
        out_shape=[jax.ShapeDtypeStruct((t, D_MODEL), F32), jax.ShapeDtypeStruct((t, D_MODEL), BF16)],
        compiler_params=_cparams("parallel"),
        name="moe_combine_ln",
    )(x, ya, yb, wts, g, b)


def _router_kernel(x_ref, w_ref, b_ref, id_ref, wt_ref):
    logits = jnp.dot(x_ref[...], w_ref[...], preferred_element_type=F32, precision=lax.Precision.HIGHEST)
    logits = logits + b_ref[...]
    lane = _iota(logits.shape, 1).astype(F32)
    first = lambda hit: jnp.min(jnp.where(hit, lane, float(LANES)), axis=-1, keepdims=True)
    gl = jnp.where(lane < N_GROUPS, logits, -jnp.inf)
    gm = jnp.max(gl, axis=-1, keepdims=True)
    g_w = 1.0 / jnp.sum(jnp.exp(gl - gm), axis=-1, keepdims=True)
    lo = N_GROUPS + first(gl == gm) * EXP_PER_GROUP
    el = jnp.where((lane >= lo) & (lane < lo + EXP_PER_GROUP), logits, -jnp.inf)
    m1 = jnp.max(el, axis=-1, keepdims=True)
    i1 = first(el == m1)
    el2 = jnp.where(lane == i1, -jnp.inf, el)
    m2 = jnp.max(el2, axis=-1, keepdims=True)
    i2 = first(el2 == m2)
    e2 = jnp.exp(m2 - m1)
    w1 = g_w / (1.0 + e2)
    w2 = g_w * e2 / (1.0 + e2)
    ids = jnp.where(lane == 0.0, i1 - N_GROUPS, jnp.where(lane == 1.0, i2 - N_GROUPS, 0.0))
    id_ref[...] = ids.astype(jnp.int32)
    wt_ref[...] = jnp.where(lane == 0.0, w1, jnp.where(lane == 1.0, w2, 0.0))


def _router(x, w_rt, b_rt):
    tm = ROUTE_TM
    t = x.shape[0]
    return pl.pallas_call(
        _router_kernel,
        grid=(t // tm,),
        in_specs=[pl.BlockSpec((tm, D_MODEL), lambda i: (i, 0)),
                  pl.BlockSpec((D_MODEL, LANES), lambda i: (0, 0)),
                  pl.BlockSpec((1, LANES), lambda i: (0, 0))],
        out_specs=[pl.BlockSpec((tm, LANES), lambda i: (i, 0)), pl.BlockSpec((tm, LANES), lambda i: (i, 0))],
        out_shape=[jax.ShapeDtypeStruct((t, LANES), jnp.int32), jax.ShapeDtypeStruct((t, LANES), F32)],
        compiler_params=_cparams("parallel"),
        name="moe_router",
    )(x, w_rt, b_rt)


def _expert_kernel(te_ref, na_ref, *refs, n_chunks, tiles_per_chunk):
    x_refs = refs[:n_chunks]
    wg_ref, wu_ref, wd_ref, y_ref = refs[n_chunks:]
    j = pl.program_id(0)

    @pl.when(j < na_ref[0])
    def _():
        c = j // tiles_per_chunk
        x = x_refs[n_chunks - 1][...]
        for r in range(n_chunks - 2, -1, -1):
            x = jnp.where(c == r, x_refs[r][...], x)
        hg = _dot(x, wg_ref[0, 0].astype(BF16))
        hu = _dot(x, wu_ref[0, 0].astype(BF16))
        a = hg * jax.nn.sigmoid(hg) * hu
        y_ref[...] = _dot(a.astype(BF16), wd_ref[0, 0].astype(BF16)).astype(y_ref.dtype)

    @pl.when(j >= na_ref[0])
    def _():
        y_ref[...] = jnp.zeros(y_ref.shape, y_ref.dtype)


def _experts(x_chunks, tile_expert, n_active, wg, wu, wd, layer):
    tm = EXP_TM
    n_chunks = len(x_chunks)
    tiles_per_chunk = x_chunks[0].shape[0] // tm
    n_tiles = n_chunks * tiles_per_chunk

    def x_spec(r):
        return pl.BlockSpec((tm, D_MODEL), lambda j, te, na: (
            jnp.clip(jnp.minimum(j, na[0] - 1) - r * tiles_per_chunk, 0, tiles_per_chunk - 1), 0))

    grid_spec = pltpu.PrefetchScalarGridSpec(
        num_scalar_prefetch=2,
        grid=(n_tiles,),
        in_specs=[x_spec(r) for r in range(n_chunks)] + [
            pl.BlockSpec((1, 1, D_MODEL, D_EXPERT), lambda j, te, na: (layer, te[j], 0, 0)),
            pl.BlockSpec((1, 1, D_MODEL, D_EXPERT), lambda j, te, na: (layer, te[j], 0, 0)),
            pl.BlockSpec((1, 1, D_EXPERT, D_MODEL), lambda j, te, na: (layer, te[j], 0, 0))],
        out_specs=pl.BlockSpec((tm, D_MODEL), lambda j, te, na: (j, 0)),
    )
    return pl.pallas_call(
        functools.partial(_expert_kernel, n_chunks=n_chunks, tiles_per_chunk=tiles_per_chunk),
        grid_spec=grid_spec,
        out_shape=jax.ShapeDtypeStruct((n_tiles * tm, D_MODEL), BF16),
        compiler_params=_cparams("arbitrary"),
        name="moe_experts",
    )(tile_expert, n_active, *x_chunks, wg, wu, wd)


def _dispatch_plan(expert_ids, tm, n_tiles):
    t = expert_ids.shape[0]
    e_flat = expert_ids.reshape(-1)
    onehot = (e_flat[:, None] == jnp.arange(N_EXPERTS, dtype=jnp.int32)[None, :]).astype(jnp.int32)
    csum = jnp.cumsum(onehot, axis=0)
    rank = jnp.sum(csum * onehot, axis=1) - 1
    counts = csum[-1]
    padded = ((counts + tm - 1) // tm) * tm
    ends = jnp.cumsum(padded)
    dest = jnp.sum((ends - padded)[None, :] * onehot, axis=1) + rank
    filler = jnp.arange(n_tiles * tm, dtype=jnp.int32) % t
    src_tok = filler.at[dest].set(jnp.arange(2 * t, dtype=jnp.int32) // 2,
                                  mode="promise_in_bounds", unique_indices=True)
    n_active = ends[-1] // tm
    tile_start = jnp.minimum(jnp.arange(n_tiles, dtype=jnp.int32), n_active - 1) * tm
    te = jnp.sum((ends[None, :] <= tile_start[:, None]).astype(jnp.int32), axis=1)
    te = jnp.minimum(te, N_EXPERTS - 1)
    return dest.reshape(t, 2), src_tok, te, n_active.reshape(1).astype(jnp.int32)


def _bf16_const(a):
    a16 = a.astype(BF16)
    assert np.all(a16.astype(np.float32) == a)
    return jnp.asarray(a16)


def _alibi_np(n):
    slopes = np.asarray([2.0 ** (-8.0 * (i + 1) / n) for i in range(n)], np.float32)
    _bf16_const(slopes)
    return slopes


def _selection_overlap_t(seq):
    n_cmp = (seq - L_CMP) // STRIDE_CMP + 1
    n_sel = seq // L_SEL
    c_start = STRIDE_CMP * np.arange(n_cmp)
    s_start = L_SEL * np.arange(n_sel)
    inter = np.clip(np.minimum(c_start[:, None] + L_CMP, s_start[None, :] + L_SEL)
                    - np.maximum(c_start[:, None], s_start[None, :]), 0, None) / L_CMP
    out = np.zeros((LANES, LANES), np.float32)
    out[:n_sel, :n_cmp] = inter.T
    return _bf16_const(out)


def _key_aug(seq, block, pad_rows=0, width=LANES, lanes=(AUG_POS_HI, AUG_POS_LO, AUG_ONE, AUG_PAD)):
    rows = pad_rows + seq
    pos = np.arange(rows)
    key = pos - pad_rows
    real = key >= 0
    out = np.zeros((rows, width), np.float32)
    if block:
        out[pos[real], key[real] // block] = -BIG
    out[:, lanes[0]] = pos // POS_SPLIT
    out[:, lanes[1]] = pos % POS_SPLIT
    out[:, lanes[2]] = 1.0
    out[~real, lanes[3]] = -BIG
    return _bf16_const(out)


def _block_average(seq):
    out = np.zeros((2 * SUBLANES, seq), np.float32)
    for n in range(seq // BLK_C):
        out[n, n * BLK_C:(n + 1) * BLK_C] = 1.0 / BLK_C
    return _bf16_const(out)


def kernel(x, w_in, nsa_cmp_pos, nsa_cmp_w, sink_b, w_br_a, w_br_b, w_br_c, w_out, ln1_g, ln1_b,
           w_group, b_group, w_router, b_router, w_gate, w_up, w_down, ln2_g, ln2_b):
    batch, seq, d = x.shape
    t = batch * seq
    assert d == D_MODEL and w_in.shape[2] == D_IN and seq % BLK_C == 0 and t % MM_TM == 0
    assert seq // L_SEL <= AUG_POS_HI and (seq - L_CMP) // STRIDE_CMP + 1 < LANES
    assert seq // BLK_C <= SUBLANES and (WIN_A + seq) // POS_SPLIT < 256
    n_tiles = (TOPK_EXPERT * t) // EXP_TM + N_EXPERTS
    assert n_tiles % EXP_CHUNKS == 0

    slopes_a = jnp.asarray(_alibi_np(H_A))
    slopes_c = jnp.asarray(_alibi_np(H_C))
    _alibi_np(H_B)
    inter_t = _selection_overlap_t(seq)
    kaug_slc = _key_aug(seq, L_SEL)
    kaug_win = _key_aug(seq, 0, pad_rows=WIN_A)
    kaug_swa = _key_aug(seq, 0, pad_rows=WIN_B, width=HD_B, lanes=(SW_POS_HI, SW_POS_LO, SW_ONE, SW_PAD))
    kaug_moba = _key_aug(seq, BLK_C)
    avg = _block_average(seq)
    take = lambda a, idx: a.at[idx].get(mode="promise_in_bounds")
    sc_take = compute_on("tpu_sparsecore")(jax.jit(take))

    xf = x.reshape(t, d)
    xb = xf.astype(BF16)
    for l in range(DEPTH):
        z = _matmul(xb, _prep_w_in(w_in, l), BF16)
        kvc = _cmp_kv(z, nsa_cmp_pos, nsa_cmp_w, l, batch, seq)
        o_cmp, nsel = _nsa_cmp(z, kvc, slopes_a, inter_t, batch, seq)
        o_slc = _nsa_slc(z, nsel, kaug_slc, slopes_a, batch, seq)
        o_win = _nsa_win(z, kaug_win, slopes_a, batch, seq)
        o_b = _swa(z, kaug_swa, sink_b, l, batch, seq)
        o_c = _moba(z, kaug_moba, avg, slopes_c, batch, seq)
        merged = _merge(o_cmp, o_slc, o_win, o_b, o_c, z,
                        w_br_a[l].astype(BF16), w_br_b[l].astype(BF16), w_br_c[l].astype(BF16))
        xf, xb = _proj_ln(merged, w_out[l].astype(BF16), xf, ln1_g[l][None, :], ln1_b[l][None, :])

        w_rt = jnp.concatenate([w_group[l], w_router[l],
                                jnp.zeros((d, LANES - N_GROUPS - N_EXPERTS), F32)], axis=1)
        b_rt = jnp.concatenate([b_group[l], b_router[l],
                                jnp.zeros((LANES - N_GROUPS - N_EXPERTS,), F32)])[None, :]
        ids, wts = _router(xf, w_rt, b_rt)
        dest, src_tok, tile_expert, n_active = _dispatch_plan(ids[:, :TOPK_EXPERT], EXP_TM, n_tiles)
        x_chunks = [sc_take(xb, c) for c in jnp.split(src_tok, EXP_CHUNKS)]
        y = _experts(x_chunks, tile_expert, n_active, w_gate, w_up, w_down, l)
        ya = take(y, dest[:, 0])
        yb = take(y, dest[:, 1])
        xf, xb = _moe_ln(xf, ya, yb, wts, ln2_g[l][None, :], ln2_b[l][None, :])
    return xf.reshape(batch, seq, d)
```

```python
import functools

import numpy as np
import jax
import jax.numpy as jnp
from jax import lax
from jax.experimental import pallas as pl
from jax.experimental.pallas import tpu as pltpu
from jax.experimental.compute_on import compute_on

F32 = jnp.float32
BF16 = jnp.bfloat16

D_MODEL = 2048
DEPTH = 2
H_A, KV_A, HD_A = 8, 2, 128
HPG_A = H_A // KV_A
L_CMP, STRIDE_CMP, L_SEL, N_SEL, WIN_A = 32, 16, 64, 8, 512
H_B, KV_B, HD_B, WIN_B = 8, 2, 64, 128
HPG_B = H_B // KV_B
H_C, HD_C, BLK_C, TOPK_C = 4, 128, 256, 3
N_GROUPS, EXP_PER_GROUP, D_EXPERT, TOPK_EXPERT = 4, 8, 256, 2
N_EXPERTS = N_GROUPS * EXP_PER_GROUP
LN_EPS = 1e-5
NEG_INF = -1e30
ALPHA = (2.0 * DEPTH) ** 0.25

LANES = 128
SUBLANES = 8
VMEM_LIMIT = 48 * 1024 * 1024
BIG = 2.0 ** 100

Q_A_W = H_A * HD_A
KV_A_W = 3 * 2 * KV_A * HD_A
GATE_A_W = 3 * H_A
Q_B_W = H_B * HD_B
KV_B_W = KV_B * HD_B
C_W = H_C * HD_C
MERGE_W = 3 * D_MODEL
D_IN = Q_A_W + KV_A_W + GATE_A_W + Q_B_W + 2 * KV_B_W + 3 * C_W + MERGE_W
OFF_MERGE = 0
OFF_QA = OFF_MERGE + MERGE_W
OFF_QB = OFF_QA + Q_A_W
OFF_QC = OFF_QB + Q_B_W
OFF_KC = OFF_QC + C_W
OFF_VC = OFF_KC + C_W
OFF_KVA = OFF_VC + C_W
OFF_KB = OFF_KVA + KV_A_W
OFF_VB = OFF_KB + KV_B_W
OFF_GATE_A = OFF_VB + KV_B_W
GATE_A_PAD = 256
Z_W = OFF_GATE_A + GATE_A_PAD
assert OFF_QA % (HPG_A * HD_A) == 0 and OFF_QB % Q_B_W == 0 and OFF_MERGE % D_MODEL == 0
assert OFF_QC % C_W == 0 and OFF_KC % C_W == 0 and OFF_VC % C_W == 0

AUG_POS_HI, AUG_POS_LO, AUG_ONE, AUG_PAD = 32, 33, 34, 35
POS_SPLIT = 256

MM_TM, MM_TN = 1024, 512
ROW_TM = 256
ROUTE_TM = 512
EXP_TM = 256
EXP_CHUNKS = 3
ATT_TQ = 256
SWA_TQ = 128


def _cparams(*sem):
    return pltpu.CompilerParams(dimension_semantics=sem, vmem_limit_bytes=VMEM_LIMIT)


def _dot(a, b):
    return jnp.dot(a, b, preferred_element_type=F32)


def _dot_nt(a, b):
    return lax.dot_general(a, b, (((1,), (1,)), ((), ())), preferred_element_type=F32)


def _iota(shape, dim):
    return lax.broadcasted_iota(jnp.int32, shape, dim)


def _split3(x):
    hi = x.astype(BF16)
    r1 = x - hi.astype(F32)
    mid = r1.astype(BF16)
    lo = (r1 - mid.astype(F32)).astype(BF16)
    return hi, mid, lo


def _pick_lane(x, lane_idx):
    lane = _iota(x.shape, 1)
    return jnp.sum(jnp.where(lane == lane_idx, x, 0.0), axis=-1, keepdims=True)


def _topk_rows(vals, k):
    row = _iota(vals.shape, 0).astype(F32)
    sel = jnp.zeros(vals.shape, F32)
    for _ in range(k):
        m = jnp.max(vals, axis=0, keepdims=True)
        idx = jnp.min(jnp.where(vals == m, row, float(LANES)), axis=0, keepdims=True)
        pick = row == idx
        sel = jnp.where(pick, 1.0, sel)
        vals = jnp.where(pick, -jnp.inf, vals)
    return sel


def _rows_to_lanes(x_t, tq):
    pad = jnp.zeros((LANES - x_t.shape[0], tq), F32)
    return jnp.concatenate([x_t, pad], axis=0).T


def _query_aug(base, slope, center, pad_flag=False):
    lane = _iota(base.shape, 1)
    aug = jnp.where(lane == AUG_POS_HI, slope * float(POS_SPLIT), base)
    aug = jnp.where(lane == AUG_POS_LO, slope, aug)
    aug = jnp.where(lane == AUG_ONE, -slope * center, aug)
    if pad_flag:
        aug = jnp.where(lane == AUG_PAD, 1.0, aug)
    return aug.astype(BF16)


def _softmax_update(s, v, carry):
    m, l, acc = carry
    m_new = jnp.maximum(m, jnp.max(s, axis=-1, keepdims=True))
    alpha = jnp.exp(m - m_new)
    p = jnp.exp(s - m_new)
    l = alpha * l + jnp.sum(p, axis=-1, keepdims=True)
    acc = alpha * acc + _dot(p.astype(BF16), v)
    return m_new, l, acc


def _softmax_init(rows, hd):
    return (jnp.full((rows, 1), NEG_INF, F32), jnp.zeros((rows, 1), F32), jnp.zeros((rows, hd), F32))


def _z_column_sources():
    s_kva = Q_A_W
    s_gate = s_kva + KV_A_W
    s_qb = s_gate + GATE_A_W
    s_kb = s_qb + Q_B_W
    s_qc = s_kb + 2 * KV_B_W
    s_merge = s_qc + 3 * C_W
    segments = [(OFF_MERGE, s_merge, MERGE_W), (OFF_QA, 0, Q_A_W), (OFF_QB, s_qb, Q_B_W),
                (OFF_QC, s_qc, 3 * C_W), (OFF_KVA, s_kva, KV_A_W), (OFF_KB, s_kb, 2 * KV_B_W)]
    src = np.zeros((Z_W // LANES,), np.int32)
    valid = np.zeros((Z_W // LANES,), np.int32)
    for dst, start, width in segments:
        for b in range(width // LANES):
            src[dst // LANES + b] = start + b * LANES
            valid[dst // LANES + b] = LANES
    src[OFF_GATE_A // LANES] = s_gate
    valid[OFF_GATE_A // LANES] = GATE_A_W
    assert np.all(src % SUBLANES == 0) and np.all(src + LANES <= D_IN)
    return jnp.asarray(src // SUBLANES), jnp.asarray(valid)


def _prep_w_in_kernel(src_ref, valid_ref, w_ref, o_ref):
    j = pl.program_id(0)
    w = w_ref[0]
    w = jnp.where(_iota(w.shape, 0) < valid_ref[j], w, 0.0)
    o_ref[...] = w.T.astype(BF16)


def _prep_w_in(w_in_t, layer):
    src, valid = _z_column_sources()
    grid_spec = pltpu.PrefetchScalarGridSpec(
        num_scalar_prefetch=2,
        grid=(Z_W // LANES,),
        in_specs=[pl.BlockSpec((pl.Element(1), pl.Element(LANES), pl.Element(D_MODEL)),
                               lambda j, src, valid: (layer, src[j] * SUBLANES, 0))],
        out_specs=pl.BlockSpec((D_MODEL, LANES), lambda j, src, valid: (0, j)),
    )
    return pl.pallas_call(
        _prep_w_in_kernel,
        grid_spec=grid_spec,
        out_shape=jax.ShapeDtypeStruct((D_MODEL, Z_W), BF16),
        compiler_params=_cparams("parallel"),
        name="w_in_relayout",
    )(src, valid, w_in_t)


def _mm_kernel(a_ref, b_ref, o_ref):
    o_ref[...] = _dot(a_ref[...], b_ref[...]).astype(o_ref.dtype)


def _matmul(a, b, out_dtype):
    m, k = a.shape
    n = b.shape[1]
    return pl.pallas_call(
        _mm_kernel,
        grid=(m // MM_TM, n // MM_TN),
        in_specs=[pl.BlockSpec((MM_TM, k), lambda i, j: (i, 0)),
                  pl.BlockSpec((k, MM_TN), lambda i, j: (0, j))],
        out_specs=pl.BlockSpec((MM_TM, MM_TN), lambda i, j: (i, j)),
        out_shape=jax.ShapeDtypeStruct((m, n), out_dtype),
        compiler_params=_cparams("parallel", "parallel"),
        name="in_proj",
    )(a, b)


def _cmp_kv_kernel(k_ref, pos_ref, w_ref, o_ref, kf_ref, *, seq):
    kf_ref[0:seq, :] = k_ref[...].astype(F32)
    kf_ref[seq:seq + LANES, :] = jnp.zeros((LANES, HD_A), F32)
    acc = jnp.zeros((LANES, HD_A), F32)
    for l in range(L_CMP):
        rows = kf_ref[pl.ds(l, LANES, stride=STRIDE_CMP), :] + pos_ref[0, 0, l:l + 1, :]
        acc = acc + _dot(rows.astype(BF16), w_ref[0, 0, l].astype(BF16))
    o_ref[0, 0] = acc.astype(o_ref.dtype)


def _cmp_kv(z, cmp_pos, cmp_w, layer, batch, seq):
    blk0 = OFF_KVA // LANES
    return pl.pallas_call(
        functools.partial(_cmp_kv_kernel, seq=seq),
        grid=(batch, 2 * KV_A),
        in_specs=[pl.BlockSpec((seq, LANES), lambda b, j: (b, blk0 + j)),
                  pl.BlockSpec((1, 1, L_CMP, HD_A), lambda b, j: (layer, j // KV_A, 0, 0)),
                  pl.BlockSpec((1, 1, L_CMP, HD_A, HD_A), lambda b, j: (layer, j // KV_A, 0, 0, 0))],
        out_specs=pl.BlockSpec((1, 1, LANES, HD_A), lambda b, j: (b, j, 0, 0)),
        out_shape=jax.ShapeDtypeStruct((batch, 2 * KV_A, LANES, HD_A), BF16),
        scratch_shapes=[pltpu.VMEM((seq + LANES, HD_A), F32)],
        compiler_params=_cparams("parallel", "parallel"),
        name="nsa_cmp_kv",
    )(z, cmp_pos, cmp_w)


def _nsa_cmp_kernel(slopes_ref, q_ref, kc_ref, vc_ref, gate_ref, inter_t_ref, o_ref, nsel_ref, *, tq):
    g = pl.program_id(1)
    i = pl.program_id(2)
    shape = (tq, LANES)
    t = i * tq + _iota(shape, 0)
    lane = _iota(shape, 1)
    dist_i = t - (lane * STRIDE_CMP + (L_CMP - 1))
    ok = dist_i >= 0
    okf = ok.astype(F32)
    dist = dist_i.astype(F32)
    kc = kc_ref[0, 0]
    vc = vc_ref[0, 0]
    sig = jax.nn.sigmoid(gate_ref[...].astype(F32))
    scale = HD_A ** -0.5
    psum = jnp.zeros(shape, F32)
    outs = []
    for h in range(HPG_A):
        q = q_ref[:, h * HD_A:(h + 1) * HD_A]
        s = _dot_nt(q, kc) * scale - slopes_ref[g * HPG_A + h] * dist
        s = jnp.where(ok, s, NEG_INF)
        e = jnp.exp(s - jnp.max(s, axis=-1, keepdims=True))
        p = e / jnp.sum(e, axis=-1, keepdims=True) * okf
        psum = psum + p
        o = _dot(p.astype(BF16), vc)
        outs.append(o * _pick_lane(sig, (g * HPG_A + h) * 3 + 0))
    o_ref[...] = jnp.concatenate(outs, axis=1).astype(o_ref.dtype)

    n_sel_rows = 32
    inter_t = inter_t_ref[...]
    imp_t = sum(_dot_nt(inter_t, part) for part in _split3(psum))[0:n_sel_rows]
    shape_t = (n_sel_rows, tq)
    j = _iota(shape_t, 0)
    blk_t = (i * tq + _iota(shape_t, 1)) // L_SEL
    valid = j <= blk_t
    forced = (j == 0) | (j == blk_t) | (j == blk_t - 1)
    vals = jnp.where(forced, jnp.inf, jnp.where(valid, imp_t, -jnp.inf))
    sel = _topk_rows(vals, N_SEL)
    not_selected = jnp.where(valid & (sel > 0.5), 0.0, 1.0)
    nsel_ref[0, 0] = _rows_to_lanes(not_selected, tq).astype(nsel_ref.dtype)


def _nsa_cmp(z, kvc, slopes, inter_t, batch, seq):
    tq = ATT_TQ
    nq = seq // tq
    qw = HPG_A * HD_A
    gate_blk = OFF_GATE_A // LANES
    return pl.pallas_call(
        functools.partial(_nsa_cmp_kernel, tq=tq),
        grid=(batch, KV_A, nq),
        in_specs=[pl.BlockSpec(memory_space=pltpu.SMEM),
                  pl.BlockSpec((tq, qw), lambda b, g, i: (b * nq + i, OFF_QA // qw + g)),
                  pl.BlockSpec((1, 1, LANES, HD_A), lambda b, g, i: (b, g, 0, 0)),
                  pl.BlockSpec((1, 1, LANES, HD_A), lambda b, g, i: (b, KV_A + g, 0, 0)),
                  pl.BlockSpec((tq, LANES), lambda b, g, i: (b * nq + i, gate_blk)),
                  pl.BlockSpec((LANES, LANES), lambda b, g, i: (0, 0))],
        out_specs=[pl.BlockSpec((tq, qw), lambda b, g, i: (b * nq + i, g)),
                   pl.BlockSpec((1, 1, tq, LANES), lambda b, g, i: (b, g, i, 0))],
        out_shape=[jax.ShapeDtypeStruct((batch * seq, Q_A_W), BF16),
                   jax.ShapeDtypeStruct((batch, KV_A, seq, LANES), BF16)],
        compiler_params=_cparams("parallel", "parallel", "parallel"),
        name="nsa_cmp_attn",
    )(slopes, z, kvc, kvc, z, inter_t)


def _scaled_q(q_ref, h, hd, scale):
    return (q_ref[:, h * hd:(h + 1) * hd].astype(F32) * scale).astype(BF16)


def _gated_heads(o, sig, g, branch, tq):
    outs = [o[h * tq:(h + 1) * tq] * _pick_lane(sig, (g * HPG_A + h) * 3 + branch) for h in range(HPG_A)]
    return jnp.concatenate(outs, axis=1)


def _nsa_slc_kernel(slopes_ref, q_ref, k_ref, v_ref, kaug_ref, nsel_ref, gate_ref, o_ref, *, tq):
    g = pl.program_id(1)
    i = pl.program_id(2)
    rows = HPG_A * tq
    scale = HD_A ** -0.5
    nsel = nsel_ref[0, 0].astype(F32)
    center = (i * tq).astype(F32)
    qx = jnp.concatenate(
        [jnp.concatenate([_scaled_q(q_ref, h, HD_A, scale),
                          _query_aug(nsel, slopes_ref[g * HPG_A + h], center)], axis=1)
         for h in range(HPG_A)], axis=0)

    def scores(kt):
        k0 = pl.multiple_of(kt * tq, tq)
        kx = jnp.concatenate([k_ref[pl.ds(k0, tq), :], kaug_ref[pl.ds(k0, tq), :]], axis=1)
        return _dot_nt(qx, kx)

    def values(kt):
        return v_ref[pl.ds(pl.multiple_of(kt * tq, tq), tq), :]

    def body(kt, carry):
        s, m, l, acc = carry
        s_next = scores(kt + 1)
        m, l, acc = _softmax_update(s, values(kt), (m, l, acc))
        return s_next, m, l, acc

    s, m, l, acc = lax.fori_loop(0, i, body, (scores(0),) + _softmax_init(rows, HD_A))
    r = _iota((rows, tq), 0)
    causal = (r - (r // tq) * tq) >= _iota((rows, tq), 1)
    _, l, acc = _softmax_update(jnp.where(causal, s, -BIG), values(i), (m, l, acc))
    sig = jax.nn.sigmoid(gate_ref[...].astype(F32))
    o_ref[...] = _gated_heads(acc / l, sig, g, 1, tq).astype(o_ref.dtype)


def _nsa_slc(z, nsel, kaug, slopes, batch, seq):
    tq = ATT_TQ
    nq = seq // tq
    qw = HPG_A * HD_A
    kblk = OFF_KVA // LANES + 1 * 2 * KV_A
    vblk = kblk + KV_A
    gate_blk = OFF_GATE_A // LANES
    return pl.pallas_call(
        functools.partial(_nsa_slc_kernel, tq=tq),
        grid=(batch, KV_A, nq),
        in_specs=[pl.BlockSpec(memory_space=pltpu.SMEM),
                  pl.BlockSpec((tq, qw), lambda b, g, i: (b * nq + i, OFF_QA // qw + g)),
                  pl.BlockSpec((seq, LANES), lambda b, g, i: (b, kblk + g)),
                  pl.BlockSpec((seq, LANES), lambda b, g, i: (b, vblk + g)),
                  pl.BlockSpec((seq, LANES), lambda b, g, i: (0, 0)),
                  pl.BlockSpec((1, 1, tq, LANES), lambda b, g, i: (b, g, i, 0)),
                  pl.BlockSpec((tq, LANES), lambda b, g, i: (b * nq + i, gate_blk))],
        out_specs=pl.BlockSpec((tq, qw), lambda b, g, i: (b * nq + i, g)),
        out_shape=jax.ShapeDtypeStruct((batch * seq, Q_A_W), BF16),
        compiler_params=_cparams("parallel", "parallel", "parallel"),
        name="nsa_slc_attn",
    )(slopes, z, z, z, kaug, nsel, z)


def _band_mask(tq, span, window, n_heads):
    r = _iota((n_heads * tq, span), 0)
    r = r - (r // tq) * tq
    c = _iota((n_heads * tq, span), 1)
    return (c > r) & (c <= r + window)


def _nsa_win_kernel(slopes_ref, q_ref, k_ref, v_ref, kaug_ref, gate_ref, o_ref, kx_ref, vx_ref, *, tq, window, seq):
    g = pl.program_id(1)
    i = pl.program_id(2)

    @pl.when(i == 0)
    def _():
        kx_ref[0:window, 0:HD_A] = jnp.zeros((window, HD_A), BF16)
        kx_ref[window:window + seq, 0:HD_A] = k_ref[...]
        kx_ref[:, HD_A:2 * HD_A] = kaug_ref[...]
        vx_ref[0:window, :] = jnp.zeros((window, HD_A), BF16)
        vx_ref[window:window + seq, :] = v_ref[...]

    span = window + tq
    r0 = pl.multiple_of(i * tq, tq)
    center = (i * tq + window).astype(F32)
    base = jnp.zeros((tq, LANES), F32)
    qx = jnp.concatenate(
        [jnp.concatenate([_scaled_q(q_ref, h, HD_A, HD_A ** -0.5),
                          _query_aug(base, slopes_ref[g * HPG_A + h], center, pad_flag=True)], axis=1)
         for h in range(HPG_A)], axis=0)
    s = _dot_nt(qx, kx_ref[pl.ds(r0, span), :])
    s = jnp.where(_band_mask(tq, span, window, HPG_A), s, -BIG)
    p = jnp.exp(s - jnp.max(s, axis=-1, keepdims=True))
    o = _dot(p.astype(BF16), vx_ref[pl.ds(r0, span), :]) / jnp.sum(p, axis=-1, keepdims=True)
    sig = jax.nn.sigmoid(gate_ref[...].astype(F32))
    o_ref[...] = _gated_heads(o, sig, g, 2, tq).astype(o_ref.dtype)


def _nsa_win(z, kaug_pad, slopes, batch, seq):
    tq = ATT_TQ
    nq = seq // tq
    qw = HPG_A * HD_A
    kblk = OFF_KVA // LANES + 2 * 2 * KV_A
    vblk = kblk + KV_A
    gate_blk = OFF_GATE_A // LANES
    return pl.pallas_call(
        functools.partial(_nsa_win_kernel, tq=tq, window=WIN_A, seq=seq),
        grid=(batch, KV_A, nq),
        in_specs=[pl.BlockSpec(memory_space=pltpu.SMEM),
                  pl.BlockSpec((tq, qw), lambda b, g, i: (b * nq + i, OFF_QA // qw + g)),
                  pl.BlockSpec((seq, LANES), lambda b, g, i: (b, kblk + g)),
                  pl.BlockSpec((seq, LANES), lambda b, g, i: (b, vblk + g)),
                  pl.BlockSpec((WIN_A + seq, LANES), lambda b, g, i: (0, 0)),
                  pl.BlockSpec((tq, LANES), lambda b, g, i: (b * nq + i, gate_blk))],
        out_specs=pl.BlockSpec((tq, qw), lambda b, g, i: (b * nq + i, g)),
        out_shape=jax.ShapeDtypeStruct((batch * seq, Q_A_W), BF16),
        scratch_shapes=[pltpu.VMEM((WIN_A + seq, 2 * HD_A), BF16), pltpu.VMEM((WIN_A + seq, HD_A), BF16)],
        compiler_params=_cparams("parallel", "parallel", "arbitrary"),
        name="nsa_win_attn",
    )(slopes, z, z, z, kaug_pad, z)


SW_POS_HI, SW_POS_LO, SW_ONE, SW_PAD = 0, 1, 2, 3


def _swa_kernel(sink_ref, q_ref, k_ref, v_ref, kaug_ref, o_ref, kx_ref, vx_ref, *, tq, window, seq):
    i = pl.program_id(1)

    @pl.when(i == 0)
    def _():
        for g in range(KV_B):
            kx_ref[g, 0:window, 0:HD_B] = jnp.zeros((window, HD_B), BF16)
            kx_ref[g, window:window + seq, 0:HD_B] = k_ref[:, g * HD_B:(g + 1) * HD_B]
            kx_ref[g, :, HD_B:2 * HD_B] = kaug_ref[...]
            vx_ref[g, 0:window, :] = jnp.zeros((window, HD_B), BF16)
            vx_ref[g, window:window + seq, :] = v_ref[:, g * HD_B:(g + 1) * HD_B]

    span = window + tq
    rows = HPG_B * tq
    r0 = pl.multiple_of(i * tq, tq)
    center = (i * tq + window).astype(F32)
    band = _band_mask(tq, span, window, HPG_B)
    lane = _iota((tq, HD_B), 1)
    hh = _iota((rows, 1), 0) // tq
    row_in_tile = (_iota((rows, 1), 0) - hh * tq).astype(F32)

    def head_column(values):
        col = jnp.full((rows, 1), values[HPG_B - 1], F32)
        for h in range(HPG_B - 2, -1, -1):
            col = jnp.where(hh == h, values[h], col)
        return col

    outs = []
    for g in range(KV_B):
        slopes = [2.0 ** (-8.0 * (g * HPG_B + h + 1) / H_B) for h in range(HPG_B)]
        parts = []
        for h in range(HPG_B):
            aug = jnp.where(lane == SW_POS_HI, slopes[h] * POS_SPLIT, jnp.where(lane == SW_POS_LO, slopes[h], 0.0))
            aug = jnp.where(lane == SW_ONE, -slopes[h] * center, jnp.where(lane == SW_PAD, 1.0, aug))
            parts.append(jnp.concatenate([_scaled_q(q_ref, g * HPG_B + h, HD_B, HD_B ** -0.5),
                                          aug.astype(BF16)], axis=1))
        qx = jnp.concatenate(parts, axis=0)
        s = _dot_nt(qx, kx_ref[g, pl.ds(r0, span), :])
        s = jnp.where(band, s, -BIG)
        sink_shifted = (head_column([sink_ref[g * HPG_B + h] for h in range(HPG_B)])
                        + head_column(slopes) * row_in_tile)
        m = jnp.maximum(jnp.max(s, axis=-1, keepdims=True), sink_shifted)
        p = jnp.exp(s - m)
        denom = jnp.sum(p, axis=-1, keepdims=True) + jnp.exp(sink_shifted - m)
        o = _dot(p.astype(BF16), vx_ref[g, pl.ds(r0, span), :]) / denom
        outs += [o[h * tq:(h + 1) * tq] for h in range(HPG_B)]
    o_ref[...] = jnp.concatenate(outs, axis=1).astype(o_ref.dtype)


def _swa(z, kaug_sw, sink, layer, batch, seq):
    tq = SWA_TQ
    nq = seq // tq
    return pl.pallas_call(
        functools.partial(_swa_kernel, tq=tq, window=WIN_B, seq=seq),
        grid=(batch, nq),
        in_specs=[pl.BlockSpec(memory_space=pltpu.SMEM),
                  pl.BlockSpec((tq, Q_B_W), lambda b, i: (b * nq + i, OFF_QB // Q_B_W)),
                  pl.BlockSpec((seq, KV_B_W), lambda b, i: (b, OFF_KB // KV_B_W)),
                  pl.BlockSpec((seq, KV_B_W), lambda b, i: (b, OFF_VB // KV_B_W)),
                  pl.BlockSpec((WIN_B + seq, HD_B), lambda b, i: (0, 0))],
        out_specs=pl.BlockSpec((tq, Q_B_W), lambda b, i: (b * nq + i, 0)),
        out_shape=jax.ShapeDtypeStruct((batch * seq, Q_B_W), BF16),
        scratch_shapes=[pltpu.VMEM((KV_B, WIN_B + seq, 2 * HD_B), BF16),
                        pltpu.VMEM((KV_B, WIN_B + seq, HD_B), BF16)],
        compiler_params=_cparams("parallel", "arbitrary"),
        name="swa_attn",
    )(sink[layer], z, z, z, kaug_sw)


def _moba_kernel(slopes_ref, q_ref, k_ref, v_ref, kaug_ref, avg_ref, o_ref, km_ref, *, seq):
    i = pl.program_id(1)
    tq = BLK_C
    n_blk = seq // BLK_C
    scale = HD_C ** -0.5

    @pl.when(i == 0)
    def _():
        for h in range(H_C):
            k_mean = _dot(avg_ref[...], k_ref[:, h * HD_C:(h + 1) * HD_C])[0:SUBLANES]
            terms = [t.astype(F32) for t in _split3(k_mean)] + [jnp.zeros((SUBLANES, HD_C), F32)]
            km_ref[h] = jnp.concatenate(terms, axis=0).astype(BF16)

    center = (i * tq).astype(F32)
    blk = _iota((SUBLANES, tq), 0)
    past = blk < i
    qx = []
    for h in range(H_C):
        q = q_ref[:, h * HD_C:(h + 1) * HD_C]
        sc = _dot_nt(km_ref[h], q)
        score_t = sc[0:SUBLANES] + sc[SUBLANES:2 * SUBLANES] + sc[2 * SUBLANES:3 * SUBLANES]
        sel = _topk_rows(jnp.where(past, score_t, -jnp.inf), TOPK_C)
        not_selected = jnp.where(past & (sel < 0.5), 1.0, 0.0)
        qx.append(jnp.concatenate([_scaled_q(q_ref, h, HD_C, scale),
                                   _query_aug(_rows_to_lanes(not_selected, tq), slopes_ref[h], center)], axis=1))
    assert n_blk <= SUBLANES

    def scores(h, kt):
        k0 = pl.multiple_of(kt * tq, tq)
        kx = jnp.concatenate([k_ref[pl.ds(k0, tq), h * HD_C:(h + 1) * HD_C], kaug_ref[pl.ds(k0, tq), :]], axis=1)
        return _dot_nt(qx[h], kx)

    def values(h, kt):
        return v_ref[pl.ds(pl.multiple_of(kt * tq, tq), tq), h * HD_C:(h + 1) * HD_C]

    def body(kt, carry):
        out = []
        for h in range(H_C):
            s, m, l, acc = carry[h]
            s_next = scores(h, kt + 1)
            out.append((s_next,) + _softmax_update(s, values(h, kt), (m, l, acc)))
        return tuple(out)

    init = tuple((scores(h, 0),) + _softmax_init(tq, HD_C) for h in range(H_C))
    carry = lax.fori_loop(0, i, body, init)
    causal = _iota((tq, tq), 0) >= _iota((tq, tq), 1)
    outs = []
    for h in range(H_C):
        s, m, l, acc = carry[h]
        _, l, acc = _softmax_update(jnp.where(causal, s, -BIG), values(h, i), (m, l, acc))
        outs.append(acc / l)
    o_ref[...] = jnp.concatenate(outs, axis=1).astype(o_ref.dtype)


def _moba(z, kaug, avg, slopes, batch, seq):
    nq = seq // BLK_C
    return pl.pallas_call(
        functools.partial(_moba_kernel, seq=seq),
        grid=(batch, nq),
        in_specs=[pl.BlockSpec(memory_space=pltpu.SMEM),
                  pl.BlockSpec((BLK_C, C_W), lambda b, i: (b * nq + i, OFF_QC // C_W)),
                  pl.BlockSpec((seq, C_W), lambda b, i: (b, OFF_KC // C_W)),
                  pl.BlockSpec((seq, C_W), lambda b, i: (b, OFF_VC // C_W)),
                  pl.BlockSpec((seq, LANES), lambda b, i: (0, 0)),
                  pl.BlockSpec((2 * SUBLANES, seq), lambda b, i: (0, 0))],
        out_specs=pl.BlockSpec((BLK_C, C_W), lambda b, i: (b * nq + i, 0)),
        out_shape=jax.ShapeDtypeStruct((batch * seq, C_W), BF16),
        scratch_shapes=[pltpu.VMEM((H_C, 4 * SUBLANES, HD_C), BF16)],
        compiler_params=_cparams("parallel", "arbitrary"),
        name="moba_attn",
    )(slopes, z, z, z, kaug, avg)


def _merge_kernel(oc_ref, os_ref, ow_ref, ob_ref, om_ref, g0_ref, g1_ref, g2_ref, wa_ref, wb_ref, wc_ref, o_ref):
    o_a = (oc_ref[...].astype(F32) + os_ref[...].astype(F32) + ow_ref[...].astype(F32)).astype(BF16)
    merged = jax.nn.sigmoid(g0_ref[...].astype(F32)) * _dot(o_a, wa_ref[...])
    merged = merged + jax.nn.sigmoid(g1_ref[...].astype(F32)) * _dot(ob_ref[...], wb_ref[...])
    merged = merged + jax.nn.sigmoid(g2_ref[...].astype(F32)) * _dot(om_ref[...], wc_ref[...])
    o_ref[...] = merged.astype(o_ref.dtype)


def _merge(o_cmp, o_slc, o_win, o_b, o_c, z, wa, wb, wc):
    tm = ROW_TM
    t = z.shape[0]
    gblk = OFF_MERGE // D_MODEL
    row = lambda w: pl.BlockSpec((tm, w), lambda i: (i, 0))
    full = lambda a: pl.BlockSpec(a.shape, lambda i: (0, 0))
    gate = lambda r: pl.BlockSpec((tm, D_MODEL), lambda i: (i, gblk + r))
    return pl.pallas_call(
        _merge_kernel,
        grid=(t // tm,),
        in_specs=[row(Q_A_W), row(Q_A_W), row(Q_A_W), row(Q_B_W), row(C_W),
                  gate(0), gate(1), gate(2), full(wa), full(wb), full(wc)],
        out_specs=row(D_MODEL),
        out_shape=jax.ShapeDtypeStruct((t, D_MODEL), BF16),
        compiler_params=_cparams("parallel"),
        name="mixer_merge",
    )(o_cmp, o_slc, o_win, o_b, o_c, z, z, z, wa, wb, wc)


def _layer_norm(h, g_ref, b_ref):
    mu = jnp.mean(h, axis=-1, keepdims=True)
    xc = h - mu
    var = jnp.mean(xc * xc, axis=-1, keepdims=True)
    return xc * lax.rsqrt(var + LN_EPS) * g_ref[...] + b_ref[...]


def _proj_ln_kernel(m_ref, w_ref, x_ref, g_ref, b_ref, xo_ref, xb_ref):
    y = _dot(m_ref[...], w_ref[...])
    out = _layer_norm(ALPHA * x_ref[...] + y, g_ref, b_ref)
    xo_ref[...] = out
    xb_ref[...] = out.astype(BF16)


def _proj_ln(merged, w_out, x, g, b):
    tm = ROW_TM
    t = x.shape[0]
    row = pl.BlockSpec((tm, D_MODEL), lambda i: (i, 0))
    vec = pl.BlockSpec((1, D_MODEL), lambda i: (0, 0))
    return pl.pallas_call(
        _proj_ln_kernel,
        grid=(t // tm,),
        in_specs=[row, pl.BlockSpec((D_MODEL, D_MODEL), lambda i: (0, 0)), row, vec, vec],
        out_specs=[row, row],
        out_shape=[jax.ShapeDtypeStruct((t, D_MODEL), F32), jax.ShapeDtypeStruct((t, D_MODEL), BF16)],
        compiler_params=_cparams("parallel"),
        name="out_proj_ln",
    )(merged, w_out, x, g, b)


def _moe_ln_kernel(x_ref, ya_ref, yb_ref, w_ref, g_ref, b_ref, xo_ref, xb_ref):
    w = w_ref[...]
    y = w[:, 0:1] * ya_ref[...].astype(F32) + w[:, 1:2] * yb_ref[...].astype(F32)
    out = _layer_norm(ALPHA * x_ref[...] + y, g_ref, b_ref)
    xo_ref[...] = out
    xb_ref[...] = out.astype(BF16)


def _moe_ln(x, ya, yb, wts, g, b):
    tm = ROW_TM
    t = x.shape[0]
    row = pl.BlockSpec((tm, D_MODEL), lambda i: (i, 0))
    vec = pl.BlockSpec((1, D_MODEL), lambda i: (0, 0))
    return pl.pallas_call(
        _moe_ln_kernel,
        grid=(t // tm,),
        in_specs=[row, row, row, pl.BlockSpec((tm, LANES), lambda i: (i, 0)), vec, vec],
        out_specs=[row, row],
        out_shape=[jax.ShapeDtypeStruct((t, D_MODEL), F32), jax.ShapeDtypeStruct((t, D_MODEL), BF16)],
        compiler_params=_cparams("parallel"),
        name="moe_combine_ln",
    )(x, ya, yb, wts, g, b)


def _router_kernel(x_ref, wh_ref, wl_ref, b_ref, id_ref, wt_ref):
    x = x_ref[...]
    x_hi = x.astype(BF16)
    x_lo = (x - x_hi.astype(F32)).astype(BF16)
    logits = _dot(x_hi, wh_ref[...]) + _dot(x_lo, wh_ref[...]) + _dot(x_hi, wl_ref[...]) + b_ref[...]
    lane = _iota(logits.shape, 1).astype(F32)
    first = lambda hit: jnp.min(jnp.where(hit, lane, float(LANES)), axis=-1, keepdims=True)
    gl = jnp.where(lane < N_GROUPS, logits, -jnp.inf)
    gm = jnp.max(gl, axis=-1, keepdims=True)
    g_w = 1.0 / jnp.sum(jnp.exp(gl - gm), axis=-1, keepdims=True)
    lo = N_GROUPS + first(gl == gm) * EXP_PER_GROUP
    el = jnp.where((lane >= lo) & (lane < lo + EXP_PER_GROUP), logits, -jnp.inf)
    m1 = jnp.max(el, axis=-1, keepdims=True)
    i1 = first(el == m1)
    el2 = jnp.where(lane == i1, -jnp.inf, el)
    m2 = jnp.max(el2, axis=-1, keepdims=True)
    i2 = first(el2 == m2)
    e2 = jnp.exp(m2 - m1)
    w1 = g_w / (1.0 + e2)
    w2 = g_w * e2 / (1.0 + e2)
    ids = jnp.where(lane == 0.0, i1 - N_GROUPS, jnp.where(lane == 1.0, i2 - N_GROUPS, 0.0))
    id_ref[...] = ids.astype(jnp.int32)
    wt_ref[...] = jnp.where(lane == 0.0, w1, jnp.where(lane == 1.0, w2, 0.0))


def _router(x, w_rt, b_rt):
    tm = ROUTE_TM
    t = x.shape[0]
    w_hi = w_rt.astype(BF16)
    w_lo = (w_rt - w_hi.astype(F32)).astype(BF16)
    return pl.pallas_call(
        _router_kernel,
        grid=(t // tm,),
        in_specs=[pl.BlockSpec((tm, D_MODEL), lambda i: (i, 0)),
                  pl.BlockSpec((D_MODEL, LANES), lambda i: (0, 0)),
                  pl.BlockSpec((D_MODEL, LANES), lambda i: (0, 0)),
                  pl.BlockSpec((1, LANES), lambda i: (0, 0))],
        out_specs=[pl.BlockSpec((tm, LANES), lambda i: (i, 0)), pl.BlockSpec((tm, LANES), lambda i: (i, 0))],
        out_shape=[jax.ShapeDtypeStruct((t, LANES), jnp.int32), jax.ShapeDtypeStruct((t, LANES), F32)],
        compiler_params=_cparams("parallel"),
        name="moe_router",
    )(x, w_hi, w_lo, b_rt)


def _expert_kernel(te_ref, na_ref, *refs, n_chunks, tiles_per_chunk):
    x_refs = refs[:n_chunks]
    wg_ref, wu_ref, wd_ref, y_ref = refs[n_chunks:]
    j = pl.program_id(0)

    @pl.when(j < na_ref[0])
    def _():
        c = j // tiles_per_chunk
        x = x_refs[n_chunks - 1][...]
        for r in range(n_chunks - 2, -1, -1):
            x = jnp.where(c == r, x_refs[r][...], x)
        hg = _dot(x, wg_ref[0, 0].astype(BF16))
        hu = _dot(x, wu_ref[0, 0].astype(BF16))
        a = hg * jax.nn.sigmoid(hg) * hu
        y_ref[...] = _dot(a.astype(BF16), wd_ref[0, 0].astype(BF16)).astype(y_ref.dtype)

    @pl.when(j >= na_ref[0])
    def _():
        y_ref[...] = jnp.zeros(y_ref.shape, y_ref.dtype)


def _experts(x_chunks, tile_expert, n_active, wg, wu, wd, layer):
    tm = EXP_TM
    n_chunks = len(x_chunks)
    tiles_per_chunk = x_chunks[0].shape[0] // tm
    n_tiles = n_chunks * tiles_per_chunk

    def x_spec(r):
        return pl.BlockSpec((tm, D_MODEL), lambda j, te, na: (
            jnp.clip(jnp.minimum(j, na[0] - 1) - r * tiles_per_chunk, 0, tiles_per_chunk - 1), 0))

    grid_spec = pltpu.PrefetchScalarGridSpec(
        num_scalar_prefetch=2,
        grid=(n_tiles,),
        in_specs=[x_spec(r) for r in range(n_chunks)] + [
            pl.BlockSpec((1, 1, D_MODEL, D_EXPERT), lambda j, te, na: (layer, te[j], 0, 0)),
            pl.BlockSpec((1, 1, D_MODEL, D_EXPERT), lambda j, te, na: (layer, te[j], 0, 0)),
            pl.BlockSpec((1, 1, D_EXPERT, D_MODEL), lambda j, te, na: (layer, te[j], 0, 0))],
        out_specs=pl.BlockSpec((tm, D_MODEL), lambda j, te, na: (j, 0)),
    )
    return pl.pallas_call(
        functools.partial(_expert_kernel, n_chunks=n_chunks, tiles_per_chunk=tiles_per_chunk),
        grid_spec=grid_spec,
        out_shape=jax.ShapeDtypeStruct((n_tiles * tm, D_MODEL), BF16),
        compiler_params=_cparams("arbitrary"),
        name="moe_experts",
    )(tile_expert, n_active, *x_chunks, wg, wu, wd)


def _dispatch_plan(expert_ids, tm, n_tiles):
    t = expert_ids.shape[0]
    e_flat = expert_ids.reshape(-1)
    onehot = (e_flat[:, None] == jnp.arange(N_EXPERTS, dtype=jnp.int32)[None, :]).astype(jnp.int32)
    csum = jnp.cumsum(onehot, axis=0)
    rank = jnp.sum(csum * onehot, axis=1) - 1
    counts = csum[-1]
    padded = ((counts + tm - 1) // tm) * tm
    ends = jnp.cumsum(padded)
    dest = jnp.sum((ends - padded)[None, :] * onehot, axis=1) + rank
    filler = jnp.arange(n_tiles * tm, dtype=jnp.int32) % t
    scatter_add = lambda base, idx, vals: base.at[idx].add(vals, mode="promise_in_bounds", unique_indices=True)
    src_tok = compute_on("tpu_sparsecore")(jax.jit(scatter_add))(
        filler, dest, jnp.arange(2 * t, dtype=jnp.int32) // 2 - dest % t)
    n_active = ends[-1] // tm
    tile_start = jnp.minimum(jnp.arange(n_tiles, dtype=jnp.int32), n_active - 1) * tm
    te = jnp.sum((ends[None, :] <= tile_start[:, None]).astype(jnp.int32), axis=1)
    te = jnp.minimum(te, N_EXPERTS - 1)
    return dest.reshape(t, 2), src_tok, te, n_active.reshape(1).astype(jnp.int32)


def _bf16_const(a):
    a16 = a.astype(BF16)
    assert np.all(a16.astype(np.float32) == a)
    return jnp.asarray(a16)


def _alibi_np(n):
    slopes = np.asarray([2.0 ** (-8.0 * (i + 1) / n) for i in range(n)], np.float32)
    _bf16_const(slopes)
    return slopes


def _selection_overlap_t(seq):
    n_cmp = (seq - L_CMP) // STRIDE_CMP + 1
    n_sel = seq // L_SEL
    c_start = STRIDE_CMP * np.arange(n_cmp)
    s_start = L_SEL * np.arange(n_sel)
    inter = np.clip(np.minimum(c_start[:, None] + L_CMP, s_start[None, :] + L_SEL)
                    - np.maximum(c_start[:, None], s_start[None, :]), 0, None) / L_CMP
    out = np.zeros((LANES, LANES), np.float32)
    out[:n_sel, :n_cmp] = inter.T
    return _bf16_const(out)


def _key_aug(seq, block, pad_rows=0, width=LANES, lanes=(AUG_POS_HI, AUG_POS_LO, AUG_ONE, AUG_PAD)):
    rows = pad_rows + seq
    pos = np.arange(rows)
    key = pos - pad_rows
    real = key >= 0
    out = np.zeros((rows, width), np.float32)
    if block:
        out[pos[real], key[real] // block] = -BIG
    out[:, lanes[0]] = pos // POS_SPLIT
    out[:, lanes[1]] = pos % POS_SPLIT
    out[:, lanes[2]] = 1.0
    out[~real, lanes[3]] = -BIG
    return _bf16_const(out)


def _block_average(seq):
    out = np.zeros((2 * SUBLANES, seq), np.float32)
    for n in range(seq // BLK_C):
        out[n, n * BLK_C:(n + 1) * BLK_C] = 1.0 / BLK_C
    return _bf16_const(out)


def kernel(x, w_in, nsa_cmp_pos, nsa_cmp_w, sink_b, w_br_a, w_br_b, w_br_c, w_out, ln1_g, ln1_b,
           w_group, b_group, w_router, b_router, w_gate, w_up, w_down, ln2_g, ln2_b):
    batch, seq, d = x.shape
    t = batch * seq
    assert d == D_MODEL and w_in.shape[2] == D_IN and seq % BLK_C == 0 and t % MM_TM == 0
    assert seq // L_SEL <= AUG_POS_HI and (seq - L_CMP) // STRIDE_CMP + 1 < LANES
    assert seq // BLK_C <= SUBLANES and (WIN_A + seq) // POS_SPLIT < 256
    n_tiles = (TOPK_EXPERT * t) // EXP_TM + N_EXPERTS
    assert n_tiles % EXP_CHUNKS == 0

    slopes_a = jnp.asarray(_alibi_np(H_A))
    slopes_c = jnp.asarray(_alibi_np(H_C))
    _alibi_np(H_B)
    inter_t = _selection_overlap_t(seq)
    kaug_slc = _key_aug(seq, L_SEL)
    kaug_win = _key_aug(seq, 0, pad_rows=WIN_A)
    kaug_swa = _key_aug(seq, 0, pad_rows=WIN_B, width=HD_B, lanes=(SW_POS_HI, SW_POS_LO, SW_ONE, SW_PAD))
    kaug_moba = _key_aug(seq, BLK_C)
    avg = _block_average(seq)
    take = lambda a, idx: a.at[idx].get(mode="promise_in_bounds")
    sc_take = compute_on("tpu_sparsecore")(jax.jit(take))

    w_in_t = jnp.swapaxes(w_in, 1, 2)
    xf = x.reshape(t, d)
    xb = xf.astype(BF16)
    for l in range(DEPTH):
        z = _matmul(xb, _prep_w_in(w_in_t, l), BF16)
        kvc = _cmp_kv(z, nsa_cmp_pos, nsa_cmp_w, l, batch, seq)
        o_cmp, nsel = _nsa_cmp(z, kvc, slopes_a, inter_t, batch, seq)
        o_slc = _nsa_slc(z, nsel, kaug_slc, slopes_a, batch, seq)
        o_win = _nsa_win(z, kaug_win, slopes_a, batch, seq)
        o_b = _swa(z, kaug_swa, sink_b, l, batch, seq)
        o_c = _moba(z, kaug_moba, avg, slopes_c, batch, seq)
        merged = _merge(o_cmp, o_slc, o_win, o_b, o_c, z,
                        w_br_a[l].astype(BF16), w_br_b[l].astype(BF16), w_br_c[l].astype(BF16))
        xf, xb = _proj_ln(merged, w_out[l].astype(BF16), xf, ln1_g[l][None, :], ln1_b[l][None, :])

        w_rt = jnp.concatenate([w_group[l], w_router[l],
                                jnp.zeros((d, LANES - N_GROUPS - N_EXPERTS), F32)], axis=1)
        b_rt = jnp.concatenate([b_group[l], b_router[l],
                                jnp.zeros((LANES - N_GROUPS - N_EXPERTS,), F32)])[None, :]
        ids, wts = _router(xf, w_rt, b_rt)
        dest, src_tok, tile_expert, n_active = _dispatch_plan(ids[:, :TOPK_EXPERT], EXP_TM, n_tiles)
        x_chunks = [sc_take(xb, c) for c in jnp.split(src_tok, EXP_CHUNKS)]
        y = _experts(x_chunks, tile_expert, n_active, w_gate, w_up, w_down, l)
        ya = take(y, dest[:, 0])
        yb = take(y, dest[:, 1])
        xf, xb = _moe_ln(xf, ya, yb, wts, ln2_g[l][None, :], ln2_b[l][None, :])
    return xf.reshape(batch, seq, d)
```

```python
import functools

import numpy as np
import jax
import jax.numpy as jnp
from jax import lax
from jax.experimental import pallas as pl
from jax.experimental.pallas import tpu as pltpu
from jax.experimental.compute_on import compute_on

F32 = jnp.float32
BF16 = jnp.bfloat16

D_MODEL = 2048
DEPTH = 2
H_A, KV_A, HD_A = 8, 2, 128
HPG_A = H_A // KV_A
L_CMP, STRIDE_CMP, L_SEL, N_SEL, WIN_A = 32, 16, 64, 8, 512
H_B, KV_B, HD_B, WIN_B = 8, 2, 64, 128
HPG_B = H_B // KV_B
H_C, HD_C, BLK_C, TOPK_C = 4, 128, 256, 3
N_GROUPS, EXP_PER_GROUP, D_EXPERT, TOPK_EXPERT = 4, 8, 256, 2
N_EXPERTS = N_GROUPS * EXP_PER_GROUP
LN_EPS = 1e-5
NEG_INF = -1e30
ALPHA = (2.0 * DEPTH) ** 0.25

LANES = 128
SUBLANES = 8
VMEM_LIMIT = 48 * 1024 * 1024
BIG = 2.0 ** 100

Q_A_W = H_A * HD_A
KV_A_W = 3 * 2 * KV_A * HD_A
GATE_A_W = 3 * H_A
Q_B_W = H_B * HD_B
KV_B_W = KV_B * HD_B
C_W = H_C * HD_C
MERGE_W = 3 * D_MODEL
D_IN = Q_A_W + KV_A_W + GATE_A_W + Q_B_W + 2 * KV_B_W + 3 * C_W + MERGE_W
OFF_MERGE = 0
OFF_QA = OFF_MERGE + MERGE_W
OFF_QB = OFF_QA + Q_A_W
OFF_QC = OFF_QB + Q_B_W
OFF_KC = OFF_QC + C_W
OFF_VC = OFF_KC + C_W
OFF_KVA = OFF_VC + C_W
OFF_KB = OFF_KVA + KV_A_W
OFF_VB = OFF_KB + KV_B_W
OFF_GATE_A = OFF_VB + KV_B_W
GATE_A_PAD = 256
Z_W = OFF_GATE_A + GATE_A_PAD
assert OFF_QA % (HPG_A * HD_A) == 0 and OFF_QB % Q_B_W == 0 and OFF_MERGE % D_MODEL == 0
assert OFF_QC % C_W == 0 and OFF_KC % C_W == 0 and OFF_VC % C_W == 0

AUG_POS_HI, AUG_POS_LO, AUG_ONE, AUG_PAD = 32, 33, 34, 35
POS_SPLIT = 256

RELAYOUT_COLS = 256
MM_TM, MM_TN = 2048, 512
ROW_TM = 256
ROUTE_TM = 512
EXP_TM = 256
EXP_CHUNKS = 3
ATT_TQ = 256
SWA_TQ = 128


def _cparams(*sem):
    return pltpu.CompilerParams(dimension_semantics=sem, vmem_limit_bytes=VMEM_LIMIT)


def _dot(a, b):
    return jnp.dot(a, b, preferred_element_type=F32)


def _dot_nt(a, b):
    return lax.dot_general(a, b, (((1,), (1,)), ((), ())), preferred_element_type=F32)


def _iota(shape, dim):
    return lax.broadcasted_iota(jnp.int32, shape, dim)


def _split3(x):
    hi = x.astype(BF16)
    r1 = x - hi.astype(F32)
    mid = r1.astype(BF16)
    lo = (r1 - mid.astype(F32)).astype(BF16)
    return hi, mid, lo


def _pick_lane(x, lane_idx):
    lane = _iota(x.shape, 1)
    return jnp.sum(jnp.where(lane == lane_idx, x, 0.0), axis=-1, keepdims=True)


def _topk_rows(vals, k):
    row = _iota(vals.shape, 0).astype(F32)
    sel = jnp.zeros(vals.shape, F32)
    for _ in range(k):
        m = jnp.max(vals, axis=0, keepdims=True)
        idx = jnp.min(jnp.where(vals == m, row, float(LANES)), axis=0, keepdims=True)
        pick = row == idx
        sel = jnp.where(pick, 1.0, sel)
        vals = jnp.where(pick, -jnp.inf, vals)
    return sel


def _rows_to_lanes(x_t, tq):
    pad = jnp.zeros((LANES - x_t.shape[0], tq), F32)
    return jnp.concatenate([x_t, pad], axis=0).T


def _query_aug(base, slope, center, pad_flag=False):
    lane = _iota(base.shape, 1)
    aug = jnp.where(lane == AUG_POS_HI, slope * float(POS_SPLIT), base)
    aug = jnp.where(lane == AUG_POS_LO, slope, aug)
    aug = jnp.where(lane == AUG_ONE, -slope * center, aug)
    if pad_flag:
        aug = jnp.where(lane == AUG_PAD, 1.0, aug)
    return aug.astype(BF16)


def _softmax_update(s, v, carry):
    m, l, acc = carry
    m_new = jnp.maximum(m, jnp.max(s, axis=-1, keepdims=True))
    alpha = jnp.exp(m - m_new)
    p = jnp.exp(s - m_new)
    l = alpha * l + jnp.sum(p, axis=-1, keepdims=True)
    acc = alpha * acc + _dot(p.astype(BF16), v)
    return m_new, l, acc


def _softmax_init(rows, hd):
    return (jnp.full((rows, 1), NEG_INF, F32), jnp.zeros((rows, 1), F32), jnp.zeros((rows, hd), F32))


def _z_column_sources():
    s_kva = Q_A_W
    s_gate = s_kva + KV_A_W
    s_qb = s_gate + GATE_A_W
    s_kb = s_qb + Q_B_W
    s_qc = s_kb + 2 * KV_B_W
    s_merge = s_qc + 3 * C_W
    assert GATE_A_PAD == RELAYOUT_COLS
    segments = [(OFF_MERGE, s_merge, MERGE_W), (OFF_QA, 0, Q_A_W), (OFF_QB, s_qb, Q_B_W),
                (OFF_QC, s_qc, 3 * C_W), (OFF_KVA, s_kva, KV_A_W), (OFF_KB, s_kb, 2 * KV_B_W)]
    src = np.zeros((Z_W // RELAYOUT_COLS,), np.int32)
    valid = np.zeros((Z_W // RELAYOUT_COLS,), np.int32)
    for dst, start, width in segments:
        assert dst % RELAYOUT_COLS == 0 and width % RELAYOUT_COLS == 0
        for b in range(width // RELAYOUT_COLS):
            src[dst // RELAYOUT_COLS + b] = start + b * RELAYOUT_COLS
            valid[dst // RELAYOUT_COLS + b] = RELAYOUT_COLS
    src[OFF_GATE_A // RELAYOUT_COLS] = s_gate
    valid[OFF_GATE_A // RELAYOUT_COLS] = GATE_A_W
    assert np.all(src % SUBLANES == 0) and np.all(src + RELAYOUT_COLS <= D_IN)
    return jnp.asarray(src // SUBLANES), jnp.asarray(valid)


def _prep_w_in_kernel(src_ref, valid_ref, w_ref, o_ref):
    j = pl.program_id(0)
    w = w_ref[0]
    w = jnp.where(_iota(w.shape, 0) < valid_ref[j], w, 0.0)
    o_ref[...] = w.T.astype(BF16)


def _prep_w_in(w_in_t, layer):
    src, valid = _z_column_sources()
    grid_spec = pltpu.PrefetchScalarGridSpec(
        num_scalar_prefetch=2,
        grid=(Z_W // RELAYOUT_COLS,),
        in_specs=[pl.BlockSpec((pl.Element(1), pl.Element(RELAYOUT_COLS), pl.Element(D_MODEL)),
                               lambda j, src, valid: (layer, src[j] * SUBLANES, 0))],
        out_specs=pl.BlockSpec((D_MODEL, RELAYOUT_COLS), lambda j, src, valid: (0, j)),
    )
    return pl.pallas_call(
        _prep_w_in_kernel,
        grid_spec=grid_spec,
        out_shape=jax.ShapeDtypeStruct((D_MODEL, Z_W), BF16),
        compiler_params=_cparams("parallel"),
        name="w_in_relayout",
    )(src, valid, w_in_t)


def _mm_kernel(a_ref, b_ref, o_ref):
    o_ref[...] = _dot(a_ref[...], b_ref[...]).astype(o_ref.dtype)


def _matmul(a, b, out_dtype):
    m, k = a.shape
    n = b.shape[1]
    return pl.pallas_call(
        _mm_kernel,
        grid=(m // MM_TM, n // MM_TN),
        in_specs=[pl.BlockSpec((MM_TM, k), lambda i, j: (i, 0)),
                  pl.BlockSpec((k, MM_TN), lambda i, j: (0, j))],
        out_specs=pl.BlockSpec((MM_TM, MM_TN), lambda i, j: (i, j)),
        out_shape=jax.ShapeDtypeStruct((m, n), out_dtype),
        compiler_params=_cparams("parallel", "parallel"),
        name="in_proj",
    )(a, b)


def _cmp_kv_kernel(k_ref, pos_ref, w_ref, o_ref, kf_ref, *, seq):
    kf_ref[0:seq, :] = k_ref[...].astype(F32)
    kf_ref[seq:seq + LANES, :] = jnp.zeros((LANES, HD_A), F32)
    acc = jnp.zeros((LANES, HD_A), F32)
    for l in range(L_CMP):
        rows = kf_ref[pl.ds(l, LANES, stride=STRIDE_CMP), :] + pos_ref[0, 0, l:l + 1, :]
        acc = acc + _dot(rows.astype(BF16), w_ref[0, 0, l].astype(BF16))
    o_ref[0, 0] = acc.astype(o_ref.dtype)


def _cmp_kv(z, cmp_pos, cmp_w, layer, batch, seq):
    blk0 = OFF_KVA // LANES
    return pl.pallas_call(
        functools.partial(_cmp_kv_kernel, seq=seq),
        grid=(batch, 2 * KV_A),
        in_specs=[pl.BlockSpec((seq, LANES), lambda b, j: (b, blk0 + j)),
                  pl.BlockSpec((1, 1, L_CMP, HD_A), lambda b, j: (layer, j // KV_A, 0, 0)),
                  pl.BlockSpec((1, 1, L_CMP, HD_A, HD_A), lambda b, j: (layer, j // KV_A, 0, 0, 0))],
        out_specs=pl.BlockSpec((1, 1, LANES, HD_A), lambda b, j: (b, j, 0, 0)),
        out_shape=jax.ShapeDtypeStruct((batch, 2 * KV_A, LANES, HD_A), BF16),
        scratch_shapes=[pltpu.VMEM((seq + LANES, HD_A), F32)],
        compiler_params=_cparams("parallel", "parallel"),
        name="nsa_cmp_kv",
    )(z, cmp_pos, cmp_w)


def _nsa_cmp_kernel(slopes_ref, q_ref, kc_ref, vc_ref, gate_ref, inter_t_ref, o_ref, nsel_ref, *, tq):
    g = pl.program_id(1)
    i = pl.program_id(2)
    shape = (tq, LANES)
    t = i * tq + _iota(shape, 0)
    lane = _iota(shape, 1)
    dist_i = t - (lane * STRIDE_CMP + (L_CMP - 1))
    ok = dist_i >= 0
    okf = ok.astype(F32)
    dist = dist_i.astype(F32)
    kc = kc_ref[0, 0]
    vc = vc_ref[0, 0]
    sig = jax.nn.sigmoid(gate_ref[...].astype(F32))
    scale = HD_A ** -0.5
    psum = jnp.zeros(shape, F32)
    outs = []
    for h in range(HPG_A):
        q = q_ref[:, h * HD_A:(h + 1) * HD_A]
        s = _dot_nt(q, kc) * scale - slopes_ref[g * HPG_A + h] * dist
        s = jnp.where(ok, s, NEG_INF)
        e = jnp.exp(s - jnp.max(s, axis=-1, keepdims=True))
        p = e / jnp.sum(e, axis=-1, keepdims=True) * okf
        psum = psum + p
        o = _dot(p.astype(BF16), vc)
        outs.append(o * _pick_lane(sig, (g * HPG_A + h) * 3 + 0))
    o_ref[...] = jnp.concatenate(outs, axis=1).astype(o_ref.dtype)

    n_sel_rows = 32
    inter_t = inter_t_ref[...]
    imp_t = sum(_dot_nt(inter_t, part) for part in _split3(psum))[0:n_sel_rows]
    shape_t = (n_sel_rows, tq)
    j = _iota(shape_t, 0)
    blk_t = (i * tq + _iota(shape_t, 1)) // L_SEL
    valid = j <= blk_t
    forced = (j == 0) | (j == blk_t) | (j == blk_t - 1)
    vals = jnp.where(forced, jnp.inf, jnp.where(valid, imp_t, -jnp.inf))
    sel = _topk_rows(vals, N_SEL)
    not_selected = jnp.where(valid & (sel > 0.5), 0.0, 1.0)
    nsel_ref[0, 0] = _rows_to_lanes(not_selected, tq).astype(nsel_ref.dtype)


def _nsa_cmp(z, kvc, slopes, inter_t, batch, seq):
    tq = ATT_TQ
    nq = seq // tq
    qw = HPG_A * HD_A
    gate_blk = OFF_GATE_A // LANES
    return pl.pallas_call(
        functools.partial(_nsa_cmp_kernel, tq=tq),
        grid=(batch, KV_A, nq),
        in_specs=[pl.BlockSpec(memory_space=pltpu.SMEM),
                  pl.BlockSpec((tq, qw), lambda b, g, i: (b * nq + i, OFF_QA // qw + g)),
                  pl.BlockSpec((1, 1, LANES, HD_A), lambda b, g, i: (b, g, 0, 0)),
                  pl.BlockSpec((1, 1, LANES, HD_A), lambda b, g, i: (b, KV_A + g, 0, 0)),
                  pl.BlockSpec((tq, LANES), lambda b, g, i: (b * nq + i, gate_blk)),
                  pl.BlockSpec((LANES, LANES), lambda b, g, i: (0, 0))],
        out_specs=[pl.BlockSpec((tq, qw), lambda b, g, i: (b * nq + i, g)),
                   pl.BlockSpec((1, 1, tq, LANES), lambda b, g, i: (b, g, i, 0))],
        out_shape=[jax.ShapeDtypeStruct((batch * seq, Q_A_W), BF16),
                   jax.ShapeDtypeStruct((batch, KV_A, seq, LANES), BF16)],
        compiler_params=_cparams("parallel", "parallel", "parallel"),
        name="nsa_cmp_attn",
    )(slopes, z, kvc, kvc, z, inter_t)


def _scaled_q(q_ref, h, hd, scale):
    return (q_ref[:, h * hd:(h + 1) * hd].astype(F32) * scale).astype(BF16)


def _gated_heads(o, sig, g, branch, tq):
    outs = [o[h * tq:(h + 1) * tq] * _pick_lane(sig, (g * HPG_A + h) * 3 + branch) for h in range(HPG_A)]
    return jnp.concatenate(outs, axis=1)


def _nsa_slc_kernel(slopes_ref, q_ref, k_ref, v_ref, kaug_ref, nsel_ref, gate_ref, o_ref, *, tq):
    g = pl.program_id(1)
    i = pl.program_id(2)
    rows = HPG_A * tq
    scale = HD_A ** -0.5
    nsel = nsel_ref[0, 0].astype(F32)
    center = (i * tq).astype(F32)
    qx = jnp.concatenate(
        [jnp.concatenate([_scaled_q(q_ref, h, HD_A, scale),
                          _query_aug(nsel, slopes_ref[g * HPG_A + h], center)], axis=1)
         for h in range(HPG_A)], axis=0)

    def scores(kt):
        k0 = pl.multiple_of(kt * tq, tq)
        kx = jnp.concatenate([k_ref[pl.ds(k0, tq), :], kaug_ref[pl.ds(k0, tq), :]], axis=1)
        return _dot_nt(qx, kx)

    def values(kt):
        return v_ref[pl.ds(pl.multiple_of(kt * tq, tq), tq), :]

    def body(kt, carry):
        s, m, l, acc = carry
        s_next = scores(kt + 1)
        m, l, acc = _softmax_update(s, values(kt), (m, l, acc))
        return s_next, m, l, acc

    s, m, l, acc = lax.fori_loop(0, i, body, (scores(0),) + _softmax_init(rows, HD_A))
    r = _iota((rows, tq), 0)
    causal = (r - (r // tq) * tq) >= _iota((rows, tq), 1)
    _, l, acc = _softmax_update(jnp.where(causal, s, -BIG), values(i), (m, l, acc))
    sig = jax.nn.sigmoid(gate_ref[...].astype(F32))
    o_ref[...] = _gated_heads(acc / l, sig, g, 1, tq).astype(o_ref.dtype)


def _nsa_slc(z, nsel, kaug, slopes, batch, seq):
    tq = ATT_TQ
    nq = seq // tq
    qw = HPG_A * HD_A
    kblk = OFF_KVA // LANES + 1 * 2 * KV_A
    vblk = kblk + KV_A
    gate_blk = OFF_GATE_A // LANES
    return pl.pallas_call(
        functools.partial(_nsa_slc_kernel, tq=tq),
        grid=(batch, KV_A, nq),
        in_specs=[pl.BlockSpec(memory_space=pltpu.SMEM),
                  pl.BlockSpec((tq, qw), lambda b, g, i: (b * nq + i, OFF_QA // qw + g)),
                  pl.BlockSpec((seq, LANES), lambda b, g, i: (b, kblk + g)),
                  pl.BlockSpec((seq, LANES), lambda b, g, i: (b, vblk + g)),
                  pl.BlockSpec((seq, LANES), lambda b, g, i: (0, 0)),
                  pl.BlockSpec((1, 1, tq, LANES), lambda b, g, i: (b, g, i, 0)),
                  pl.BlockSpec((tq, LANES), lambda b, g, i: (b * nq + i, gate_blk))],
        out_specs=pl.BlockSpec((tq, qw), lambda b, g, i: (b * nq + i, g)),
        out_shape=jax.ShapeDtypeStruct((batch * seq, Q_A_W), BF16),
        compiler_params=_cparams("parallel", "parallel", "parallel"),
        name="nsa_slc_attn",
    )(slopes, z, z, z, kaug, nsel, z)


def _band_mask(tq, span, window, n_heads):
    r = _iota((n_heads * tq, span), 0)
    r = r - (r // tq) * tq
    c = _iota((n_heads * tq, span), 1)
    return (c > r) & (c <= r + window)


def _nsa_win_kernel(slopes_ref, q_ref, k_ref, v_ref, kaug_ref, gate_ref, o_ref, kx_ref, vx_ref, *, tq, window, seq):
    g = pl.program_id(1)
    i = pl.program_id(2)

    @pl.when(i == 0)
    def _():
        kx_ref[0:window, 0:HD_A] = jnp.zeros((window, HD_A), BF16)
        kx_ref[window:window + seq, 0:HD_A] = k_ref[...]
        kx_ref[:, HD_A:2 * HD_A] = kaug_ref[...]
        vx_ref[0:window, :] = jnp.zeros((window, HD_A), BF16)
        vx_ref[window:window + seq, :] = v_ref[...]

    span = window + tq
    r0 = pl.multiple_of(i * tq, tq)
    center = (i * tq + window).astype(F32)
    base = jnp.zeros((tq, LANES), F32)
    qx = jnp.concatenate(
        [jnp.concatenate([_scaled_q(q_ref, h, HD_A, HD_A ** -0.5),
                          _query_aug(base, slopes_ref[g * HPG_A + h], center, pad_flag=True)], axis=1)
         for h in range(HPG_A)], axis=0)
    s = _dot_nt(qx, kx_ref[pl.ds(r0, span), :])
    s = jnp.where(_band_mask(tq, span, window, HPG_A), s, -BIG)
    p = jnp.exp(s - jnp.max(s, axis=-1, keepdims=True))
    o = _dot(p.astype(BF16), vx_ref[pl.ds(r0, span), :]) / jnp.sum(p, axis=-1, keepdims=True)
    sig = jax.nn.sigmoid(gate_ref[...].astype(F32))
    o_ref[...] = _gated_heads(o, sig, g, 2, tq).astype(o_ref.dtype)


def _nsa_win(z, kaug_pad, slopes, batch, seq):
    tq = ATT_TQ
    nq = seq // tq
    qw = HPG_A * HD_A
    kblk = OFF_KVA // LANES + 2 * 2 * KV_A
    vblk = kblk + KV_A
    gate_blk = OFF_GATE_A // LANES
    return pl.pallas_call(
        functools.partial(_nsa_win_kernel, tq=tq, window=WIN_A, seq=seq),
        grid=(batch, KV_A, nq),
        in_specs=[pl.BlockSpec(memory_space=pltpu.SMEM),
                  pl.BlockSpec((tq, qw), lambda b, g, i: (b * nq + i, OFF_QA // qw + g)),
                  pl.BlockSpec((seq, LANES), lambda b, g, i: (b, kblk + g)),
                  pl.BlockSpec((seq, LANES), lambda b, g, i: (b, vblk + g)),
                  pl.BlockSpec((WIN_A + seq, LANES), lambda b, g, i: (0, 0)),
                  pl.BlockSpec((tq, LANES), lambda b, g, i: (b * nq + i, gate_blk))],
        out_specs=pl.BlockSpec((tq, qw), lambda b, g, i: (b * nq + i, g)),
        out_shape=jax.ShapeDtypeStruct((batch * seq, Q_A_W), BF16),
        scratch_shapes=[pltpu.VMEM((WIN_A + seq, 2 * HD_A), BF16), pltpu.VMEM((WIN_A + seq, HD_A), BF16)],
        compiler_params=_cparams("parallel", "parallel", "arbitrary"),
        name="nsa_win_attn",
    )(slopes, z, z, z, kaug_pad, z)


SW_POS_HI, SW_POS_LO, SW_ONE, SW_PAD = 0, 1, 2, 3


def _swa_kernel(sink_ref, q_ref, k_ref, v_ref, kaug_ref, o_ref, kx_ref, vx_ref, *, tq, window, seq):
    i = pl.program_id(1)

    @pl.when(i == 0)
    def _():
        for g in range(KV_B):
            kx_ref[g, 0:window, 0:HD_B] = jnp.zeros((window, HD_B), BF16)
            kx_ref[g, window:window + seq, 0:HD_B] = k_ref[:, g * HD_B:(g + 1) * HD_B]
            kx_ref[g, :, HD_B:2 * HD_B] = kaug_ref[...]
            vx_ref[g, 0:window, :] = jnp.zeros((window, HD_B), BF16)
            vx_ref[g, window:window + seq, :] = v_ref[:, g * HD_B:(g + 1) * HD_B]

    span = window + tq
    rows = HPG_B * tq
    r0 = pl.multiple_of(i * tq, tq)
    center = (i * tq + window).astype(F32)
    band = _band_mask(tq, span, window, HPG_B)
    lane = _iota((tq, HD_B), 1)
    hh = _iota((rows, 1), 0) // tq
    row_in_tile = (_iota((rows, 1), 0) - hh * tq).astype(F32)

    def head_column(values):
        col = jnp.full((rows, 1), values[HPG_B - 1], F32)
        for h in range(HPG_B - 2, -1, -1):
            col = jnp.where(hh == h, values[h], col)
        return col

    outs = []
    for g in range(KV_B):
        slopes = [2.0 ** (-8.0 * (g * HPG_B + h + 1) / H_B) for h in range(HPG_B)]
        parts = []
        for h in range(HPG_B):
            aug = jnp.where(lane == SW_POS_HI, slopes[h] * POS_SPLIT, jnp.where(lane == SW_POS_LO, slopes[h], 0.0))
            aug = jnp.where(lane == SW_ONE, -slopes[h] * center, jnp.where(lane == SW_PAD, 1.0, aug))
            parts.append(jnp.concatenate([_scaled_q(q_ref, g * HPG_B + h, HD_B, HD_B ** -0.5),
                                          aug.astype(BF16)], axis=1))
        qx = jnp.concatenate(parts, axis=0)
        s = _dot_nt(qx, kx_ref[g, pl.ds(r0, span), :])
        s = jnp.where(band, s, -BIG)
        sink_shifted = (head_column([sink_ref[g * HPG_B + h] for h in range(HPG_B)])
                        + head_column(slopes) * row_in_tile)
        m = jnp.maximum(jnp.max(s, axis=-1, keepdims=True), sink_shifted)
        p = jnp.exp(s - m)
        denom = jnp.sum(p, axis=-1, keepdims=True) + jnp.exp(sink_shifted - m)
        o = _dot(p.astype(BF16), vx_ref[g, pl.ds(r0, span), :]) / denom
        outs += [o[h * tq:(h + 1) * tq] for h in range(HPG_B)]
    o_ref[...] = jnp.concatenate(outs, axis=1).astype(o_ref.dtype)


def _swa(z, kaug_sw, sink, layer, batch, seq):
    tq = SWA_TQ
    nq = seq // tq
    return pl.pallas_call(
        functools.partial(_swa_kernel, tq=tq, window=WIN_B, seq=seq),
        grid=(batch, nq),
        in_specs=[pl.BlockSpec(memory_space=pltpu.SMEM),
                  pl.BlockSpec((tq, Q_B_W), lambda b, i: (b * nq + i, OFF_QB // Q_B_W)),
                  pl.BlockSpec((seq, KV_B_W), lambda b, i: (b, OFF_KB // KV_B_W)),
                  pl.BlockSpec((seq, KV_B_W), lambda b, i: (b, OFF_VB // KV_B_W)),
                  pl.BlockSpec((WIN_B + seq, HD_B), lambda b, i: (0, 0))],
        out_specs=pl.BlockSpec((tq, Q_B_W), lambda b, i: (b * nq + i, 0)),
        out_shape=jax.ShapeDtypeStruct((batch * seq, Q_B_W), BF16),
        scratch_shapes=[pltpu.VMEM((KV_B, WIN_B + seq, 2 * HD_B), BF16),
                        pltpu.VMEM((KV_B, WIN_B + seq, HD_B), BF16)],
        compiler_params=_cparams("parallel", "arbitrary"),
        name="swa_attn",
    )(sink[layer], z, z, z, kaug_sw)


def _moba_kernel(slopes_ref, q_ref, k_ref, v_ref, kaug_ref, avg_ref, o_ref, km_ref, *, seq):
    i = pl.program_id(1)
    tq = BLK_C
    n_blk = seq // BLK_C
    scale = HD_C ** -0.5

    @pl.when(i == 0)
    def _():
        for h in range(H_C):
            k_mean = _dot(avg_ref[...], k_ref[:, h * HD_C:(h + 1) * HD_C])[0:SUBLANES]
            terms = [t.astype(F32) for t in _split3(k_mean)] + [jnp.zeros((SUBLANES, HD_C), F32)]
            km_ref[h] = jnp.concatenate(terms, axis=0).astype(BF16)

    center = (i * tq).astype(F32)
    blk = _iota((SUBLANES, tq), 0)
    past = blk < i
    qx = []
    for h in range(H_C):
        q = q_ref[:, h * HD_C:(h + 1) * HD_C]
        sc = _dot_nt(km_ref[h], q)
        score_t = sc[0:SUBLANES] + sc[SUBLANES:2 * SUBLANES] + sc[2 * SUBLANES:3 * SUBLANES]
        sel = _topk_rows(jnp.where(past, score_t, -jnp.inf), TOPK_C)
        not_selected = jnp.where(past & (sel < 0.5), 1.0, 0.0)
        qx.append(jnp.concatenate([_scaled_q(q_ref, h, HD_C, scale),
                                   _query_aug(_rows_to_lanes(not_selected, tq), slopes_ref[h], center)], axis=1))
    assert n_blk <= SUBLANES

    def scores(h, kt):
        k0 = pl.multiple_of(kt * tq, tq)
        kx = jnp.concatenate([k_ref[pl.ds(k0, tq), h * HD_C:(h + 1) * HD_C], kaug_ref[pl.ds(k0, tq), :]], axis=1)
        return _dot_nt(qx[h], kx)

    def values(h, kt):
        return v_ref[pl.ds(pl.multiple_of(kt * tq, tq), tq), h * HD_C:(h + 1) * HD_C]

    def body(kt, carry):
        out = []
        for h in range(H_C):
            s, m, l, acc = carry[h]
            s_next = scores(h, kt + 1)
            out.append((s_next,) + _softmax_update(s, values(h, kt), (m, l, acc)))
        return tuple(out)

    init = tuple((scores(h, 0),) + _softmax_init(tq, HD_C) for h in range(H_C))
    carry = lax.fori_loop(0, i, body, init)
    causal = _iota((tq, tq), 0) >= _iota((tq, tq), 1)
    outs = []
    for h in range(H_C):
        s, m, l, acc = carry[h]
        _, l, acc = _softmax_update(jnp.where(causal, s, -BIG), values(h, i), (m, l, acc))
        outs.append(acc / l)
    o_ref[...] = jnp.concatenate(outs, axis=1).astype(o_ref.dtype)


def _moba(z, kaug, avg, slopes, batch, seq):
    nq = seq // BLK_C
    return pl.pallas_call(
        functools.partial(_moba_kernel, seq=seq),
        grid=(batch, nq),
        in_specs=[pl.BlockSpec(memory_space=pltpu.SMEM),
                  pl.BlockSpec((BLK_C, C_W), lambda b, i: (b * nq + i, OFF_QC // C_W)),
                  pl.BlockSpec((seq, C_W), lambda b, i: (b, OFF_KC // C_W)),
                  pl.BlockSpec((seq, C_W), lambda b, i: (b, OFF_VC // C_W)),
                  pl.BlockSpec((seq, LANES), lambda b, i: (0, 0)),
                  pl.BlockSpec((2 * SUBLANES, seq), lambda b, i: (0, 0))],
        out_specs=pl.BlockSpec((BLK_C, C_W), lambda b, i: (b * nq + i, 0)),
        out_shape=jax.ShapeDtypeStruct((batch * seq, C_W), BF16),
        scratch_shapes=[pltpu.VMEM((H_C, 4 * SUBLANES, HD_C), BF16)],
        compiler_params=_cparams("parallel", "arbitrary"),
        name="moba_attn",
    )(slopes, z, z, z, kaug, avg)


def _merge_kernel(oc_ref, os_ref, ow_ref, ob_ref, om_ref, g0_ref, g1_ref, g2_ref, wa_ref, wb_ref, wc_ref, o_ref):
    o_a = (oc_ref[...].astype(F32) + os_ref[...].astype(F32) + ow_ref[...].astype(F32)).astype(BF16)
    merged = jax.nn.sigmoid(g0_ref[...].astype(F32)) * _dot(o_a, wa_ref[...])
    merged = merged + jax.nn.sigmoid(g1_ref[...].astype(F32)) * _dot(ob_ref[...], wb_ref[...])
    merged = merged + jax.nn.sigmoid(g2_ref[...].astype(F32)) * _dot(om_ref[...], wc_ref[...])
    o_ref[...] = merged.astype(o_ref.dtype)


def _merge(o_cmp, o_slc, o_win, o_b, o_c, z, wa, wb, wc):
    tm = ROW_TM
    t = z.shape[0]
    gblk = OFF_MERGE // D_MODEL
    row = lambda w: pl.BlockSpec((tm, w), lambda i: (i, 0))
    full = lambda a: pl.BlockSpec(a.shape, lambda i: (0, 0))
    gate = lambda r: pl.BlockSpec((tm, D_MODEL), lambda i: (i, gblk + r))
    return pl.pallas_call(
        _merge_kernel,
        grid=(t // tm,),
        in_specs=[row(Q_A_W), row(Q_A_W), row(Q_A_W), row(Q_B_W), row(C_W),
                  gate(0), gate(1), gate(2), full(wa), full(wb), full(wc)],
        out_specs=row(D_MODEL),
        out_shape=jax.ShapeDtypeStruct((t, D_MODEL), BF16),
        compiler_params=_cparams("parallel"),
        name="mixer_merge",
    )(o_cmp, o_slc, o_win, o_b, o_c, z, z, z, wa, wb, wc)


def _layer_norm(h, g_ref, b_ref):
    mu = jnp.mean(h, axis=-1, keepdims=True)
    xc = h - mu
    var = jnp.mean(xc * xc, axis=-1, keepdims=True)
    return xc * lax.rsqrt(var + LN_EPS) * g_ref[...] + b_ref[...]


def _proj_ln_kernel(m_ref, w_ref, x_ref, g_ref, b_ref, xo_ref, xb_ref):
    y = _dot(m_ref[...], w_ref[...])
    out = _layer_norm(ALPHA * x_ref[...] + y, g_ref, b_ref)
    xo_ref[...] = out
    xb_ref[...] = out.astype(BF16)


def _proj_ln(merged, w_out, x, g, b):
    tm = ROW_TM
    t = x.shape[0]
    row = pl.BlockSpec((tm, D_MODEL), lambda i: (i, 0))
    vec = pl.BlockSpec((1, D_MODEL), lambda i: (0, 0))
    return pl.pallas_call(
        _proj_ln_kernel,
        grid=(t // tm,),
        in_specs=[row, pl.BlockSpec((D_MODEL, D_MODEL), lambda i: (0, 0)), row, vec, vec],
        out_specs=[row, row],
        out_shape=[jax.ShapeDtypeStruct((t, D_MODEL), F32), jax.ShapeDtypeStruct((t, D_MODEL), BF16)],
        compiler_params=_cparams("parallel"),
        name="out_proj_ln",
    )(merged, w_out, x, g, b)


def _moe_ln_kernel(x_ref, ya_ref, yb_ref, w_ref, g_ref, b_ref, xo_ref, xb_ref):
    w = w_ref[...]
    y = w[:, 0:1] * ya_ref[...].astype(F32) + w[:, 1:2] * yb_ref[...].astype(F32)
    out = _layer_norm(ALPHA * x_ref[...] + y, g_ref, b_ref)
    xo_ref[...] = out
    xb_ref[...] = out.astype(BF16)


def _moe_ln(x, ya, yb, wts, g, b):
    tm = ROW_TM
    t = x.shape[0]
    row = pl.BlockSpec((tm, D_MODEL), lambda i: (i, 0))
    vec = pl.BlockSpec((1, D_MODEL), lambda i: (0, 0))
    return pl.pallas_call(
        _moe_ln_kernel,
        grid=(t // tm,),
        in_specs=[row, row, row, pl.BlockSpec((tm, LANES), lambda i: (i, 0)), vec, vec],
        out_specs=[row, row],
        out_shape=[jax.ShapeDtypeStruct((t, D_MODEL), F32), jax.ShapeDtypeStruct((t, D_MODEL), BF16)],
        compiler_params=_cparams("parallel"),
        name="moe_combine_ln",
    )(x, ya, yb, wts, g, b)


def _router_kernel(x_ref, w_ref, b_ref, id_ref, wt_ref):
    x = x_ref[...]
    x_hi = x.astype(BF16)
    x_lo = (x - x_hi.astype(F32)).astype(BF16)
    w = w_ref[...]
    w_hi = w.astype(BF16)
    w_lo = (w - w_hi.astype(F32)).astype(BF16)
    logits = _dot(x_hi, w_hi) + _dot(x_lo, w_hi) + _dot(x_hi, w_lo) + b_ref[...]
    lane = _iota(logits.shape, 1).astype(F32)
    first = lambda hit: jnp.min(jnp.where(hit, lane, float(LANES)), axis=-1, keepdims=True)
    gl = jnp.where(lane < N_GROUPS, logits, -jnp.inf)
    gm = jnp.max(gl, axis=-1, keepdims=True)
    g_w = 1.0 / jnp.sum(jnp.exp(gl - gm), axis=-1, keepdims=True)
    lo = N_GROUPS + first(gl == gm) * EXP_PER_GROUP
    el = jnp.where((lane >= lo) & (lane < lo + EXP_PER_GROUP), logits, -jnp.inf)
    m1 = jnp.max(el, axis=-1, keepdims=True)
    i1 = first(el == m1)
    el2 = jnp.where(lane == i1, -jnp.inf, el)
    m2 = jnp.max(el2, axis=-1, keepdims=True)
    i2 = first(el2 == m2)
    e2 = jnp.exp(m2 - m1)
    w1 = g_w / (1.0 + e2)
    w2 = g_w * e2 / (1.0 + e2)
    ids = jnp.where(lane == 0.0, i1 - N_GROUPS, jnp.where(lane == 1.0, i2 - N_GROUPS, 0.0))
    id_ref[...] = ids.astype(jnp.int32)
    wt_ref[...] = jnp.where(lane == 0.0, w1, jnp.where(lane == 1.0, w2, 0.0))


def _router(x, w_rt, b_rt):
    tm = ROUTE_TM
    t = x.shape[0]
    return pl.pallas_call(
        _router_kernel,
        grid=(t // tm,),
        in_specs=[pl.BlockSpec((tm, D_MODEL), lambda i: (i, 0)),
                  pl.BlockSpec((D_MODEL, LANES), lambda i: (0, 0)),
                  pl.BlockSpec((1, LANES), lambda i: (0, 0))],
        out_specs=[pl.BlockSpec((tm, LANES), lambda i: (i, 0)), pl.BlockSpec((tm, LANES), lambda i: (i, 0))],
        out_shape=[jax.ShapeDtypeStruct((t, LANES), jnp.int32), jax.ShapeDtypeStruct((t, LANES), F32)],
        compiler_params=_cparams("parallel"),
        name="moe_router",
    )(x, w_rt, b_rt)


def _expert_kernel(te_ref, na_ref, x_ref, wg_ref, wu_ref, wd_ref, *rest, tile0):
    y_ref = rest[-1]
    j = tile0 + pl.program_id(0)

    @pl.when(j < na_ref[0])
    def _():
        x = x_ref[...]
        hg = _dot(x, wg_ref[0, 0].astype(BF16))
        hu = _dot(x, wu_ref[0, 0].astype(BF16))
        a = hg * jax.nn.sigmoid(hg) * hu
        y_ref[...] = _dot(a.astype(BF16), wd_ref[0, 0].astype(BF16)).astype(y_ref.dtype)

    @pl.when(j >= na_ref[0])
    def _():
        y_ref[...] = jnp.zeros(y_ref.shape, y_ref.dtype)


def _experts(x_chunks, tile_expert, n_active, wg, wu, wd, layer):
    tm = EXP_TM
    tiles_per_chunk = x_chunks[0].shape[0] // tm
    n_tiles = len(x_chunks) * tiles_per_chunk
    y = None
    for c, x_c in enumerate(x_chunks):
        tile0 = c * tiles_per_chunk
        x_map = lambda j, te, na, tile0=tile0: (
            jnp.clip(jnp.minimum(tile0 + j, na[0] - 1) - tile0, 0, tiles_per_chunk - 1), 0)
        w_map = lambda j, te, na, tile0=tile0: (layer, te[tile0 + j], 0, 0)
        in_specs = [pl.BlockSpec((tm, D_MODEL), x_map),
                    pl.BlockSpec((1, 1, D_MODEL, D_EXPERT), w_map),
                    pl.BlockSpec((1, 1, D_MODEL, D_EXPERT), w_map),
                    pl.BlockSpec((1, 1, D_EXPERT, D_MODEL), w_map)]
        args = [tile_expert, n_active, x_c, wg, wu, wd]
        aliases = {}
        if y is not None:
            in_specs.append(pl.BlockSpec(memory_space=pl.ANY))
            aliases = {len(args): 0}
            args.append(y)
        grid_spec = pltpu.PrefetchScalarGridSpec(
            num_scalar_prefetch=2,
            grid=(tiles_per_chunk,),
            in_specs=in_specs,
            out_specs=pl.BlockSpec((tm, D_MODEL), lambda j, te, na, tile0=tile0: (tile0 + j, 0)),
        )
        y = pl.pallas_call(
            functools.partial(_expert_kernel, tile0=tile0),
            grid_spec=grid_spec,
            out_shape=jax.ShapeDtypeStruct((n_tiles * tm, D_MODEL), BF16),
            input_output_aliases=aliases,
            compiler_params=_cparams("arbitrary"),
            name="moe_experts",
        )(*args)
    return y


def _dispatch_plan(expert_ids, tm, n_tiles):
    t = expert_ids.shape[0]
    e_flat = expert_ids.reshape(-1)
    onehot = (e_flat[:, None] == jnp.arange(N_EXPERTS, dtype=jnp.int32)[None, :]).astype(jnp.int32)
    csum = jnp.cumsum(onehot, axis=0)
    rank = jnp.sum(csum * onehot, axis=1) - 1
    counts = csum[-1]
    padded = ((counts + tm - 1) // tm) * tm
    ends = jnp.cumsum(padded)
    dest = jnp.sum((ends - padded)[None, :] * onehot, axis=1) + rank
    filler = jnp.arange(n_tiles * tm, dtype=jnp.int32) % t
    scatter_add = lambda base, idx, vals: base.at[idx].add(vals, mode="promise_in_bounds", unique_indices=True)
    src_tok = compute_on("tpu_sparsecore")(jax.jit(scatter_add))(
        filler, dest, jnp.arange(2 * t, dtype=jnp.int32) // 2 - dest % t)
    n_active = ends[-1] // tm
    tile_start = jnp.minimum(jnp.arange(n_tiles, dtype=jnp.int32), n_active - 1) * tm
    te = jnp.sum((ends[None, :] <= tile_start[:, None]).astype(jnp.int32), axis=1)
    te = jnp.minimum(te, N_EXPERTS - 1)
    return dest.reshape(t, 2), src_tok, te, n_active.reshape(1).astype(jnp.int32)


def _bf16_const(a):
    a16 = a.astype(BF16)
    assert np.all(a16.astype(np.float32) == a)
    return jnp.asarray(a16)


def _alibi_np(n):
    slopes = np.asarray([2.0 ** (-8.0 * (i + 1) / n) for i in range(n)], np.float32)
    _bf16_const(slopes)
    return slopes


def _selection_overlap_t(seq):
    n_cmp = (seq - L_CMP) // STRIDE_CMP + 1
    n_sel = seq // L_SEL
    c_start = STRIDE_CMP * np.arange(n_cmp)
    s_start = L_SEL * np.arange(n_sel)
    inter = np.clip(np.minimum(c_start[:, None] + L_CMP, s_start[None, :] + L_SEL)
                    - np.maximum(c_start[:, None], s_start[None, :]), 0, None) / L_CMP
    out = np.zeros((LANES, LANES), np.float32)
    out[:n_sel, :n_cmp] = inter.T
    return _bf16_const(out)


def _key_aug(seq, block, pad_rows=0, width=LANES, lanes=(AUG_POS_HI, AUG_POS_LO, AUG_ONE, AUG_PAD)):
    rows = pad_rows + seq
    pos = np.arange(rows)
    key = pos - pad_rows
    real = key >= 0
    out = np.zeros((rows, width), np.float32)
    if block:
        out[pos[real], key[real] // block] = -BIG
    out[:, lanes[0]] = pos // POS_SPLIT
    out[:, lanes[1]] = pos % POS_SPLIT
    out[:, lanes[2]] = 1.0
    out[~real, lanes[3]] = -BIG
    return _bf16_const(out)


def _block_average(seq):
    out = np.zeros((2 * SUBLANES, seq), np.float32)
    for n in range(seq // BLK_C):
        out[n, n * BLK_C:(n + 1) * BLK_C] = 1.0 / BLK_C
    return _bf16_const(out)


def kernel(x, w_in, nsa_cmp_pos, nsa_cmp_w, sink_b, w_br_a, w_br_b, w_br_c, w_out, ln1_g, ln1_b,
           w_group, b_group, w_router, b_router, w_gate, w_up, w_down, ln2_g, ln2_b):
    batch, seq, d = x.shape
    t = batch * seq
    assert d == D_MODEL and w_in.shape[2] == D_IN and seq % BLK_C == 0 and t % MM_TM == 0
    assert seq // L_SEL <= AUG_POS_HI and (seq - L_CMP) // STRIDE_CMP + 1 < LANES
    assert seq // BLK_C <= SUBLANES and (WIN_A + seq) // POS_SPLIT < 256
    n_tiles = (TOPK_EXPERT * t) // EXP_TM + N_EXPERTS
    assert n_tiles % EXP_CHUNKS == 0

    slopes_a = jnp.asarray(_alibi_np(H_A))
    slopes_c = jnp.asarray(_alibi_np(H_C))
    _alibi_np(H_B)
    inter_t = _selection_overlap_t(seq)
    kaug_slc = _key_aug(seq, L_SEL)
    kaug_win = _key_aug(seq, 0, pad_rows=WIN_A)
    kaug_swa = _key_aug(seq, 0, pad_rows=WIN_B, width=HD_B, lanes=(SW_POS_HI, SW_POS_LO, SW_ONE, SW_PAD))
    kaug_moba = _key_aug(seq, BLK_C)
    avg = _block_average(seq)
    take = lambda a, idx: a.at[idx].get(mode="promise_in_bounds")
    sc_take = compute_on("tpu_sparsecore")(jax.jit(take))

    w_in_t = jnp.swapaxes(w_in, 1, 2)
    xf = x.reshape(t, d)
    xb = xf.astype(BF16)
    for l in range(DEPTH):
        z = _matmul(xb, _prep_w_in(w_in_t, l), BF16)
        kvc = _cmp_kv(z, nsa_cmp_pos, nsa_cmp_w, l, batch, seq)
        o_cmp, nsel = _nsa_cmp(z, kvc, slopes_a, inter_t, batch, seq)
        o_slc = _nsa_slc(z, nsel, kaug_slc, slopes_a, batch, seq)
        o_win = _nsa_win(z, kaug_win, slopes_a, batch, seq)
        o_b = _swa(z, kaug_swa, sink_b, l, batch, seq)
        o_c = _moba(z, kaug_moba, avg, slopes_c, batch, seq)
        merged = _merge(o_cmp, o_slc, o_win, o_b, o_c, z,
                        w_br_a[l].astype(BF16), w_br_b[l].astype(BF16), w_br_c[l].astype(BF16))
        xf, xb = _proj_ln(merged, w_out[l].astype(BF16), xf, ln1_g[l][None, :], ln1_b[l][None, :])

        w_rt = jnp.concatenate([w_group[l], w_router[l],
                                jnp.zeros((d, LANES - N_GROUPS - N_EXPERTS), F32)], axis=1)
        b_rt = jnp.concatenate([b_group[l], b_router[l],
                                jnp.zeros((LANES - N_GROUPS - N_EXPERTS,), F32)])[None, :]
        ids, wts = _router(xf, w_rt, b_rt)
        dest, src_tok, tile_expert, n_active = _dispatch_plan(ids[:, :TOPK_EXPERT], EXP_TM, n_tiles)
        x_chunks = [sc_take(xb, c) for c in jnp.split(src_tok, EXP_CHUNKS)]
        y = _experts(x_chunks, tile_expert, n_active, w_gate, w_up, w_down, l)
        ya = take(y, dest[:, 0])
        yb = take(y, dest[:, 1])
        xf, xb = _moe_ln(xf, ya, yb, wts, ln2_g[l][None, :], ln2_b[l][None, :])
    return xf.reshape(batch, seq, d)
```

```python
import functools

import numpy as np
import jax
import jax.numpy as jnp
from jax import lax
from jax.experimental import pallas as pl
from jax.experimental.pallas import tpu as pltpu
from jax.experimental.compute_on import compute_on

F32 = jnp.float32
BF16 = jnp.bfloat16

D_MODEL = 2048
DEPTH = 2
H_A, KV_A, HD_A = 8, 2, 128
HPG_A = H_A // KV_A
L_CMP, STRIDE_CMP, L_SEL, N_SEL, WIN_A = 32, 16, 64, 8, 512
H_B, KV_B, HD_B, WIN_B = 8, 2, 64, 128
HPG_B = H_B // KV_B
H_C, HD_C, BLK_C, TOPK_C = 4, 128, 256, 3
N_GROUPS, EXP_PER_GROUP, D_EXPERT, TOPK_EXPERT = 4, 8, 256, 2
N_EXPERTS = N_GROUPS * EXP_PER_GROUP
LN_EPS = 1e-5
NEG_INF = -1e30
ALPHA = (2.0 * DEPTH) ** 0.25

LANES = 128
SUBLANES = 8
VMEM_LIMIT = 48 * 1024 * 1024
BIG = 2.0 ** 100

Q_A_W = H_A * HD_A
KV_A_W = 3 * 2 * KV_A * HD_A
GATE_A_W = 3 * H_A
Q_B_W = H_B * HD_B
KV_B_W = KV_B * HD_B
C_W = H_C * HD_C
MERGE_W = 3 * D_MODEL
D_IN = Q_A_W + KV_A_W + GATE_A_W + Q_B_W + 2 * KV_B_W + 3 * C_W + MERGE_W
OFF_MERGE = 0
OFF_QA = OFF_MERGE + MERGE_W
OFF_QB = OFF_QA + Q_A_W
OFF_QC = OFF_QB + Q_B_W
OFF_KC = OFF_QC + C_W
OFF_VC = OFF_KC + C_W
OFF_KVA = OFF_VC + C_W
OFF_KB = OFF_KVA + KV_A_W
OFF_VB = OFF_KB + KV_B_W
OFF_GATE_A = OFF_VB + KV_B_W
GATE_A_PAD = 256
Z_W = OFF_GATE_A + GATE_A_PAD
assert OFF_QA % (HPG_A * HD_A) == 0 and OFF_QB % Q_B_W == 0 and OFF_MERGE % D_MODEL == 0
assert OFF_QC % C_W == 0 and OFF_KC % C_W == 0 and OFF_VC % C_W == 0

AUG_POS_HI, AUG_POS_LO, AUG_ONE, AUG_PAD = 32, 33, 34, 35
POS_SPLIT = 256

RELAYOUT_COLS = 256
MM_TM, MM_TN = 2048, 512
ROW_TM = 256
ROUTE_TM = 512
EXP_TM = 256
EXP_CHUNKS = 3
ATT_TQ = 256
SWA_TQ = 128
COMBINE_SPLIT = 2


def _cparams(*sem):
    return pltpu.CompilerParams(dimension_semantics=sem, vmem_limit_bytes=VMEM_LIMIT)


def _dot(a, b):
    return jnp.dot(a, b, preferred_element_type=F32)


def _dot_nt(a, b):
    return lax.dot_general(a, b, (((1,), (1,)), ((), ())), preferred_element_type=F32)


def _iota(shape, dim):
    return lax.broadcasted_iota(jnp.int32, shape, dim)


def _split3(x):
    hi = x.astype(BF16)
    r1 = x - hi.astype(F32)
    mid = r1.astype(BF16)
    lo = (r1 - mid.astype(F32)).astype(BF16)
    return hi, mid, lo


def _pick_lane(x, lane_idx):
    lane = _iota(x.shape, 1)
    return jnp.sum(jnp.where(lane == lane_idx, x, 0.0), axis=-1, keepdims=True)


def _topk_rows(vals, k):
    row = _iota(vals.shape, 0).astype(F32)
    sel = jnp.zeros(vals.shape, F32)
    for _ in range(k):
        m = jnp.max(vals, axis=0, keepdims=True)
        idx = jnp.min(jnp.where(vals == m, row, float(LANES)), axis=0, keepdims=True)
        pick = row == idx
        sel = jnp.where(pick, 1.0, sel)
        vals = jnp.where(pick, -jnp.inf, vals)
    return sel


def _rows_to_lanes(x_t, tq):
    pad = jnp.zeros((LANES - x_t.shape[0], tq), F32)
    return jnp.concatenate([x_t, pad], axis=0).T


def _query_aug(base, slope, center, pad_flag=False):
    lane = _iota(base.shape, 1)
    aug = jnp.where(lane == AUG_POS_HI, slope * float(POS_SPLIT), base)
    aug = jnp.where(lane == AUG_POS_LO, slope, aug)
    aug = jnp.where(lane == AUG_ONE, -slope * center, aug)
    if pad_flag:
        aug = jnp.where(lane == AUG_PAD, 1.0, aug)
    return aug.astype(BF16)


def _softmax_update(s, v, carry):
    m, l, acc = carry
    m_new = jnp.maximum(m, jnp.max(s, axis=-1, keepdims=True))
    alpha = jnp.exp(m - m_new)
    p = jnp.exp(s - m_new)
    l = alpha * l + jnp.sum(p, axis=-1, keepdims=True)
    acc = alpha * acc + _dot(p.astype(BF16), v)
    return m_new, l, acc


def _softmax_init(rows, hd):
    return (jnp.full((rows, 1), NEG_INF, F32), jnp.zeros((rows, 1), F32), jnp.zeros((rows, hd), F32))


def _z_column_sources():
    s_kva = Q_A_W
    s_gate = s_kva + KV_A_W
    s_qb = s_gate + GATE_A_W
    s_kb = s_qb + Q_B_W
    s_qc = s_kb + 2 * KV_B_W
    s_merge = s_qc + 3 * C_W
    assert GATE_A_PAD == RELAYOUT_COLS
    segments = [(OFF_MERGE, s_merge, MERGE_W), (OFF_QA, 0, Q_A_W), (OFF_QB, s_qb, Q_B_W),
                (OFF_QC, s_qc, 3 * C_W), (OFF_KVA, s_kva, KV_A_W), (OFF_KB, s_kb, 2 * KV_B_W)]
    src = np.zeros((Z_W // RELAYOUT_COLS,), np.int32)
    valid = np.zeros((Z_W // RELAYOUT_COLS,), np.int32)
    for dst, start, width in segments:
        assert dst % RELAYOUT_COLS == 0 and width % RELAYOUT_COLS == 0
        for b in range(width // RELAYOUT_COLS):
            src[dst // RELAYOUT_COLS + b] = start + b * RELAYOUT_COLS
            valid[dst // RELAYOUT_COLS + b] = RELAYOUT_COLS
    src[OFF_GATE_A // RELAYOUT_COLS] = s_gate
    valid[OFF_GATE_A // RELAYOUT_COLS] = GATE_A_W
    assert np.all(src % SUBLANES == 0) and np.all(src + RELAYOUT_COLS <= D_IN)
    return jnp.asarray(src // SUBLANES), jnp.asarray(valid)


def _prep_w_in_kernel(src_ref, valid_ref, w_ref, o_ref):
    j = pl.program_id(0)
    w = w_ref[0]
    w = jnp.where(_iota(w.shape, 0) < valid_ref[j], w, 0.0)
    o_ref[...] = w.T.astype(BF16)


def _prep_w_in(w_in_t, layer):
    src, valid = _z_column_sources()
    grid_spec = pltpu.PrefetchScalarGridSpec(
        num_scalar_prefetch=2,
        grid=(Z_W // RELAYOUT_COLS,),
        in_specs=[pl.BlockSpec((pl.Element(1), pl.Element(RELAYOUT_COLS), pl.Element(D_MODEL)),
                               lambda j, src, valid: (layer, src[j] * SUBLANES, 0))],
        out_specs=pl.BlockSpec((D_MODEL, RELAYOUT_COLS), lambda j, src, valid: (0, j)),
    )
    return pl.pallas_call(
        _prep_w_in_kernel,
        grid_spec=grid_spec,
        out_shape=jax.ShapeDtypeStruct((D_MODEL, Z_W), BF16),
        compiler_params=_cparams("parallel"),
        name="w_in_relayout",
    )(src, valid, w_in_t)


def _mm_kernel(a_ref, b_ref, o_ref):
    o_ref[...] = _dot(a_ref[...], b_ref[...]).astype(o_ref.dtype)


def _matmul(a, b, out_dtype):
    m, k = a.shape
    n = b.shape[1]
    return pl.pallas_call(
        _mm_kernel,
        grid=(m // MM_TM, n // MM_TN),
        in_specs=[pl.BlockSpec((MM_TM, k), lambda i, j: (i, 0)),
                  pl.BlockSpec((k, MM_TN), lambda i, j: (0, j))],
        out_specs=pl.BlockSpec((MM_TM, MM_TN), lambda i, j: (i, j)),
        out_shape=jax.ShapeDtypeStruct((m, n), out_dtype),
        compiler_params=_cparams("parallel", "parallel"),
        name="in_proj",
    )(a, b)


def _cmp_kv_kernel(k_ref, pos_ref, w_ref, o_ref, kf_ref, *, seq):
    kf_ref[0:seq, :] = k_ref[...].astype(F32)
    kf_ref[seq:seq + LANES, :] = jnp.zeros((LANES, HD_A), F32)
    acc = jnp.zeros((LANES, HD_A), F32)
    for l in range(L_CMP):
        rows = kf_ref[pl.ds(l, LANES, stride=STRIDE_CMP), :] + pos_ref[0, 0, l:l + 1, :]
        acc = acc + _dot(rows.astype(BF16), w_ref[0, 0, l].astype(BF16))
    o_ref[0, 0] = acc.astype(o_ref.dtype)


def _cmp_kv(z, cmp_pos, cmp_w, layer, batch, seq):
    blk0 = OFF_KVA // LANES
    return pl.pallas_call(
        functools.partial(_cmp_kv_kernel, seq=seq),
        grid=(batch, 2 * KV_A),
        in_specs=[pl.BlockSpec((seq, LANES), lambda b, j: (b, blk0 + j)),
                  pl.BlockSpec((1, 1, L_CMP, HD_A), lambda b, j: (layer, j // KV_A, 0, 0)),
                  pl.BlockSpec((1, 1, L_CMP, HD_A, HD_A), lambda b, j: (layer, j // KV_A, 0, 0, 0))],
        out_specs=pl.BlockSpec((1, 1, LANES, HD_A), lambda b, j: (b, j, 0, 0)),
        out_shape=jax.ShapeDtypeStruct((batch, 2 * KV_A, LANES, HD_A), BF16),
        scratch_shapes=[pltpu.VMEM((seq + LANES, HD_A), F32)],
        compiler_params=_cparams("parallel", "parallel"),
        name="nsa_cmp_kv",
    )(z, cmp_pos, cmp_w)


def _nsa_cmp_kernel(q_ref, kvc_ref, gate_ref, inter_t_ref, o_ref, nsel_ref, *, tq):
    i = pl.program_id(1)
    shape = (tq, LANES)
    t = i * tq + _iota(shape, 0)
    lane = _iota(shape, 1)
    dist_i = t - (lane * STRIDE_CMP + (L_CMP - 1))
    ok = dist_i >= 0
    okf = ok.astype(F32)
    dist = dist_i.astype(F32)
    sig = jax.nn.sigmoid(gate_ref[...].astype(F32))
    scale = HD_A ** -0.5
    n_sel_rows = 32
    inter_t = inter_t_ref[...]
    shape_t = (n_sel_rows, tq)
    j = _iota(shape_t, 0)
    blk_t = (i * tq + _iota(shape_t, 1)) // L_SEL
    valid = j <= blk_t
    forced = (j == 0) | (j == blk_t) | (j == blk_t - 1)
    for g in range(KV_A):
        kc = kvc_ref[0, g]
        vc = kvc_ref[0, KV_A + g]
        psum = jnp.zeros(shape, F32)
        outs = []
        for h in range(HPG_A):
            head = g * HPG_A + h
            q = q_ref[:, head * HD_A:(head + 1) * HD_A]
            s = _dot_nt(q, kc) * scale - (2.0 ** (-8.0 * (head + 1) / H_A)) * dist
            s = jnp.where(ok, s, NEG_INF)
            e = jnp.exp(s - jnp.max(s, axis=-1, keepdims=True))
            p = e / jnp.sum(e, axis=-1, keepdims=True) * okf
            psum = psum + p
            o = _dot(p.astype(BF16), vc)
            outs.append(o * sig[:, head * 3:head * 3 + 1])
        o_ref[:, g * HPG_A * HD_A:(g + 1) * HPG_A * HD_A] = jnp.concatenate(outs, axis=1).astype(o_ref.dtype)

        imp_t = sum(_dot_nt(inter_t, part) for part in _split3(psum))[0:n_sel_rows]
        vals = jnp.where(forced, jnp.inf, jnp.where(valid, imp_t, -jnp.inf))
        sel = _topk_rows(vals, N_SEL)
        not_selected = jnp.where(valid & (sel > 0.5), 0.0, 1.0)
        nsel_ref[0, g] = _rows_to_lanes(not_selected, tq).astype(nsel_ref.dtype)


def _nsa_cmp(z, kvc, inter_t, batch, seq):
    tq = ATT_TQ
    nq = seq // tq
    gate_blk = OFF_GATE_A // LANES
    return pl.pallas_call(
        functools.partial(_nsa_cmp_kernel, tq=tq),
        grid=(batch, nq),
        in_specs=[pl.BlockSpec((tq, Q_A_W), lambda b, i: (b * nq + i, OFF_QA // Q_A_W)),
                  pl.BlockSpec((1, 2 * KV_A, LANES, HD_A), lambda b, i: (b, 0, 0, 0)),
                  pl.BlockSpec((tq, LANES), lambda b, i: (b * nq + i, gate_blk)),
                  pl.BlockSpec((LANES, LANES), lambda b, i: (0, 0))],
        out_specs=[pl.BlockSpec((tq, Q_A_W), lambda b, i: (b * nq + i, 0)),
                   pl.BlockSpec((1, KV_A, tq, LANES), lambda b, i: (b, 0, i, 0))],
        out_shape=[jax.ShapeDtypeStruct((batch * seq, Q_A_W), BF16),
                   jax.ShapeDtypeStruct((batch, KV_A, seq, LANES), BF16)],
        compiler_params=_cparams("parallel", "parallel"),
        name="nsa_cmp_attn",
    )(z, kvc, z, inter_t)


def _scaled_q(q_ref, h, hd, scale):
    return (q_ref[:, h * hd:(h + 1) * hd].astype(F32) * scale).astype(BF16)


def _gated_heads(o, sig, g, branch, tq):
    outs = [o[h * tq:(h + 1) * tq] * _pick_lane(sig, (g * HPG_A + h) * 3 + branch) for h in range(HPG_A)]
    return jnp.concatenate(outs, axis=1)


def _nsa_slc_kernel(slopes_ref, q_ref, k_ref, v_ref, kaug_ref, nsel_ref, gate_ref, o_ref, *, tq):
    g = pl.program_id(1)
    i = pl.program_id(2)
    rows = HPG_A * tq
    scale = HD_A ** -0.5
    nsel = nsel_ref[0, 0].astype(F32)
    center = (i * tq).astype(F32)
    qx = jnp.concatenate(
        [jnp.concatenate([_scaled_q(q_ref, h, HD_A, scale),
                          _query_aug(nsel, slopes_ref[g * HPG_A + h], center)], axis=1)
         for h in range(HPG_A)], axis=0)

    def scores(kt):
        k0 = pl.multiple_of(kt * tq, tq)
        kx = jnp.concatenate([k_ref[pl.ds(k0, tq), :], kaug_ref[pl.ds(k0, tq), :]], axis=1)
        return _dot_nt(qx, kx)

    def values(kt):
        return v_ref[pl.ds(pl.multiple_of(kt * tq, tq), tq), :]

    def body(kt, carry):
        s, m, l, acc = carry
        s_next = scores(kt + 1)
        m, l, acc = _softmax_update(s, values(kt), (m, l, acc))
        return s_next, m, l, acc

    s, m, l, acc = lax.fori_loop(0, i, body, (scores(0),) + _softmax_init(rows, HD_A))
    r = _iota((rows, tq), 0)
    causal = (r - (r // tq) * tq) >= _iota((rows, tq), 1)
    _, l, acc = _softmax_update(jnp.where(causal, s, -BIG), values(i), (m, l, acc))
    sig = jax.nn.sigmoid(gate_ref[...].astype(F32))
    o_ref[...] = _gated_heads(acc / l, sig, g, 1, tq).astype(o_ref.dtype)


def _nsa_slc(z, nsel, kaug, slopes, batch, seq):
    tq = ATT_TQ
    nq = seq // tq
    qw = HPG_A * HD_A
    kblk = OFF_KVA // LANES + 1 * 2 * KV_A
    vblk = kblk + KV_A
    gate_blk = OFF_GATE_A // LANES
    return pl.pallas_call(
        functools.partial(_nsa_slc_kernel, tq=tq),
        grid=(batch, KV_A, nq),
        in_specs=[pl.BlockSpec(memory_space=pltpu.SMEM),
                  pl.BlockSpec((tq, qw), lambda b, g, i: (b * nq + i, OFF_QA // qw + g)),
                  pl.BlockSpec((seq, LANES), lambda b, g, i: (b, kblk + g)),
                  pl.BlockSpec((seq, LANES), lambda b, g, i: (b, vblk + g)),
                  pl.BlockSpec((seq, LANES), lambda b, g, i: (0, 0)),
                  pl.BlockSpec((1, 1, tq, LANES), lambda b, g, i: (b, g, i, 0)),
                  pl.BlockSpec((tq, LANES), lambda b, g, i: (b * nq + i, gate_blk))],
        out_specs=pl.BlockSpec((tq, qw), lambda b, g, i: (b * nq + i, g)),
        out_shape=jax.ShapeDtypeStruct((batch * seq, Q_A_W), BF16),
        compiler_params=_cparams("parallel", "parallel", "parallel"),
        name="nsa_slc_attn",
    )(slopes, z, z, z, kaug, nsel, z)


def _band_mask(tq, span, window, n_heads):
    r = _iota((n_heads * tq, span), 0)
    r = r - (r // tq) * tq
    c = _iota((n_heads * tq, span), 1)
    return (c > r) & (c <= r + window)


def _nsa_win_kernel(slopes_ref, q_ref, k_ref, v_ref, kaug_ref, gate_ref, o_ref, kx_ref, vx_ref, *, tq, window, seq):
    g = pl.program_id(1)
    i = pl.program_id(2)

    @pl.when(i == 0)
    def _():
        kx_ref[0:window, 0:HD_A] = jnp.zeros((window, HD_A), BF16)
        kx_ref[window:window + seq, 0:HD_A] = k_ref[...]
        kx_ref[:, HD_A:2 * HD_A] = kaug_ref[...]
        vx_ref[0:window, :] = jnp.zeros((window, HD_A), BF16)
        vx_ref[window:window + seq, :] = v_ref[...]

    span = window + tq
    r0 = pl.multiple_of(i * tq, tq)
    center = (i * tq + window).astype(F32)
    base = jnp.zeros((tq, LANES), F32)
    qx = jnp.concatenate(
        [jnp.concatenate([_scaled_q(q_ref, h, HD_A, HD_A ** -0.5),
                          _query_aug(base, slopes_ref[g * HPG_A + h], center, pad_flag=True)], axis=1)
         for h in range(HPG_A)], axis=0)
    s = _dot_nt(qx, kx_ref[pl.ds(r0, span), :])
    s = jnp.where(_band_mask(tq, span, window, HPG_A), s, -BIG)
    p = jnp.exp(s - jnp.max(s, axis=-1, keepdims=True))
    o = _dot(p.astype(BF16), vx_ref[pl.ds(r0, span), :]) / jnp.sum(p, axis=-1, keepdims=True)
    sig = jax.nn.sigmoid(gate_ref[...].astype(F32))
    o_ref[...] = _gated_heads(o, sig, g, 2, tq).astype(o_ref.dtype)


def _nsa_win(z, kaug_pad, slopes, batch, seq):
    tq = ATT_TQ
    nq = seq // tq
    qw = HPG_A * HD_A
    kblk = OFF_KVA // LANES + 2 * 2 * KV_A
    vblk = kblk + KV_A
    gate_blk = OFF_GATE_A // LANES
    return pl.pallas_call(
        functools.partial(_nsa_win_kernel, tq=tq, window=WIN_A, seq=seq),
        grid=(batch, KV_A, nq),
        in_specs=[pl.BlockSpec(memory_space=pltpu.SMEM),
                  pl.BlockSpec((tq, qw), lambda b, g, i: (b * nq + i, OFF_QA // qw + g)),
                  pl.BlockSpec((seq, LANES), lambda b, g, i: (b, kblk + g)),
                  pl.BlockSpec((seq, LANES), lambda b, g, i: (b, vblk + g)),
                  pl.BlockSpec((WIN_A + seq, LANES), lambda b, g, i: (0, 0)),
                  pl.BlockSpec((tq, LANES), lambda b, g, i: (b * nq + i, gate_blk))],
        out_specs=pl.BlockSpec((tq, qw), lambda b, g, i: (b * nq + i, g)),
        out_shape=jax.ShapeDtypeStruct((batch * seq, Q_A_W), BF16),
        scratch_shapes=[pltpu.VMEM((WIN_A + seq, 2 * HD_A), BF16), pltpu.VMEM((WIN_A + seq, HD_A), BF16)],
        compiler_params=_cparams("parallel", "parallel", "arbitrary"),
        name="nsa_win_attn",
    )(slopes, z, z, z, kaug_pad, z)


SW_POS_HI, SW_POS_LO, SW_ONE, SW_PAD = 0, 1, 2, 3


def _swa_kernel(sink_ref, q_ref, k_ref, v_ref, kaug_ref, o_ref, kx_ref, vx_ref, *, tq, window, seq):
    i = pl.program_id(1)

    @pl.when(i == 0)
    def _():
        for g in range(KV_B):
            kx_ref[g, 0:window, 0:HD_B] = jnp.zeros((window, HD_B), BF16)
            kx_ref[g, window:window + seq, 0:HD_B] = k_ref[:, g * HD_B:(g + 1) * HD_B]
            kx_ref[g, :, HD_B:2 * HD_B] = kaug_ref[...]
            vx_ref[g, 0:window, :] = jnp.zeros((window, HD_B), BF16)
            vx_ref[g, window:window + seq, :] = v_ref[:, g * HD_B:(g + 1) * HD_B]

    span = window + tq
    rows = HPG_B * tq
    r0 = pl.multiple_of(i * tq, tq)
    center = (i * tq + window).astype(F32)
    band = _band_mask(tq, span, window, HPG_B)
    lane = _iota((tq, HD_B), 1)
    hh = _iota((rows, 1), 0) // tq
    row_in_tile = (_iota((rows, 1), 0) - hh * tq).astype(F32)

    def head_column(values):
        col = jnp.full((rows, 1), values[HPG_B - 1], F32)
        for h in range(HPG_B - 2, -1, -1):
            col = jnp.where(hh == h, values[h], col)
        return col

    outs = []
    for g in range(KV_B):
        slopes = [2.0 ** (-8.0 * (g * HPG_B + h + 1) / H_B) for h in range(HPG_B)]
        parts = []
        for h in range(HPG_B):
            aug = jnp.where(lane == SW_POS_HI, slopes[h] * POS_SPLIT, jnp.where(lane == SW_POS_LO, slopes[h], 0.0))
            aug = jnp.where(lane == SW_ONE, -slopes[h] * center, jnp.where(lane == SW_PAD, 1.0, aug))
            parts.append(jnp.concatenate([_scaled_q(q_ref, g * HPG_B + h, HD_B, HD_B ** -0.5),
                                          aug.astype(BF16)], axis=1))
        qx = jnp.concatenate(parts, axis=0)
        s = _dot_nt(qx, kx_ref[g, pl.ds(r0, span), :])
        s = jnp.where(band, s, -BIG)
        sink_shifted = (head_column([sink_ref[g * HPG_B + h] for h in range(HPG_B)])
                        + head_column(slopes) * row_in_tile)
        m = jnp.maximum(jnp.max(s, axis=-1, keepdims=True), sink_shifted)
        p = jnp.exp(s - m)
        denom = jnp.sum(p, axis=-1, keepdims=True) + jnp.exp(sink_shifted - m)
        o = _dot(p.astype(BF16), vx_ref[g, pl.ds(r0, span), :]) / denom
        outs += [o[h * tq:(h + 1) * tq] for h in range(HPG_B)]
    o_ref[...] = jnp.concatenate(outs, axis=1).astype(o_ref.dtype)


def _swa(z, kaug_sw, sink, layer, batch, seq):
    tq = SWA_TQ
    nq = seq // tq
    return pl.pallas_call(
        functools.partial(_swa_kernel, tq=tq, window=WIN_B, seq=seq),
        grid=(batch, nq),
        in_specs=[pl.BlockSpec(memory_space=pltpu.SMEM),
                  pl.BlockSpec((tq, Q_B_W), lambda b, i: (b * nq + i, OFF_QB // Q_B_W)),
                  pl.BlockSpec((seq, KV_B_W), lambda b, i: (b, OFF_KB // KV_B_W)),
                  pl.BlockSpec((seq, KV_B_W), lambda b, i: (b, OFF_VB // KV_B_W)),
                  pl.BlockSpec((WIN_B + seq, HD_B), lambda b, i: (0, 0))],
        out_specs=pl.BlockSpec((tq, Q_B_W), lambda b, i: (b * nq + i, 0)),
        out_shape=jax.ShapeDtypeStruct((batch * seq, Q_B_W), BF16),
        scratch_shapes=[pltpu.VMEM((KV_B, WIN_B + seq, 2 * HD_B), BF16),
                        pltpu.VMEM((KV_B, WIN_B + seq, HD_B), BF16)],
        compiler_params=_cparams("parallel", "arbitrary"),
        name="swa_attn",
    )(sink[layer], z, z, z, kaug_sw)


def _moba_kernel(slopes_ref, q_ref, k_ref, v_ref, kaug_ref, avg_ref, o_ref, km_ref, *, seq):
    i = pl.program_id(1)
    tq = BLK_C
    n_blk = seq // BLK_C
    scale = HD_C ** -0.5

    @pl.when(i == 0)
    def _():
        for h in range(H_C):
            k_mean = _dot(avg_ref[...], k_ref[:, h * HD_C:(h + 1) * HD_C])[0:SUBLANES]
            terms = [t.astype(F32) for t in _split3(k_mean)] + [jnp.zeros((SUBLANES, HD_C), F32)]
            km_ref[h] = jnp.concatenate(terms, axis=0).astype(BF16)

    center = (i * tq).astype(F32)
    blk = _iota((SUBLANES, tq), 0)
    past = blk < i
    qx = []
    for h in range(H_C):
        q = q_ref[:, h * HD_C:(h + 1) * HD_C]
        sc = _dot_nt(km_ref[h], q)
        score_t = sc[0:SUBLANES] + sc[SUBLANES:2 * SUBLANES] + sc[2 * SUBLANES:3 * SUBLANES]
        sel = _topk_rows(jnp.where(past, score_t, -jnp.inf), TOPK_C)
        not_selected = jnp.where(past & (sel < 0.5), 1.0, 0.0)
        qx.append(jnp.concatenate([_scaled_q(q_ref, h, HD_C, scale),
                                   _query_aug(_rows_to_lanes(not_selected, tq), slopes_ref[h], center)], axis=1))
    assert n_blk <= SUBLANES

    def scores(h, kt):
        k0 = pl.multiple_of(kt * tq, tq)
        kx = jnp.concatenate([k_ref[pl.ds(k0, tq), h * HD_C:(h + 1) * HD_C], kaug_ref[pl.ds(k0, tq), :]], axis=1)
        return _dot_nt(qx[h], kx)

    def values(h, kt):
        return v_ref[pl.ds(pl.multiple_of(kt * tq, tq), tq), h * HD_C:(h + 1) * HD_C]

    def body(kt, carry):
        out = []
        for h in range(H_C):
            s, m, l, acc = carry[h]
            s_next = scores(h, kt + 1)
            out.append((s_next,) + _softmax_update(s, values(h, kt), (m, l, acc)))
        return tuple(out)

    init = tuple((scores(h, 0),) + _softmax_init(tq, HD_C) for h in range(H_C))
    carry = lax.fori_loop(0, i, body, init)
    causal = _iota((tq, tq), 0) >= _iota((tq, tq), 1)
    outs = []
    for h in range(H_C):
        s, m, l, acc = carry[h]
        _, l, acc = _softmax_update(jnp.where(causal, s, -BIG), values(h, i), (m, l, acc))
        outs.append(acc / l)
    o_ref[...] = jnp.concatenate(outs, axis=1).astype(o_ref.dtype)


def _moba(z, kaug, avg, slopes, batch, seq):
    nq = seq // BLK_C
    return pl.pallas_call(
        functools.partial(_moba_kernel, seq=seq),
        grid=(batch, nq),
        in_specs=[pl.BlockSpec(memory_space=pltpu.SMEM),
                  pl.BlockSpec((BLK_C, C_W), lambda b, i: (b * nq + i, OFF_QC // C_W)),
                  pl.BlockSpec((seq, C_W), lambda b, i: (b, OFF_KC // C_W)),
                  pl.BlockSpec((seq, C_W), lambda b, i: (b, OFF_VC // C_W)),
                  pl.BlockSpec((seq, LANES), lambda b, i: (0, 0)),
                  pl.BlockSpec((2 * SUBLANES, seq), lambda b, i: (0, 0))],
        out_specs=pl.BlockSpec((BLK_C, C_W), lambda b, i: (b * nq + i, 0)),
        out_shape=jax.ShapeDtypeStruct((batch * seq, C_W), BF16),
        scratch_shapes=[pltpu.VMEM((H_C, 4 * SUBLANES, HD_C), BF16)],
        compiler_params=_cparams("parallel", "arbitrary"),
        name="moba_attn",
    )(slopes, z, z, z, kaug, avg)


def _merge_kernel(oc_ref, os_ref, ow_ref, ob_ref, om_ref, g0_ref, g1_ref, g2_ref, wa_ref, wb_ref, wc_ref, o_ref):
    o_a = (oc_ref[...].astype(F32) + os_ref[...].astype(F32) + ow_ref[...].astype(F32)).astype(BF16)
    merged = jax.nn.sigmoid(g0_ref[...].astype(F32)) * _dot(o_a, wa_ref[...])
    merged = merged + jax.nn.sigmoid(g1_ref[...].astype(F32)) * _dot(ob_ref[...], wb_ref[...])
    merged = merged + jax.nn.sigmoid(g2_ref[...].astype(F32)) * _dot(om_ref[...], wc_ref[...])
    o_ref[...] = merged.astype(o_ref.dtype)


def _merge(o_cmp, o_slc, o_win, o_b, o_c, z, wa, wb, wc):
    tm = ROW_TM
    t = z.shape[0]
    gblk = OFF_MERGE // D_MODEL
    row = lambda w: pl.BlockSpec((tm, w), lambda i: (i, 0))
    full = lambda a: pl.BlockSpec(a.shape, lambda i: (0, 0))
    gate = lambda r: pl.BlockSpec((tm, D_MODEL), lambda i: (i, gblk + r))
    return pl.pallas_call(
        _merge_kernel,
        grid=(t // tm,),
        in_specs=[row(Q_A_W), row(Q_A_W), row(Q_A_W), row(Q_B_W), row(C_W),
                  gate(0), gate(1), gate(2), full(wa), full(wb), full(wc)],
        out_specs=row(D_MODEL),
        out_shape=jax.ShapeDtypeStruct((t, D_MODEL), BF16),
        compiler_params=_cparams("parallel"),
        name="mixer_merge",
    )(o_cmp, o_slc, o_win, o_b, o_c, z, z, z, wa, wb, wc)


def _layer_norm(h, g_ref, b_ref):
    mu = jnp.mean(h, axis=-1, keepdims=True)
    xc = h - mu
    var = jnp.mean(xc * xc, axis=-1, keepdims=True)
    return xc * lax.rsqrt(var + LN_EPS) * g_ref[...] + b_ref[...]


def _proj_ln_kernel(m_ref, w_ref, x_ref, g_ref, b_ref, xo_ref, xb_ref):
    y = _dot(m_ref[...], w_ref[...])
    out = _layer_norm(ALPHA * x_ref[...] + y, g_ref, b_ref)
    xo_ref[...] = out
    xb_ref[...] = out.astype(BF16)


def _proj_ln(merged, w_out, x, g, b):
    tm = ROW_TM
    t = x.shape[0]
    row = pl.BlockSpec((tm, D_MODEL), lambda i: (i, 0))
    vec = pl.BlockSpec((1, D_MODEL), lambda i: (0, 0))
    return pl.pallas_call(
        _proj_ln_kernel,
        grid=(t // tm,),
        in_specs=[row, pl.BlockSpec((D_MODEL, D_MODEL), lambda i: (0, 0)), row, vec, vec],
        out_specs=[row, row],
        out_shape=[jax.ShapeDtypeStruct((t, D_MODEL), F32), jax.ShapeDtypeStruct((t, D_MODEL), BF16)],
        compiler_params=_cparams("parallel"),
        name="out_proj_ln",
    )(merged, w_out, x, g, b)


def _moe_ln_kernel(x_ref, ya_ref, yb_ref, w_ref, g_ref, b_ref, *rest):
    xo_ref, xb_ref = rest[-2:]
    w = w_ref[...]
    y = w[:, 0:1] * ya_ref[...].astype(F32) + w[:, 1:2] * yb_ref[...].astype(F32)
    out = _layer_norm(ALPHA * x_ref[...] + y, g_ref, b_ref)
    xo_ref[...] = out
    xb_ref[...] = out.astype(BF16)


def _moe_ln(x, y_groups, wts, g, b):
    tm = ROW_TM
    t = x.shape[0]
    blocks = t // tm // len(y_groups)
    vec = pl.BlockSpec((1, D_MODEL), lambda i: (0, 0))
    group_row = pl.BlockSpec((tm, D_MODEL), lambda i: (i, 0))
    outs = None
    for n, (ya, yb) in enumerate(y_groups):
        off = n * blocks
        row = pl.BlockSpec((tm, D_MODEL), lambda i, off=off: (off + i, 0))
        in_specs = [row, group_row, group_row, pl.BlockSpec((tm, LANES), lambda i, off=off: (off + i, 0)), vec, vec]
        args = [x, ya, yb, wts, g, b]
        aliases = {}
        if outs is not None:
            in_specs += [pl.BlockSpec(memory_space=pl.ANY)] * 2
            aliases = {len(args): 0, len(args) + 1: 1}
            args += list(outs)
        outs = pl.pallas_call(
            _moe_ln_kernel,
            grid=(blocks,),
            in_specs=in_specs,
            out_specs=[row, row],
            out_shape=[jax.ShapeDtypeStruct((t, D_MODEL), F32), jax.ShapeDtypeStruct((t, D_MODEL), BF16)],
            input_output_aliases=aliases,
            compiler_params=_cparams("parallel"),
            name="moe_combine_ln",
        )(*args)
    return outs


def _router_kernel(x_ref, w_ref, b_ref, id_ref, wt_ref):
    x = x_ref[...]
    x_hi = x.astype(BF16)
    x_lo = (x - x_hi.astype(F32)).astype(BF16)
    w = w_ref[...]
    w_hi = w.astype(BF16)
    w_lo = (w - w_hi.astype(F32)).astype(BF16)
    logits = _dot(x_hi, w_hi) + _dot(x_lo, w_hi) + _dot(x_hi, w_lo) + b_ref[...]
    lane = _iota(logits.shape, 1).astype(F32)
    first = lambda hit: jnp.min(jnp.where(hit, lane, float(LANES)), axis=-1, keepdims=True)
    gl = jnp.where(lane < N_GROUPS, logits, -jnp.inf)
    gm = jnp.max(gl, axis=-1, keepdims=True)
    g_w = 1.0 / jnp.sum(jnp.exp(gl - gm), axis=-1, keepdims=True)
    lo = N_GROUPS + first(gl == gm) * EXP_PER_GROUP
    el = jnp.where((lane >= lo) & (lane < lo + EXP_PER_GROUP), logits, -jnp.inf)
    m1 = jnp.max(el, axis=-1, keepdims=True)
    i1 = first(el == m1)
    el2 = jnp.where(lane == i1, -jnp.inf, el)
    m2 = jnp.max(el2, axis=-1, keepdims=True)
    i2 = first(el2 == m2)
    e2 = jnp.exp(m2 - m1)
    w1 = g_w / (1.0 + e2)
    w2 = g_w * e2 / (1.0 + e2)
    ids = jnp.where(lane == 0.0, i1 - N_GROUPS, jnp.where(lane == 1.0, i2 - N_GROUPS, 0.0))
    id_ref[...] = ids.astype(jnp.int32)
    wt_ref[...] = jnp.where(lane == 0.0, w1, jnp.where(lane == 1.0, w2, 0.0))


def _router(x, w_rt, b_rt):
    tm = ROUTE_TM
    t = x.shape[0]
    return pl.pallas_call(
        _router_kernel,
        grid=(t // tm,),
        in_specs=[pl.BlockSpec((tm, D_MODEL), lambda i: (i, 0)),
                  pl.BlockSpec((D_MODEL, LANES), lambda i: (0, 0)),
                  pl.BlockSpec((1, LANES), lambda i: (0, 0))],
        out_specs=[pl.BlockSpec((tm, LANES), lambda i: (i, 0)), pl.BlockSpec((tm, LANES), lambda i: (i, 0))],
        out_shape=[jax.ShapeDtypeStruct((t, LANES), jnp.int32), jax.ShapeDtypeStruct((t, LANES), F32)],
        compiler_params=_cparams("parallel"),
        name="moe_router",
    )(x, w_rt, b_rt)


def _expert_kernel(te_ref, na_ref, x_ref, wg_ref, wu_ref, wd_ref, *rest, tile0):
    y_ref = rest[-1]
    j = tile0 + pl.program_id(0)

    @pl.when(j < na_ref[0])
    def _():
        x = x_ref[...]
        hg = _dot(x, wg_ref[0, 0].astype(BF16))
        hu = _dot(x, wu_ref[0, 0].astype(BF16))
        a = hg * jax.nn.sigmoid(hg) * hu
        y_ref[...] = _dot(a.astype(BF16), wd_ref[0, 0].astype(BF16)).astype(y_ref.dtype)

    @pl.when(j >= na_ref[0])
    def _():
        y_ref[...] = jnp.zeros(y_ref.shape, y_ref.dtype)


def _experts(x_chunks, tile_expert, n_active, wg, wu, wd, layer):
    tm = EXP_TM
    tiles_per_chunk = x_chunks[0].shape[0] // tm
    n_tiles = len(x_chunks) * tiles_per_chunk
    y = None
    for c, x_c in enumerate(x_chunks):
        tile0 = c * tiles_per_chunk
        x_map = lambda j, te, na, tile0=tile0: (
            jnp.clip(jnp.minimum(tile0 + j, na[0] - 1) - tile0, 0, tiles_per_chunk - 1), 0)
        w_map = lambda j, te, na, tile0=tile0: (layer, te[tile0 + j], 0, 0)
        in_specs = [pl.BlockSpec((tm, D_MODEL), x_map),
                    pl.BlockSpec((1, 1, D_MODEL, D_EXPERT), w_map),
                    pl.BlockSpec((1, 1, D_MODEL, D_EXPERT), w_map),
                    pl.BlockSpec((1, 1, D_EXPERT, D_MODEL), w_map)]
        args = [tile_expert, n_active, x_c, wg, wu, wd]
        aliases = {}
        if y is not None:
            in_specs.append(pl.BlockSpec(memory_space=pl.ANY))
            aliases = {len(args): 0}
            args.append(y)
        grid_spec = pltpu.PrefetchScalarGridSpec(
            num_scalar_prefetch=2,
            grid=(tiles_per_chunk,),
            in_specs=in_specs,
            out_specs=pl.BlockSpec((tm, D_MODEL), lambda j, te, na, tile0=tile0: (tile0 + j, 0)),
        )
        y = pl.pallas_call(
            functools.partial(_expert_kernel, tile0=tile0),
            grid_spec=grid_spec,
            out_shape=jax.ShapeDtypeStruct((n_tiles * tm, D_MODEL), BF16),
            input_output_aliases=aliases,
            compiler_params=_cparams("arbitrary"),
            name="moe_experts",
        )(*args)
    return y


def _dispatch_plan(expert_ids, tm, n_tiles):
    t = expert_ids.shape[0]
    e_flat = expert_ids.reshape(-1)
    onehot = (e_flat[:, None] == jnp.arange(N_EXPERTS, dtype=jnp.int32)[None, :]).astype(jnp.int32)
    csum = jnp.cumsum(onehot, axis=0)
    rank = jnp.sum(csum * onehot, axis=1) - 1
    counts = csum[-1]
    padded = ((counts + tm - 1) // tm) * tm
    ends = jnp.cumsum(padded)
    dest = jnp.sum((ends - padded)[None, :] * onehot, axis=1) + rank
    filler = jnp.arange(n_tiles * tm, dtype=jnp.int32) % t
    scatter_add = lambda base, idx, vals: base.at[idx].add(vals, mode="promise_in_bounds", unique_indices=True)
    src_tok = compute_on("tpu_sparsecore")(jax.jit(scatter_add))(
        filler, dest, jnp.arange(2 * t, dtype=jnp.int32) // 2 - dest % t)
    n_active = ends[-1] // tm
    tile_start = jnp.minimum(jnp.arange(n_tiles, dtype=jnp.int32), n_active - 1) * tm
    te = jnp.sum((ends[None, :] <= tile_start[:, None]).astype(jnp.int32), axis=1)
    te = jnp.minimum(te, N_EXPERTS - 1)
    return dest.reshape(t, 2), src_tok, te, n_active.reshape(1).astype(jnp.int32)


def _bf16_const(a):
    a16 = a.astype(BF16)
    assert np.all(a16.astype(np.float32) == a)
    return jnp.asarray(a16)


def _alibi_np(n):
    slopes = np.asarray([2.0 ** (-8.0 * (i + 1) / n) for i in range(n)], np.float32)
    _bf16_const(slopes)
    return slopes


def _selection_overlap_t(seq):
    n_cmp = (seq - L_CMP) // STRIDE_CMP + 1
    n_sel = seq // L_SEL
    c_start = STRIDE_CMP * np.arange(n_cmp)
    s_start = L_SEL * np.arange(n_sel)
    inter = np.clip(np.minimum(c_start[:, None] + L_CMP, s_start[None, :] + L_SEL)
                    - np.maximum(c_start[:, None], s_start[None, :]), 0, None) / L_CMP
    out = np.zeros((LANES, LANES), np.float32)
    out[:n_sel, :n_cmp] = inter.T
    return _bf16_const(out)


def _key_aug(seq, block, pad_rows=0, width=LANES, lanes=(AUG_POS_HI, AUG_POS_LO, AUG_ONE, AUG_PAD)):
    rows = pad_rows + seq
    pos = np.arange(rows)
    key = pos - pad_rows
    real = key >= 0
    out = np.zeros((rows, width), np.float32)
    if block:
        out[pos[real], key[real] // block] = -BIG
    out[:, lanes[0]] = pos // POS_SPLIT
    out[:, lanes[1]] = pos % POS_SPLIT
    out[:, lanes[2]] = 1.0
    out[~real, lanes[3]] = -BIG
    return _bf16_const(out)


def _block_average(seq):
    out = np.zeros((2 * SUBLANES, seq), np.float32)
    for n in range(seq // BLK_C):
        out[n, n * BLK_C:(n + 1) * BLK_C] = 1.0 / BLK_C
    return _bf16_const(out)


def kernel(x, w_in, nsa_cmp_pos, nsa_cmp_w, sink_b, w_br_a, w_br_b, w_br_c, w_out, ln1_g, ln1_b,
           w_group, b_group, w_router, b_router, w_gate, w_up, w_down, ln2_g, ln2_b):
    batch, seq, d = x.shape
    t = batch * seq
    assert d == D_MODEL and w_in.shape[2] == D_IN and seq % BLK_C == 0 and t % MM_TM == 0
    assert seq // L_SEL <= AUG_POS_HI and (seq - L_CMP) // STRIDE_CMP + 1 < LANES
    assert seq // BLK_C <= SUBLANES and (WIN_A + seq) // POS_SPLIT < 256
    n_tiles = (TOPK_EXPERT * t) // EXP_TM + N_EXPERTS
    assert n_tiles % EXP_CHUNKS == 0 and (t // ROW_TM) % COMBINE_SPLIT == 0 and OFF_QA % Q_A_W == 0

    slopes_a = jnp.asarray(_alibi_np(H_A))
    slopes_c = jnp.asarray(_alibi_np(H_C))
    _alibi_np(H_B)
    inter_t = _selection_overlap_t(seq)
    kaug_slc = _key_aug(seq, L_SEL)
    kaug_win = _key_aug(seq, 0, pad_rows=WIN_A)
    kaug_swa = _key_aug(seq, 0, pad_rows=WIN_B, width=HD_B, lanes=(SW_POS_HI, SW_POS_LO, SW_ONE, SW_PAD))
    kaug_moba = _key_aug(seq, BLK_C)
    avg = _block_average(seq)
    take = lambda a, idx: a.at[idx].get(mode="promise_in_bounds")
    sc_take = compute_on("tpu_sparsecore")(jax.jit(take))

    w_in_t = jnp.swapaxes(w_in, 1, 2)
    xf = x.reshape(t, d)
    xb = xf.astype(BF16)
    for l in range(DEPTH):
        z = _matmul(xb, _prep_w_in(w_in_t, l), BF16)
        kvc = _cmp_kv(z, nsa_cmp_pos, nsa_cmp_w, l, batch, seq)
        o_cmp, nsel = _nsa_cmp(z, kvc, inter_t, batch, seq)
        o_slc = _nsa_slc(z, nsel, kaug_slc, slopes_a, batch, seq)
        o_win = _nsa_win(z, kaug_win, slopes_a, batch, seq)
        o_b = _swa(z, kaug_swa, sink_b, l, batch, seq)
        o_c = _moba(z, kaug_moba, avg, slopes_c, batch, seq)
        merged = _merge(o_cmp, o_slc, o_win, o_b, o_c, z,
                        w_br_a[l].astype(BF16), w_br_b[l].astype(BF16), w_br_c[l].astype(BF16))
        xf, xb = _proj_ln(merged, w_out[l].astype(BF16), xf, ln1_g[l][None, :], ln1_b[l][None, :])

        w_rt = jnp.concatenate([w_group[l], w_router[l],
                                jnp.zeros((d, LANES - N_GROUPS - N_EXPERTS), F32)], axis=1)
        b_rt = jnp.concatenate([b_group[l], b_router[l],
                                jnp.zeros((LANES - N_GROUPS - N_EXPERTS,), F32)])[None, :]
        ids, wts = _router(xf, w_rt, b_rt)
        dest, src_tok, tile_expert, n_active = _dispatch_plan(ids[:, :TOPK_EXPERT], EXP_TM, n_tiles)
        x_chunks = [sc_take(xb, c) for c in jnp.split(src_tok, EXP_CHUNKS)]
        y = _experts(x_chunks, tile_expert, n_active, w_gate, w_up, w_down, l)
        y_groups = [(take(y, d[:, 0]), take(y, d[:, 1])) for d in jnp.split(dest, COMBINE_SPLIT)]
        xf, xb = _moe_ln(xf, y_groups, wts, ln2_g[l][None, :], ln2_b[l][None, :])
    return xf.reshape(batch, seq, d)
```

```python
import functools

import numpy as np
import jax
import jax.numpy as jnp
from jax import lax
from jax.experimental import pallas as pl
from jax.experimental.pallas import tpu as pltpu
from jax.experimental.compute_on import compute_on

F32 = jnp.float32
BF16 = jnp.bfloat16

D_MODEL = 2048
DEPTH = 2
H_A, KV_A, HD_A = 8, 2, 128
HPG_A = H_A // KV_A
L_CMP, STRIDE_CMP, L_SEL, N_SEL, WIN_A = 32, 16, 64, 8, 512
H_B, KV_B, HD_B, WIN_B = 8, 2, 64, 128
HPG_B = H_B // KV_B
H_C, HD_C, BLK_C, TOPK_C = 4, 128, 256, 3
N_GROUPS, EXP_PER_GROUP, D_EXPERT, TOPK_EXPERT = 4, 8, 256, 2
N_EXPERTS = N_GROUPS * EXP_PER_GROUP
LN_EPS = 1e-5
NEG_INF = -1e30
ALPHA = (2.0 * DEPTH) ** 0.25

LANES = 128
SUBLANES = 8
VMEM_LIMIT = 48 * 1024 * 1024
BIG = 2.0 ** 100

Q_A_W = H_A * HD_A
KV_A_W = 3 * 2 * KV_A * HD_A
GATE_A_W = 3 * H_A
Q_B_W = H_B * HD_B
KV_B_W = KV_B * HD_B
C_W = H_C * HD_C
MERGE_W = 3 * D_MODEL
D_IN = Q_A_W + KV_A_W + GATE_A_W + Q_B_W + 2 * KV_B_W + 3 * C_W + MERGE_W
OFF_MERGE = 0
OFF_QA = OFF_MERGE + MERGE_W
OFF_QB = OFF_QA + Q_A_W
OFF_QC = OFF_QB + Q_B_W
OFF_KC = OFF_QC + C_W
OFF_VC = OFF_KC + C_W
OFF_KVA = OFF_VC + C_W
OFF_KB = OFF_KVA + KV_A_W
OFF_VB = OFF_KB + KV_B_W
OFF_GATE_A = OFF_VB + KV_B_W
GATE_A_PAD = 256
Z_W = OFF_GATE_A + GATE_A_PAD
assert OFF_QA % (HPG_A * HD_A) == 0 and OFF_QB % Q_B_W == 0 and OFF_MERGE % D_MODEL == 0
assert OFF_QC % C_W == 0 and OFF_KC % C_W == 0 and OFF_VC % C_W == 0

AUG_POS_HI, AUG_POS_LO, AUG_ONE, AUG_PAD = 32, 33, 34, 35
POS_SPLIT = 256

RELAYOUT_COLS = 256
MM_TM, MM_TN = 2048, 512
ROW_TM = 256
ROUTE_TM = 512
EXP_TM = 256
EXP_CHUNKS = 3
ATT_TQ = 256
SWA_TQ = 128
COMBINE_SPLIT = 2


def _cparams(*sem):
    return pltpu.CompilerParams(dimension_semantics=sem, vmem_limit_bytes=VMEM_LIMIT)


def _dot(a, b):
    return jnp.dot(a, b, preferred_element_type=F32)


def _dot_nt(a, b):
    return lax.dot_general(a, b, (((1,), (1,)), ((), ())), preferred_element_type=F32)


def _iota(shape, dim):
    return lax.broadcasted_iota(jnp.int32, shape, dim)


def _split3(x):
    hi = x.astype(BF16)
    r1 = x - hi.astype(F32)
    mid = r1.astype(BF16)
    lo = (r1 - mid.astype(F32)).astype(BF16)
    return hi, mid, lo


def _pick_lane(x, lane_idx):
    lane = _iota(x.shape, 1)
    return jnp.sum(jnp.where(lane == lane_idx, x, 0.0), axis=-1, keepdims=True)


def _topk_rows(vals, k):
    row = _iota(vals.shape, 0).astype(F32)
    sel = jnp.zeros(vals.shape, F32)
    for _ in range(k):
        m = jnp.max(vals, axis=0, keepdims=True)
        idx = jnp.min(jnp.where(vals == m, row, float(LANES)), axis=0, keepdims=True)
        pick = row == idx
        sel = jnp.where(pick, 1.0, sel)
        vals = jnp.where(pick, -jnp.inf, vals)
    return sel


def _rows_to_lanes(x_t, tq):
    pad = jnp.zeros((LANES - x_t.shape[0], tq), F32)
    return jnp.concatenate([x_t, pad], axis=0).T


def _query_aug(base, slope, center, pad_flag=False):
    lane = _iota(base.shape, 1)
    aug = jnp.where(lane == AUG_POS_HI, slope * float(POS_SPLIT), base)
    aug = jnp.where(lane == AUG_POS_LO, slope, aug)
    aug = jnp.where(lane == AUG_ONE, -slope * center, aug)
    if pad_flag:
        aug = jnp.where(lane == AUG_PAD, 1.0, aug)
    return aug.astype(BF16)


def _softmax_update(s, v, carry):
    m, l, acc = carry
    m_new = jnp.maximum(m, jnp.max(s, axis=-1, keepdims=True))
    alpha = jnp.exp(m - m_new)
    p = jnp.exp(s - m_new)
    l = alpha * l + jnp.sum(p, axis=-1, keepdims=True)
    acc = alpha * acc + _dot(p.astype(BF16), v)
    return m_new, l, acc


def _softmax_init(rows, hd):
    return (jnp.full((rows, 1), NEG_INF, F32), jnp.zeros((rows, 1), F32), jnp.zeros((rows, hd), F32))


def _z_column_sources():
    s_kva = Q_A_W
    s_gate = s_kva + KV_A_W
    s_qb = s_gate + GATE_A_W
    s_kb = s_qb + Q_B_W
    s_qc = s_kb + 2 * KV_B_W
    s_merge = s_qc + 3 * C_W
    assert GATE_A_PAD == RELAYOUT_COLS
    segments = [(OFF_MERGE, s_merge, MERGE_W), (OFF_QA, 0, Q_A_W), (OFF_QB, s_qb, Q_B_W),
                (OFF_QC, s_qc, 3 * C_W), (OFF_KVA, s_kva, KV_A_W), (OFF_KB, s_kb, 2 * KV_B_W)]
    src = np.zeros((Z_W // RELAYOUT_COLS,), np.int32)
    valid = np.zeros((Z_W // RELAYOUT_COLS,), np.int32)
    for dst, start, width in segments:
        assert dst % RELAYOUT_COLS == 0 and width % RELAYOUT_COLS == 0
        for b in range(width // RELAYOUT_COLS):
            src[dst // RELAYOUT_COLS + b] = start + b * RELAYOUT_COLS
            valid[dst // RELAYOUT_COLS + b] = RELAYOUT_COLS
    src[OFF_GATE_A // RELAYOUT_COLS] = s_gate
    valid[OFF_GATE_A // RELAYOUT_COLS] = GATE_A_W
    assert np.all(src % SUBLANES == 0) and np.all(src + RELAYOUT_COLS <= D_IN)
    return jnp.asarray(src // SUBLANES), jnp.asarray(valid)


def _prep_w_in_kernel(src_ref, valid_ref, w_ref, o_ref):
    j = pl.program_id(0)
    w = w_ref[0]
    w = jnp.where(_iota(w.shape, 0) < valid_ref[j], w, 0.0)
    o_ref[...] = w.T.astype(BF16)


def _prep_w_in(w_in_t, layer):
    src, valid = _z_column_sources()
    grid_spec = pltpu.PrefetchScalarGridSpec(
        num_scalar_prefetch=2,
        grid=(Z_W // RELAYOUT_COLS,),
        in_specs=[pl.BlockSpec((pl.Element(1), pl.Element(RELAYOUT_COLS), pl.Element(D_MODEL)),
                               lambda j, src, valid: (layer, src[j] * SUBLANES, 0))],
        out_specs=pl.BlockSpec((D_MODEL, RELAYOUT_COLS), lambda j, src, valid: (0, j)),
    )
    return pl.pallas_call(
        _prep_w_in_kernel,
        grid_spec=grid_spec,
        out_shape=jax.ShapeDtypeStruct((D_MODEL, Z_W), BF16),
        compiler_params=_cparams("parallel"),
        name="w_in_relayout",
    )(src, valid, w_in_t)


def _mm_kernel(a_ref, b_ref, o_ref):
    o_ref[...] = _dot(a_ref[...], b_ref[...]).astype(o_ref.dtype)


def _matmul(a, b, out_dtype):
    m, k = a.shape
    n = b.shape[1]
    return pl.pallas_call(
        _mm_kernel,
        grid=(m // MM_TM, n // MM_TN),
        in_specs=[pl.BlockSpec((MM_TM, k), lambda i, j: (i, 0)),
                  pl.BlockSpec((k, MM_TN), lambda i, j: (0, j))],
        out_specs=pl.BlockSpec((MM_TM, MM_TN), lambda i, j: (i, j)),
        out_shape=jax.ShapeDtypeStruct((m, n), out_dtype),
        compiler_params=_cparams("parallel", "parallel"),
        name="in_proj",
    )(a, b)


def _cmp_kv_kernel(k_ref, pos_ref, w_ref, o_ref, kf_ref, *, seq):
    kf_ref[0:seq, :] = k_ref[...].astype(F32)
    kf_ref[seq:seq + LANES, :] = jnp.zeros((LANES, HD_A), F32)
    acc = jnp.zeros((LANES, HD_A), F32)
    for l in range(L_CMP):
        rows = kf_ref[pl.ds(l, LANES, stride=STRIDE_CMP), :] + pos_ref[0, 0, l:l + 1, :]
        acc = acc + _dot(rows.astype(BF16), w_ref[0, 0, l].astype(BF16))
    o_ref[0, 0] = acc.astype(o_ref.dtype)


def _cmp_kv(z, cmp_pos, cmp_w, layer, batch, seq):
    blk0 = OFF_KVA // LANES
    return pl.pallas_call(
        functools.partial(_cmp_kv_kernel, seq=seq),
        grid=(batch, 2 * KV_A),
        in_specs=[pl.BlockSpec((seq, LANES), lambda b, j: (b, blk0 + j)),
                  pl.BlockSpec((1, 1, L_CMP, HD_A), lambda b, j: (layer, j // KV_A, 0, 0)),
                  pl.BlockSpec((1, 1, L_CMP, HD_A, HD_A), lambda b, j: (layer, j // KV_A, 0, 0, 0))],
        out_specs=pl.BlockSpec((1, 1, LANES, HD_A), lambda b, j: (b, j, 0, 0)),
        out_shape=jax.ShapeDtypeStruct((batch, 2 * KV_A, LANES, HD_A), BF16),
        scratch_shapes=[pltpu.VMEM((seq + LANES, HD_A), F32)],
        compiler_params=_cparams("parallel", "parallel"),
        name="nsa_cmp_kv",
    )(z, cmp_pos, cmp_w)


def _nsa_cmp_kernel(q_ref, kvc_ref, gate_ref, inter_t_ref, o_ref, nsel_ref, *, tq):
    i = pl.program_id(1)
    shape = (tq, LANES)
    t = i * tq + _iota(shape, 0)
    lane = _iota(shape, 1)
    dist_i = t - (lane * STRIDE_CMP + (L_CMP - 1))
    ok = dist_i >= 0
    okf = ok.astype(F32)
    dist = dist_i.astype(F32)
    sig = jax.nn.sigmoid(gate_ref[...].astype(F32))
    scale = HD_A ** -0.5
    n_sel_rows = 32
    inter_t = inter_t_ref[...]
    shape_t = (n_sel_rows, tq)
    j = _iota(shape_t, 0)
    blk_t = (i * tq + _iota(shape_t, 1)) // L_SEL
    valid = j <= blk_t
    forced = (j == 0) | (j == blk_t) | (j == blk_t - 1)
    for g in range(KV_A):
        kc = kvc_ref[0, g]
        vc = kvc_ref[0, KV_A + g]
        psum = jnp.zeros(shape, F32)
        outs = []
        for h in range(HPG_A):
            head = g * HPG_A + h
            q = q_ref[:, head * HD_A:(head + 1) * HD_A]
            s = _dot_nt(q, kc) * scale - (2.0 ** (-8.0 * (head + 1) / H_A)) * dist
            s = jnp.where(ok, s, NEG_INF)
            e = jnp.exp(s - jnp.max(s, axis=-1, keepdims=True))
            p = e / jnp.sum(e, axis=-1, keepdims=True) * okf
            psum = psum + p
            o = _dot(p.astype(BF16), vc)
            outs.append(o * sig[:, head * 3:head * 3 + 1])
        o_ref[:, g * HPG_A * HD_A:(g + 1) * HPG_A * HD_A] = jnp.concatenate(outs, axis=1).astype(o_ref.dtype)

        imp_t = sum(_dot_nt(inter_t, part) for part in _split3(psum))[0:n_sel_rows]
        vals = jnp.where(forced, jnp.inf, jnp.where(valid, imp_t, -jnp.inf))
        sel = _topk_rows(vals, N_SEL)
        not_selected = jnp.where(valid & (sel > 0.5), 0.0, 1.0)
        nsel_ref[0, g] = _rows_to_lanes(not_selected, tq).astype(nsel_ref.dtype)


def _nsa_cmp(z, kvc, inter_t, batch, seq):
    tq = ATT_TQ
    nq = seq // tq
    gate_blk = OFF_GATE_A // LANES
    return pl.pallas_call(
        functools.partial(_nsa_cmp_kernel, tq=tq),
        grid=(batch, nq),
        in_specs=[pl.BlockSpec((tq, Q_A_W), lambda b, i: (b * nq + i, OFF_QA // Q_A_W)),
                  pl.BlockSpec((1, 2 * KV_A, LANES, HD_A), lambda b, i: (b, 0, 0, 0)),
                  pl.BlockSpec((tq, LANES), lambda b, i: (b * nq + i, gate_blk)),
                  pl.BlockSpec((LANES, LANES), lambda b, i: (0, 0))],
        out_specs=[pl.BlockSpec((tq, Q_A_W), lambda b, i: (b * nq + i, 0)),
                   pl.BlockSpec((1, KV_A, tq, LANES), lambda b, i: (b, 0, i, 0))],
        out_shape=[jax.ShapeDtypeStruct((batch * seq, Q_A_W), BF16),
                   jax.ShapeDtypeStruct((batch, KV_A, seq, LANES), BF16)],
        compiler_params=_cparams("parallel", "parallel"),
        name="nsa_cmp_attn",
    )(z, kvc, z, inter_t)


def _scaled_q(q_ref, h, hd, scale):
    return (q_ref[:, h * hd:(h + 1) * hd].astype(F32) * scale).astype(BF16)


def _gated_heads(o, sig, g, branch, tq):
    outs = [o[h * tq:(h + 1) * tq] * _pick_lane(sig, (g * HPG_A + h) * 3 + branch) for h in range(HPG_A)]
    return jnp.concatenate(outs, axis=1)


def _nsa_slc_kernel(slopes_ref, q_ref, k_ref, v_ref, kaug_ref, causal_ref, nsel_ref, gate_ref, o_ref, *, tq):
    g = pl.program_id(1)
    i = pl.program_id(2)
    rows = HPG_A * tq
    scale = HD_A ** -0.5
    nsel = nsel_ref[0, 0].astype(F32)
    center = (i * tq).astype(F32)
    qx = jnp.concatenate(
        [jnp.concatenate([_scaled_q(q_ref, h, HD_A, scale),
                          _query_aug(nsel, slopes_ref[g * HPG_A + h], center)], axis=1)
         for h in range(HPG_A)], axis=0)

    def scores(kt):
        k0 = pl.multiple_of(kt * tq, tq)
        kx = jnp.concatenate([k_ref[pl.ds(k0, tq), :], kaug_ref[pl.ds(k0, tq), :]], axis=1)
        return _dot_nt(qx, kx)

    def values(kt):
        return v_ref[pl.ds(pl.multiple_of(kt * tq, tq), tq), :]

    def body(kt, carry):
        return _softmax_update(scores(kt), values(kt), carry)

    m, l, acc = lax.fori_loop(0, i, body, _softmax_init(rows, HD_A))
    causal_bias = jnp.concatenate([causal_ref[...]] * HPG_A, axis=0)
    _, l, acc = _softmax_update(scores(i) + causal_bias, values(i), (m, l, acc))
    sig = jax.nn.sigmoid(gate_ref[...].astype(F32))
    o_ref[...] = _gated_heads(acc / l, sig, g, 1, tq).astype(o_ref.dtype)


def _nsa_slc(z, nsel, kaug, slopes, batch, seq):
    tq = ATT_TQ
    nq = seq // tq
    qw = HPG_A * HD_A
    kblk = OFF_KVA // LANES + 1 * 2 * KV_A
    vblk = kblk + KV_A
    gate_blk = OFF_GATE_A // LANES
    causal = jnp.asarray(np.where(np.arange(tq)[:, None] >= np.arange(tq)[None, :], 0.0, -BIG).astype(np.float32))
    return pl.pallas_call(
        functools.partial(_nsa_slc_kernel, tq=tq),
        grid=(batch, KV_A, nq),
        in_specs=[pl.BlockSpec(memory_space=pltpu.SMEM),
                  pl.BlockSpec((tq, qw), lambda b, g, i: (b * nq + i, OFF_QA // qw + g)),
                  pl.BlockSpec((seq, LANES), lambda b, g, i: (b, kblk + g)),
                  pl.BlockSpec((seq, LANES), lambda b, g, i: (b, vblk + g)),
                  pl.BlockSpec((seq, LANES), lambda b, g, i: (0, 0)),
                  pl.BlockSpec((tq, tq), lambda b, g, i: (0, 0)),
                  pl.BlockSpec((1, 1, tq, LANES), lambda b, g, i: (b, g, i, 0)),
                  pl.BlockSpec((tq, LANES), lambda b, g, i: (b * nq + i, gate_blk))],
        out_specs=pl.BlockSpec((tq, qw), lambda b, g, i: (b * nq + i, g)),
        out_shape=jax.ShapeDtypeStruct((batch * seq, Q_A_W), BF16),
        compiler_params=_cparams("parallel", "parallel", "parallel"),
        name="nsa_slc_attn",
    )(slopes, z, z, z, kaug, causal, nsel, z)


def _band_mask(tq, span, window, n_heads):
    r = _iota((n_heads * tq, span), 0)
    r = r - (r // tq) * tq
    c = _iota((n_heads * tq, span), 1)
    return (c > r) & (c <= r + window)


def _nsa_win_kernel(slopes_ref, q_ref, k_ref, v_ref, kaug_ref, band_ref, gate_ref, o_ref, kx_ref, vx_ref, *, tq, window, seq):
    g = pl.program_id(1)
    i = pl.program_id(2)

    @pl.when(i == 0)
    def _():
        kx_ref[0:window, 0:HD_A] = jnp.zeros((window, HD_A), BF16)
        kx_ref[window:window + seq, 0:HD_A] = k_ref[...]
        kx_ref[:, HD_A:2 * HD_A] = kaug_ref[...]
        vx_ref[0:window, :] = jnp.zeros((window, HD_A), BF16)
        vx_ref[window:window + seq, :] = v_ref[...]

    span = window + tq
    r0 = pl.multiple_of(i * tq, tq)
    center = (i * tq + window).astype(F32)
    base = jnp.zeros((tq, LANES), F32)
    sig = jax.nn.sigmoid(gate_ref[...].astype(F32))
    pair = 2
    band_bias = jnp.concatenate([band_ref[...]] * pair, axis=0)
    for h0 in range(0, HPG_A, pair):
        qx = jnp.concatenate(
            [jnp.concatenate([_scaled_q(q_ref, h, HD_A, HD_A ** -0.5),
                              _query_aug(base, slopes_ref[g * HPG_A + h], center, pad_flag=True)], axis=1)
             for h in range(h0, h0 + pair)], axis=0)
        s = _dot_nt(qx, kx_ref[pl.ds(r0, span), :]) + band_bias
        p = jnp.exp(s - jnp.max(s, axis=-1, keepdims=True))
        o = _dot(p.astype(BF16), vx_ref[pl.ds(r0, span), :]) / jnp.sum(p, axis=-1, keepdims=True)
        for h in range(h0, h0 + pair):
            gated = o[(h - h0) * tq:(h - h0 + 1) * tq] * _pick_lane(sig, (g * HPG_A + h) * 3 + 2)
            o_ref[:, h * HD_A:(h + 1) * HD_A] = gated.astype(o_ref.dtype)


def _band_bias(tq, window):
    r = np.arange(tq)[:, None]
    c = np.arange(window + tq)[None, :]
    return jnp.asarray(np.where((c > r) & (c <= r + window), 0.0, -BIG).astype(np.float32))


def _nsa_win(z, kaug_pad, slopes, batch, seq):
    tq = ATT_TQ
    nq = seq // tq
    qw = HPG_A * HD_A
    kblk = OFF_KVA // LANES + 2 * 2 * KV_A
    vblk = kblk + KV_A
    gate_blk = OFF_GATE_A // LANES
    return pl.pallas_call(
        functools.partial(_nsa_win_kernel, tq=tq, window=WIN_A, seq=seq),
        grid=(batch, KV_A, nq),
        in_specs=[pl.BlockSpec(memory_space=pltpu.SMEM),
                  pl.BlockSpec((tq, qw), lambda b, g, i: (b * nq + i, OFF_QA // qw + g)),
                  pl.BlockSpec((seq, LANES), lambda b, g, i: (b, kblk + g)),
                  pl.BlockSpec((seq, LANES), lambda b, g, i: (b, vblk + g)),
                  pl.BlockSpec((WIN_A + seq, LANES), lambda b, g, i: (0, 0)),
                  pl.BlockSpec((tq, WIN_A + tq), lambda b, g, i: (0, 0)),
                  pl.BlockSpec((tq, LANES), lambda b, g, i: (b * nq + i, gate_blk))],
        out_specs=pl.BlockSpec((tq, qw), lambda b, g, i: (b * nq + i, g)),
        out_shape=jax.ShapeDtypeStruct((batch * seq, Q_A_W), BF16),
        scratch_shapes=[pltpu.VMEM((WIN_A + seq, 2 * HD_A), BF16), pltpu.VMEM((WIN_A + seq, HD_A), BF16)],
        compiler_params=_cparams("parallel", "parallel", "arbitrary"),
        name="nsa_win_attn",
    )(slopes, z, z, z, kaug_pad, _band_bias(tq, WIN_A), z)


SW_POS_HI, SW_POS_LO, SW_ONE, SW_PAD = 0, 1, 2, 3


def _swa_kernel(sink_ref, q_ref, k_ref, v_ref, kaug_ref, o_ref, kx_ref, vx_ref, *, tq, window, seq):
    i = pl.program_id(1)

    @pl.when(i == 0)
    def _():
        for g in range(KV_B):
            kx_ref[g, 0:window, 0:HD_B] = jnp.zeros((window, HD_B), BF16)
            kx_ref[g, window:window + seq, 0:HD_B] = k_ref[:, g * HD_B:(g + 1) * HD_B]
            kx_ref[g, :, HD_B:2 * HD_B] = kaug_ref[...]
            vx_ref[g, 0:window, :] = jnp.zeros((window, HD_B), BF16)
            vx_ref[g, window:window + seq, :] = v_ref[:, g * HD_B:(g + 1) * HD_B]

    span = window + tq
    rows = HPG_B * tq
    r0 = pl.multiple_of(i * tq, tq)
    center = (i * tq + window).astype(F32)
    band = _band_mask(tq, span, window, HPG_B)
    lane = _iota((tq, HD_B), 1)
    hh = _iota((rows, 1), 0) // tq
    row_in_tile = (_iota((rows, 1), 0) - hh * tq).astype(F32)

    def head_column(values):
        col = jnp.full((rows, 1), values[HPG_B - 1], F32)
        for h in range(HPG_B - 2, -1, -1):
            col = jnp.where(hh == h, values[h], col)
        return col

    outs = []
    for g in range(KV_B):
        slopes = [2.0 ** (-8.0 * (g * HPG_B + h + 1) / H_B) for h in range(HPG_B)]
        parts = []
        for h in range(HPG_B):
            aug = jnp.where(lane == SW_POS_HI, slopes[h] * POS_SPLIT, jnp.where(lane == SW_POS_LO, slopes[h], 0.0))
            aug = jnp.where(lane == SW_ONE, -slopes[h] * center, jnp.where(lane == SW_PAD, 1.0, aug))
            parts.append(jnp.concatenate([_scaled_q(q_ref, g * HPG_B + h, HD_B, HD_B ** -0.5),
                                          aug.astype(BF16)], axis=1))
        qx = jnp.concatenate(parts, axis=0)
        s = _dot_nt(qx, kx_ref[g, pl.ds(r0, span), :])
        s = jnp.where(band, s, -BIG)
        sink_shifted = (head_column([sink_ref[g * HPG_B + h] for h in range(HPG_B)])
                        + head_column(slopes) * row_in_tile)
        m = jnp.maximum(jnp.max(s, axis=-1, keepdims=True), sink_shifted)
        p = jnp.exp(s - m)
        denom = jnp.sum(p, axis=-1, keepdims=True) + jnp.exp(sink_shifted - m)
        o = _dot(p.astype(BF16), vx_ref[g, pl.ds(r0, span), :]) / denom
        outs += [o[h * tq:(h + 1) * tq] for h in range(HPG_B)]
    o_ref[...] = jnp.concatenate(outs, axis=1).astype(o_ref.dtype)


def _swa(z, kaug_sw, sink, layer, batch, seq):
    tq = SWA_TQ
    nq = seq // tq
    return pl.pallas_call(
        functools.partial(_swa_kernel, tq=tq, window=WIN_B, seq=seq),
        grid=(batch, nq),
        in_specs=[pl.BlockSpec(memory_space=pltpu.SMEM),
                  pl.BlockSpec((tq, Q_B_W), lambda b, i: (b * nq + i, OFF_QB // Q_B_W)),
                  pl.BlockSpec((seq, KV_B_W), lambda b, i: (b, OFF_KB // KV_B_W)),
                  pl.BlockSpec((seq, KV_B_W), lambda b, i: (b, OFF_VB // KV_B_W)),
                  pl.BlockSpec((WIN_B + seq, HD_B), lambda b, i: (0, 0))],
        out_specs=pl.BlockSpec((tq, Q_B_W), lambda b, i: (b * nq + i, 0)),
        out_shape=jax.ShapeDtypeStruct((batch * seq, Q_B_W), BF16),
        scratch_shapes=[pltpu.VMEM((KV_B, WIN_B + seq, 2 * HD_B), BF16),
                        pltpu.VMEM((KV_B, WIN_B + seq, HD_B), BF16)],
        compiler_params=_cparams("parallel", "arbitrary"),
        name="swa_attn",
    )(sink[layer], z, z, z, kaug_sw)


def _moba_kernel(slopes_ref, q_ref, k_ref, v_ref, kaug_ref, avg_ref, o_ref, km_ref, *, seq):
    i = pl.program_id(1)
    tq = BLK_C
    n_blk = seq // BLK_C
    scale = HD_C ** -0.5

    @pl.when(i == 0)
    def _():
        for h in range(H_C):
            k_mean = _dot(avg_ref[...], k_ref[:, h * HD_C:(h + 1) * HD_C])[0:SUBLANES]
            terms = [t.astype(F32) for t in _split3(k_mean)] + [jnp.zeros((SUBLANES, HD_C), F32)]
            km_ref[h] = jnp.concatenate(terms, axis=0).astype(BF16)

    center = (i * tq).astype(F32)
    blk = _iota((SUBLANES, tq), 0)
    past = blk < i
    qx = []
    for h in range(H_C):
        q = q_ref[:, h * HD_C:(h + 1) * HD_C]
        sc = _dot_nt(km_ref[h], q)
        score_t = sc[0:SUBLANES] + sc[SUBLANES:2 * SUBLANES] + sc[2 * SUBLANES:3 * SUBLANES]
        sel = _topk_rows(jnp.where(past, score_t, -jnp.inf), TOPK_C)
        not_selected = jnp.where(past & (sel < 0.5), 1.0, 0.0)
        qx.append(jnp.concatenate([_scaled_q(q_ref, h, HD_C, scale),
                                   _query_aug(_rows_to_lanes(not_selected, tq), slopes_ref[h], center)], axis=1))
    assert n_blk <= SUBLANES

    def scores(h, kt):
        k0 = pl.multiple_of(kt * tq, tq)
        kx = jnp.concatenate([k_ref[pl.ds(k0, tq), h * HD_C:(h + 1) * HD_C], kaug_ref[pl.ds(k0, tq), :]], axis=1)
        return _dot_nt(qx[h], kx)

    def values(h, kt):
        return v_ref[pl.ds(pl.multiple_of(kt * tq, tq), tq), h * HD_C:(h + 1) * HD_C]

    def body(kt, carry):
        out = []
        for h in range(H_C):
            s, m, l, acc = carry[h]
            s_next = scores(h, kt + 1)
            out.append((s_next,) + _softmax_update(s, values(h, kt), (m, l, acc)))
        return tuple(out)

    init = tuple((scores(h, 0),) + _softmax_init(tq, HD_C) for h in range(H_C))
    carry = lax.fori_loop(0, i, body, init)
    causal = _iota((tq, tq), 0) >= _iota((tq, tq), 1)
    outs = []
    for h in range(H_C):
        s, m, l, acc = carry[h]
        _, l, acc = _softmax_update(jnp.where(causal, s, -BIG), values(h, i), (m, l, acc))
        outs.append(acc / l)
    o_ref[...] = jnp.concatenate(outs, axis=1).astype(o_ref.dtype)


def _moba(z, kaug, avg, slopes, batch, seq):
    nq = seq // BLK_C
    return pl.pallas_call(
        functools.partial(_moba_kernel, seq=seq),
        grid=(batch, nq),
        in_specs=[pl.BlockSpec(memory_space=pltpu.SMEM),
                  pl.BlockSpec((BLK_C, C_W), lambda b, i: (b * nq + i, OFF_QC // C_W)),
                  pl.BlockSpec((seq, C_W), lambda b, i: (b, OFF_KC // C_W)),
                  pl.BlockSpec((seq, C_W), lambda b, i: (b, OFF_VC // C_W)),
                  pl.BlockSpec((seq, LANES), lambda b, i: (0, 0)),
                  pl.BlockSpec((2 * SUBLANES, seq), lambda b, i: (0, 0))],
        out_specs=pl.BlockSpec((BLK_C, C_W), lambda b, i: (b * nq + i, 0)),
        out_shape=jax.ShapeDtypeStruct((batch * seq, C_W), BF16),
        scratch_shapes=[pltpu.VMEM((H_C, 4 * SUBLANES, HD_C), BF16)],
        compiler_params=_cparams("parallel", "arbitrary"),
        name="moba_attn",
    )(slopes, z, z, z, kaug, avg)


def _merge_kernel(oc_ref, os_ref, ow_ref, ob_ref, om_ref, g0_ref, g1_ref, g2_ref, wa_ref, wb_ref, wc_ref, o_ref):
    o_a = (oc_ref[...].astype(F32) + os_ref[...].astype(F32) + ow_ref[...].astype(F32)).astype(BF16)
    merged = jax.nn.sigmoid(g0_ref[...].astype(F32)) * _dot(o_a, wa_ref[...])
    merged = merged + jax.nn.sigmoid(g1_ref[...].astype(F32)) * _dot(ob_ref[...], wb_ref[...])
    merged = merged + jax.nn.sigmoid(g2_ref[...].astype(F32)) * _dot(om_ref[...], wc_ref[...])
    o_ref[...] = merged.astype(o_ref.dtype)


def _merge(o_cmp, o_slc, o_win, o_b, o_c, z, wa, wb, wc):
    tm = ROW_TM
    t = z.shape[0]
    gblk = OFF_MERGE // D_MODEL
    row = lambda w: pl.BlockSpec((tm, w), lambda i: (i, 0))
    full = lambda a: pl.BlockSpec(a.shape, lambda i: (0, 0))
    gate = lambda r: pl.BlockSpec((tm, D_MODEL), lambda i: (i, gblk + r))
    return pl.pallas_call(
        _merge_kernel,
        grid=(t // tm,),
        in_specs=[row(Q_A_W), row(Q_A_W), row(Q_A_W), row(Q_B_W), row(C_W),
                  gate(0), gate(1), gate(2), full(wa), full(wb), full(wc)],
        out_specs=row(D_MODEL),
        out_shape=jax.ShapeDtypeStruct((t, D_MODEL), BF16),
        compiler_params=_cparams("parallel"),
        name="mixer_merge",
    )(o_cmp, o_slc, o_win, o_b, o_c, z, z, z, wa, wb, wc)


def _layer_norm(h, g_ref, b_ref):
    mu = jnp.mean(h, axis=-1, keepdims=True)
    xc = h - mu
    var = jnp.mean(xc * xc, axis=-1, keepdims=True)
    return xc * lax.rsqrt(var + LN_EPS) * g_ref[...] + b_ref[...]


def _proj_ln_kernel(m_ref, w_ref, x_ref, g_ref, b_ref, xo_ref, xb_ref):
    y = _dot(m_ref[...], w_ref[...])
    out = _layer_norm(ALPHA * x_ref[...] + y, g_ref, b_ref)
    xo_ref[...] = out
    xb_ref[...] = out.astype(BF16)


def _proj_ln(merged, w_out, x, g, b):
    tm = ROW_TM
    t = x.shape[0]
    row = pl.BlockSpec((tm, D_MODEL), lambda i: (i, 0))
    vec = pl.BlockSpec((1, D_MODEL), lambda i: (0, 0))
    return pl.pallas_call(
        _proj_ln_kernel,
        grid=(t // tm,),
        in_specs=[row, pl.BlockSpec((D_MODEL, D_MODEL), lambda i: (0, 0)), row, vec, vec],
        out_specs=[row, row],
        out_shape=[jax.ShapeDtypeStruct((t, D_MODEL), F32), jax.ShapeDtypeStruct((t, D_MODEL), BF16)],
        compiler_params=_cparams("parallel"),
        name="out_proj_ln",
    )(merged, w_out, x, g, b)


def _moe_ln_kernel(x_ref, ya_ref, yb_ref, w_ref, g_ref, b_ref, *rest):
    xo_ref, xb_ref = rest[-2:]
    w = w_ref[...]
    y = w[:, 0:1] * ya_ref[...].astype(F32) + w[:, 1:2] * yb_ref[...].astype(F32)
    out = _layer_norm(ALPHA * x_ref[...] + y, g_ref, b_ref)
    xo_ref[...] = out
    xb_ref[...] = out.astype(BF16)


def _moe_ln(x, y_groups, wts, g, b):
    tm = ROW_TM
    t = x.shape[0]
    blocks = t // tm // len(y_groups)
    vec = pl.BlockSpec((1, D_MODEL), lambda i: (0, 0))
    group_row = pl.BlockSpec((tm, D_MODEL), lambda i: (i, 0))
    outs = None
    for n, (ya, yb) in enumerate(y_groups):
        off = n * blocks
        row = pl.BlockSpec((tm, D_MODEL), lambda i, off=off: (off + i, 0))
        in_specs = [row, group_row, group_row, pl.BlockSpec((tm, LANES), lambda i, off=off: (off + i, 0)), vec, vec]
        args = [x, ya, yb, wts, g, b]
        aliases = {}
        if outs is not None:
            in_specs += [pl.BlockSpec(memory_space=pl.ANY)] * 2
            aliases = {len(args): 0, len(args) + 1: 1}
            args += list(outs)
        outs = pl.pallas_call(
            _moe_ln_kernel,
            grid=(blocks,),
            in_specs=in_specs,
            out_specs=[row, row],
            out_shape=[jax.ShapeDtypeStruct((t, D_MODEL), F32), jax.ShapeDtypeStruct((t, D_MODEL), BF16)],
            input_output_aliases=aliases,
            compiler_params=_cparams("parallel"),
            name="moe_combine_ln",
        )(*args)
    return outs


def _router_kernel(x_ref, w_ref, b_ref, id_ref, wt_ref):
    x = x_ref[...]
    x_hi = x.astype(BF16)
    x_lo = (x - x_hi.astype(F32)).astype(BF16)
    w = w_ref[...]
    w_hi = w.astype(BF16)
    w_lo = (w - w_hi.astype(F32)).astype(BF16)
    logits = _dot(x_hi, w_hi) + _dot(x_lo, w_hi) + _dot(x_hi, w_lo) + b_ref[...]
    lane = _iota(logits.shape, 1).astype(F32)
    first = lambda hit: jnp.min(jnp.where(hit, lane, float(LANES)), axis=-1, keepdims=True)
    gl = jnp.where(lane < N_GROUPS, logits, -jnp.inf)
    gm = jnp.max(gl, axis=-1, keepdims=True)
    g_w = 1.0 / jnp.sum(jnp.exp(gl - gm), axis=-1, keepdims=True)
    lo = N_GROUPS + first(gl == gm) * EXP_PER_GROUP
    el = jnp.where((lane >= lo) & (lane < lo + EXP_PER_GROUP), logits, -jnp.inf)
    m1 = jnp.max(el, axis=-1, keepdims=True)
    i1 = first(el == m1)
    el2 = jnp.where(lane == i1, -jnp.inf, el)
    m2 = jnp.max(el2, axis=-1, keepdims=True)
    i2 = first(el2 == m2)
    e2 = jnp.exp(m2 - m1)
    w1 = g_w / (1.0 + e2)
    w2 = g_w * e2 / (1.0 + e2)
    ids = jnp.where(lane == 0.0, i1 - N_GROUPS, jnp.where(lane == 1.0, i2 - N_GROUPS, 0.0))
    id_ref[...] = ids.astype(jnp.int32)
    wt_ref[...] = jnp.where(lane == 0.0, w1, jnp.where(lane == 1.0, w2, 0.0))


def _router(x, w_rt, b_rt):
    tm = ROUTE_TM
    t = x.shape[0]
    return pl.pallas_call(
        _router_kernel,
        grid=(t // tm,),
        in_specs=[pl.BlockSpec((tm, D_MODEL), lambda i: (i, 0)),
                  pl.BlockSpec((D_MODEL, LANES), lambda i: (0, 0)),
                  pl.BlockSpec((1, LANES), lambda i: (0, 0))],
        out_specs=[pl.BlockSpec((tm, LANES), lambda i: (i, 0)), pl.BlockSpec((tm, LANES), lambda i: (i, 0))],
        out_shape=[jax.ShapeDtypeStruct((t, LANES), jnp.int32), jax.ShapeDtypeStruct((t, LANES), F32)],
        compiler_params=_cparams("parallel"),
        name="moe_router",
    )(x, w_rt, b_rt)


def _expert_kernel(te_ref, na_ref, x_ref, wg_ref, wu_ref, wd_ref, *rest, tile0):
    y_ref = rest[-1]
    j = tile0 + pl.program_id(0)

    @pl.when(j < na_ref[0])
    def _():
        x = x_ref[...]
        hg = _dot(x, wg_ref[0, 0].astype(BF16))
        hu = _dot(x, wu_ref[0, 0].astype(BF16))
        a = hg * jax.nn.sigmoid(hg) * hu
        y_ref[...] = _dot(a.astype(BF16), wd_ref[0, 0].astype(BF16)).astype(y_ref.dtype)

    @pl.when(j >= na_ref[0])
    def _():
        y_ref[...] = jnp.zeros(y_ref.shape, y_ref.dtype)


def _experts(x_chunks, tile_expert, n_active, wg, wu, wd, layer):
    tm = EXP_TM
    tiles_per_chunk = x_chunks[0].shape[0] // tm
    n_tiles = len(x_chunks) * tiles_per_chunk
    y = None
    for c, x_c in enumerate(x_chunks):
        tile0 = c * tiles_per_chunk
        x_map = lambda j, te, na, tile0=tile0: (
            jnp.clip(jnp.minimum(tile0 + j, na[0] - 1) - tile0, 0, tiles_per_chunk - 1), 0)
        w_map = lambda j, te, na, tile0=tile0: (layer, te[tile0 + j], 0, 0)
        in_specs = [pl.BlockSpec((tm, D_MODEL), x_map),
                    pl.BlockSpec((1, 1, D_MODEL, D_EXPERT), w_map),
                    pl.BlockSpec((1, 1, D_MODEL, D_EXPERT), w_map),
                    pl.BlockSpec((1, 1, D_EXPERT, D_MODEL), w_map)]
        args = [tile_expert, n_active, x_c, wg, wu, wd]
        aliases = {}
        if y is not None:
            in_specs.append(pl.BlockSpec(memory_space=pl.ANY))
            aliases = {len(args): 0}
            args.append(y)
        grid_spec = pltpu.PrefetchScalarGridSpec(
            num_scalar_prefetch=2,
            grid=(tiles_per_chunk,),
            in_specs=in_specs,
            out_specs=pl.BlockSpec((tm, D_MODEL), lambda j, te, na, tile0=tile0: (tile0 + j, 0)),
        )
        y = pl.pallas_call(
            functools.partial(_expert_kernel, tile0=tile0),
            grid_spec=grid_spec,
            out_shape=jax.ShapeDtypeStruct((n_tiles * tm, D_MODEL), BF16),
            input_output_aliases=aliases,
            compiler_params=_cparams("arbitrary"),
            name="moe_experts",
        )(*args)
    return y


def _dispatch_plan(expert_ids, tm, n_tiles):
    t = expert_ids.shape[0]
    e_flat = expert_ids.reshape(-1)
    onehot = (e_flat[:, None] == jnp.arange(N_EXPERTS, dtype=jnp.int32)[None, :]).astype(jnp.int32)
    csum = jnp.cumsum(onehot, axis=0)
    rank = jnp.sum(csum * onehot, axis=1) - 1
    counts = csum[-1]
    padded = ((counts + tm - 1) // tm) * tm
    ends = jnp.cumsum(padded)
    dest = jnp.sum((ends - padded)[None, :] * onehot, axis=1) + rank
    filler = jnp.arange(n_tiles * tm, dtype=jnp.int32) % t
    scatter_add = lambda base, idx, vals: base.at[idx].add(vals, mode="promise_in_bounds", unique_indices=True)
    src_tok = compute_on("tpu_sparsecore")(jax.jit(scatter_add))(
        filler, dest, jnp.arange(2 * t, dtype=jnp.int32) // 2 - dest % t)
    n_active = ends[-1] // tm
    tile_start = jnp.minimum(jnp.arange(n_tiles, dtype=jnp.int32), n_active - 1) * tm
    te = jnp.sum((ends[None, :] <= tile_start[:, None]).astype(jnp.int32), axis=1)
    te = jnp.minimum(te, N_EXPERTS - 1)
    return dest.reshape(t, 2), src_tok, te, n_active.reshape(1).astype(jnp.int32)


def _bf16_const(a):
    a16 = a.astype(BF16)
    assert np.all(a16.astype(np.float32) == a)
    return jnp.asarray(a16)


def _alibi_np(n):
    slopes = np.asarray([2.0 ** (-8.0 * (i + 1) / n) for i in range(n)], np.float32)
    _bf16_const(slopes)
    return slopes


def _selection_overlap_t(seq):
    n_cmp = (seq - L_CMP) // STRIDE_CMP + 1
    n_sel = seq // L_SEL
    c_start = STRIDE_CMP * np.arange(n_cmp)
    s_start = L_SEL * np.arange(n_sel)
    inter = np.clip(np.minimum(c_start[:, None] + L_CMP, s_start[None, :] + L_SEL)
                    - np.maximum(c_start[:, None], s_start[None, :]), 0, None) / L_CMP
    out = np.zeros((LANES, LANES), np.float32)
    out[:n_sel, :n_cmp] = inter.T
    return _bf16_const(out)


def _key_aug(seq, block, pad_rows=0, width=LANES, lanes=(AUG_POS_HI, AUG_POS_LO, AUG_ONE, AUG_PAD)):
    rows = pad_rows + seq
    pos = np.arange(rows)
    key = pos - pad_rows
    real = key >= 0
    out = np.zeros((rows, width), np.float32)
    if block:
        out[pos[real], key[real] // block] = -BIG
    out[:, lanes[0]] = pos // POS_SPLIT
    out[:, lanes[1]] = pos % POS_SPLIT
    out[:, lanes[2]] = 1.0
    out[~real, lanes[3]] = -BIG
    return _bf16_const(out)


def _block_average(seq):
    out = np.zeros((2 * SUBLANES, seq), np.float32)
    for n in range(seq // BLK_C):
        out[n, n * BLK_C:(n + 1) * BLK_C] = 1.0 / BLK_C
    return _bf16_const(out)


def kernel(x, w_in, nsa_cmp_pos, nsa_cmp_w, sink_b, w_br_a, w_br_b, w_br_c, w_out, ln1_g, ln1_b,
           w_group, b_group, w_router, b_router, w_gate, w_up, w_down, ln2_g, ln2_b):
    batch, seq, d = x.shape
    t = batch * seq
    assert d == D_MODEL and w_in.shape[2] == D_IN and seq % BLK_C == 0 and t % MM_TM == 0
    assert seq // L_SEL <= AUG_POS_HI and (seq - L_CMP) // STRIDE_CMP + 1 < LANES
    assert seq // BLK_C <= SUBLANES and (WIN_A + seq) // POS_SPLIT < 256
    n_tiles = (TOPK_EXPERT * t) // EXP_TM + N_EXPERTS
    assert n_tiles % EXP_CHUNKS == 0 and (t // ROW_TM) % COMBINE_SPLIT == 0 and OFF_QA % Q_A_W == 0

    slopes_a = jnp.asarray(_alibi_np(H_A))
    slopes_c = jnp.asarray(_alibi_np(H_C))
    _alibi_np(H_B)
    inter_t = _selection_overlap_t(seq)
    kaug_slc = _key_aug(seq, L_SEL)
    kaug_win = _key_aug(seq, 0, pad_rows=WIN_A)
    kaug_swa = _key_aug(seq, 0, pad_rows=WIN_B, width=HD_B, lanes=(SW_POS_HI, SW_POS_LO, SW_ONE, SW_PAD))
    kaug_moba = _key_aug(seq, BLK_C)
    avg = _block_average(seq)
    take = lambda a, idx: a.at[idx].get(mode="promise_in_bounds")
    sc_take = compute_on("tpu_sparsecore")(jax.jit(take))

    w_in_t = jnp.swapaxes(w_in, 1, 2)
    xf = x.reshape(t, d)
    xb = xf.astype(BF16)
    for l in range(DEPTH):
        z = _matmul(xb, _prep_w_in(w_in_t, l), BF16)
        kvc = _cmp_kv(z, nsa_cmp_pos, nsa_cmp_w, l, batch, seq)
        o_cmp, nsel = _nsa_cmp(z, kvc, inter_t, batch, seq)
        o_slc = _nsa_slc(z, nsel, kaug_slc, slopes_a, batch, seq)
        o_win = _nsa_win(z, kaug_win, slopes_a, batch, seq)
        o_b = _swa(z, kaug_swa, sink_b, l, batch, seq)
        o_c = _moba(z, kaug_moba, avg, slopes_c, batch, seq)
        merged = _merge(o_cmp, o_slc, o_win, o_b, o_c, z,
                        w_br_a[l].astype(BF16), w_br_b[l].astype(BF16), w_br_c[l].astype(BF16))
        xf, xb = _proj_ln(merged, w_out[l].astype(BF16), xf, ln1_g[l][None, :], ln1_b[l][None, :])

        w_rt = jnp.concatenate([w_group[l], w_router[l],
                                jnp.zeros((d, LANES - N_GROUPS - N_EXPERTS), F32)], axis=1)
        b_rt = jnp.concatenate([b_group[l], b_router[l],
                                jnp.zeros((LANES - N_GROUPS - N_EXPERTS,), F32)])[None, :]
        ids, wts = _router(xf, w_rt, b_rt)
        dest, src_tok, tile_expert, n_active = _dispatch_plan(ids[:, :TOPK_EXPERT], EXP_TM, n_tiles)
        x_chunks = [sc_take(xb, c) for c in jnp.split(src_tok, EXP_CHUNKS)]
        y = _experts(x_chunks, tile_expert, n_active, w_gate, w_up, w_down, l)
        y_groups = [(take(y, d[:, 0]), take(y, d[:, 1])) for d in jnp.split(dest, COMBINE_SPLIT)]
        xf, xb = _moe_ln(xf, y_groups, wts, ln2_g[l][None, :], ln2_b[l][None, :])
    return xf.reshape(batch, seq, d)
```

```python
import functools

import numpy as np
import jax
import jax.numpy as jnp
from jax import lax
from jax.experimental import pallas as pl
from jax.experimental.pallas import tpu as pltpu
from jax.experimental.compute_on import compute_on

F32 = jnp.float32
BF16 = jnp.bfloat16

D_MODEL = 2048
DEPTH = 2
H_A, KV_A, HD_A = 8, 2, 128
HPG_A = H_A // KV_A
L_CMP, STRIDE_CMP, L_SEL, N_SEL, WIN_A = 32, 16, 64, 8, 512
H_B, KV_B, HD_B, WIN_B = 8, 2, 64, 128
HPG_B = H_B // KV_B
H_C, HD_C, BLK_C, TOPK_C = 4, 128, 256, 3
N_GROUPS, EXP_PER_GROUP, D_EXPERT, TOPK_EXPERT = 4, 8, 256, 2
N_EXPERTS = N_GROUPS * EXP_PER_GROUP
LN_EPS = 1e-5
NEG_INF = -1e30
ALPHA = (2.0 * DEPTH) ** 0.25

LANES = 128
SUBLANES = 8
VMEM_LIMIT = 48 * 1024 * 1024
BIG = 2.0 ** 100

Q_A_W = H_A * HD_A
KV_A_W = 3 * 2 * KV_A * HD_A
GATE_A_W = 3 * H_A
Q_B_W = H_B * HD_B
KV_B_W = KV_B * HD_B
C_W = H_C * HD_C
MERGE_W = 3 * D_MODEL
D_IN = Q_A_W + KV_A_W + GATE_A_W + Q_B_W + 2 * KV_B_W + 3 * C_W + MERGE_W
OFF_MERGE = 0
OFF_QA = OFF_MERGE + MERGE_W
OFF_QB = OFF_QA + Q_A_W
OFF_QC = OFF_QB + Q_B_W
OFF_KC = OFF_QC + C_W
OFF_VC = OFF_KC + C_W
OFF_KVA = OFF_VC + C_W
OFF_KB = OFF_KVA + KV_A_W
OFF_VB = OFF_KB + KV_B_W
OFF_GATE_A = OFF_VB + KV_B_W
GATE_A_PAD = 256
Z_W = OFF_GATE_A + GATE_A_PAD
assert OFF_QA % (HPG_A * HD_A) == 0 and OFF_QB % Q_B_W == 0 and OFF_MERGE % D_MODEL == 0
assert OFF_QC % C_W == 0 and OFF_KC % C_W == 0 and OFF_VC % C_W == 0

AUG_POS_HI, AUG_POS_LO, AUG_ONE, AUG_PAD = 32, 33, 34, 35
POS_SPLIT = 256

RELAYOUT_COLS = 256
MM_TM, MM_TN = 2048, 512
ROW_TM = 256
ROUTE_TM = 512
EXP_TM = 256
EXP_CHUNKS = 3
ATT_TQ = 256
SWA_TQ = 128
COMBINE_SPLIT = 2


def _cparams(*sem):
    return pltpu.CompilerParams(dimension_semantics=sem, vmem_limit_bytes=VMEM_LIMIT)


def _dot(a, b):
    return jnp.dot(a, b, preferred_element_type=F32)


def _dot_nt(a, b):
    return lax.dot_general(a, b, (((1,), (1,)), ((), ())), preferred_element_type=F32)


def _iota(shape, dim):
    return lax.broadcasted_iota(jnp.int32, shape, dim)


def _split3(x):
    hi = x.astype(BF16)
    r1 = x - hi.astype(F32)
    mid = r1.astype(BF16)
    lo = (r1 - mid.astype(F32)).astype(BF16)
    return hi, mid, lo


def _pick_lane(x, lane_idx):
    lane = _iota(x.shape, 1)
    return jnp.sum(jnp.where(lane == lane_idx, x, 0.0), axis=-1, keepdims=True)


def _topk_rows(vals, k):
    row = _iota(vals.shape, 0).astype(F32)
    sel = jnp.zeros(vals.shape, F32)
    for _ in range(k):
        m = jnp.max(vals, axis=0, keepdims=True)
        idx = jnp.min(jnp.where(vals == m, row, float(LANES)), axis=0, keepdims=True)
        pick = row == idx
        sel = jnp.where(pick, 1.0, sel)
        vals = jnp.where(pick, -jnp.inf, vals)
    return sel


def _rows_to_lanes(x_t, tq):
    pad = jnp.zeros((LANES - x_t.shape[0], tq), F32)
    return jnp.concatenate([x_t, pad], axis=0).T


def _query_aug(base, slope, center, pad_flag=False):
    lane = _iota(base.shape, 1)
    aug = jnp.where(lane == AUG_POS_HI, slope * float(POS_SPLIT), base)
    aug = jnp.where(lane == AUG_POS_LO, slope, aug)
    aug = jnp.where(lane == AUG_ONE, -slope * center, aug)
    if pad_flag:
        aug = jnp.where(lane == AUG_PAD, 1.0, aug)
    return aug.astype(BF16)


def _softmax_update(s, v, carry):
    m, l, acc = carry
    m_new = jnp.maximum(m, jnp.max(s, axis=-1, keepdims=True))
    alpha = jnp.exp(m - m_new)
    p = jnp.exp(s - m_new)
    l = alpha * l + jnp.sum(p, axis=-1, keepdims=True)
    acc = alpha * acc + _dot(p.astype(BF16), v)
    return m_new, l, acc


def _softmax_init(rows, hd):
    return (jnp.full((rows, 1), NEG_INF, F32), jnp.zeros((rows, 1), F32), jnp.zeros((rows, hd), F32))


def _z_column_sources():
    s_kva = Q_A_W
    s_gate = s_kva + KV_A_W
    s_qb = s_gate + GATE_A_W
    s_kb = s_qb + Q_B_W
    s_qc = s_kb + 2 * KV_B_W
    s_merge = s_qc + 3 * C_W
    assert GATE_A_PAD == RELAYOUT_COLS
    segments = [(OFF_MERGE, s_merge, MERGE_W), (OFF_QA, 0, Q_A_W), (OFF_QB, s_qb, Q_B_W),
                (OFF_QC, s_qc, 3 * C_W), (OFF_KVA, s_kva, KV_A_W), (OFF_KB, s_kb, 2 * KV_B_W)]
    src = np.zeros((Z_W // RELAYOUT_COLS,), np.int32)
    valid = np.zeros((Z_W // RELAYOUT_COLS,), np.int32)
    for dst, start, width in segments:
        assert dst % RELAYOUT_COLS == 0 and width % RELAYOUT_COLS == 0
        for b in range(width // RELAYOUT_COLS):
            src[dst // RELAYOUT_COLS + b] = start + b * RELAYOUT_COLS
            valid[dst // RELAYOUT_COLS + b] = RELAYOUT_COLS
    src[OFF_GATE_A // RELAYOUT_COLS] = s_gate
    valid[OFF_GATE_A // RELAYOUT_COLS] = GATE_A_W
    assert np.all(src % SUBLANES == 0) and np.all(src + RELAYOUT_COLS <= D_IN)
    return jnp.asarray(src // SUBLANES), jnp.asarray(valid)


def _prep_w_in_kernel(src_ref, valid_ref, w_ref, o_ref):
    j = pl.program_id(0)
    w = w_ref[0]
    w = jnp.where(_iota(w.shape, 0) < valid_ref[j], w, 0.0)
    o_ref[...] = w.T.astype(BF16)


def _prep_w_in(w_in_t, layer):
    src, valid = _z_column_sources()
    grid_spec = pltpu.PrefetchScalarGridSpec(
        num_scalar_prefetch=2,
        grid=(Z_W // RELAYOUT_COLS,),
        in_specs=[pl.BlockSpec((pl.Element(1), pl.Element(RELAYOUT_COLS), pl.Element(D_MODEL)),
                               lambda j, src, valid: (layer, src[j] * SUBLANES, 0))],
        out_specs=pl.BlockSpec((D_MODEL, RELAYOUT_COLS), lambda j, src, valid: (0, j)),
    )
    return pl.pallas_call(
        _prep_w_in_kernel,
        grid_spec=grid_spec,
        out_shape=jax.ShapeDtypeStruct((D_MODEL, Z_W), BF16),
        compiler_params=_cparams("parallel"),
        name="w_in_relayout",
    )(src, valid, w_in_t)


def _mm_kernel(a_ref, b_ref, o_ref):
    o_ref[...] = _dot(a_ref[...], b_ref[...]).astype(o_ref.dtype)


def _matmul(a, b, out_dtype):
    m, k = a.shape
    n = b.shape[1]
    return pl.pallas_call(
        _mm_kernel,
        grid=(m // MM_TM, n // MM_TN),
        in_specs=[pl.BlockSpec((MM_TM, k), lambda i, j: (i, 0)),
                  pl.BlockSpec((k, MM_TN), lambda i, j: (0, j))],
        out_specs=pl.BlockSpec((MM_TM, MM_TN), lambda i, j: (i, j)),
        out_shape=jax.ShapeDtypeStruct((m, n), out_dtype),
        compiler_params=_cparams("parallel", "parallel"),
        name="in_proj",
    )(a, b)


def _cmp_kv_kernel(k_ref, pos_ref, w_ref, o_ref, kf_ref, *, seq):
    kf_ref[0:seq, :] = k_ref[...].astype(F32)
    kf_ref[seq:seq + LANES, :] = jnp.zeros((LANES, HD_A), F32)
    acc = jnp.zeros((LANES, HD_A), F32)
    for l in range(L_CMP):
        rows = kf_ref[pl.ds(l, LANES, stride=STRIDE_CMP), :] + pos_ref[0, 0, l:l + 1, :]
        acc = acc + _dot(rows.astype(BF16), w_ref[0, 0, l].astype(BF16))
    o_ref[0, 0] = acc.astype(o_ref.dtype)


def _cmp_kv(z, cmp_pos, cmp_w, layer, batch, seq):
    blk0 = OFF_KVA // LANES
    return pl.pallas_call(
        functools.partial(_cmp_kv_kernel, seq=seq),
        grid=(batch, 2 * KV_A),
        in_specs=[pl.BlockSpec((seq, LANES), lambda b, j: (b, blk0 + j)),
                  pl.BlockSpec((1, 1, L_CMP, HD_A), lambda b, j: (layer, j // KV_A, 0, 0)),
                  pl.BlockSpec((1, 1, L_CMP, HD_A, HD_A), lambda b, j: (layer, j // KV_A, 0, 0, 0))],
        out_specs=pl.BlockSpec((1, 1, LANES, HD_A), lambda b, j: (b, j, 0, 0)),
        out_shape=jax.ShapeDtypeStruct((batch, 2 * KV_A, LANES, HD_A), BF16),
        scratch_shapes=[pltpu.VMEM((seq + LANES, HD_A), F32)],
        compiler_params=_cparams("parallel", "parallel"),
        name="nsa_cmp_kv",
    )(z, cmp_pos, cmp_w)


def _nsa_cmp_kernel(q_ref, kvc_ref, gate_ref, inter_t_ref, o_ref, nsel_ref, *, tq):
    i = pl.program_id(1)
    shape = (tq, LANES)
    t = i * tq + _iota(shape, 0)
    lane = _iota(shape, 1)
    dist_i = t - (lane * STRIDE_CMP + (L_CMP - 1))
    ok = dist_i >= 0
    okf = ok.astype(F32)
    dist = dist_i.astype(F32)
    sig = jax.nn.sigmoid(gate_ref[...].astype(F32))
    scale = HD_A ** -0.5
    n_sel_rows = 32
    inter_t = inter_t_ref[...]
    shape_t = (n_sel_rows, tq)
    j = _iota(shape_t, 0)
    blk_t = (i * tq + _iota(shape_t, 1)) // L_SEL
    valid = j <= blk_t
    forced = (j == 0) | (j == blk_t) | (j == blk_t - 1)
    for g in range(KV_A):
        kc = kvc_ref[0, g]
        vc = kvc_ref[0, KV_A + g]
        psum = jnp.zeros(shape, F32)
        outs = []
        for h in range(HPG_A):
            head = g * HPG_A + h
            q = q_ref[:, head * HD_A:(head + 1) * HD_A]
            s = _dot_nt(q, kc) * scale - (2.0 ** (-8.0 * (head + 1) / H_A)) * dist
            s = jnp.where(ok, s, NEG_INF)
            e = jnp.exp(s - jnp.max(s, axis=-1, keepdims=True))
            p = e / jnp.sum(e, axis=-1, keepdims=True) * okf
            psum = psum + p
            o = _dot(p.astype(BF16), vc)
            outs.append(o * sig[:, head * 3:head * 3 + 1])
        o_ref[:, g * HPG_A * HD_A:(g + 1) * HPG_A * HD_A] = jnp.concatenate(outs, axis=1).astype(o_ref.dtype)

        imp_t = sum(_dot_nt(inter_t, part) for part in _split3(psum))[0:n_sel_rows]
        vals = jnp.where(forced, jnp.inf, jnp.where(valid, imp_t, -jnp.inf))
        sel = _topk_rows(vals, N_SEL)
        not_selected = jnp.where(valid & (sel > 0.5), 0.0, 1.0)
        nsel_ref[0, g] = _rows_to_lanes(not_selected, tq).astype(nsel_ref.dtype)


def _nsa_cmp(z, kvc, inter_t, batch, seq):
    tq = ATT_TQ
    nq = seq // tq
    gate_blk = OFF_GATE_A // LANES
    return pl.pallas_call(
        functools.partial(_nsa_cmp_kernel, tq=tq),
        grid=(batch, nq),
        in_specs=[pl.BlockSpec((tq, Q_A_W), lambda b, i: (b * nq + i, OFF_QA // Q_A_W)),
                  pl.BlockSpec((1, 2 * KV_A, LANES, HD_A), lambda b, i: (b, 0, 0, 0)),
                  pl.BlockSpec((tq, LANES), lambda b, i: (b * nq + i, gate_blk)),
                  pl.BlockSpec((LANES, LANES), lambda b, i: (0, 0))],
        out_specs=[pl.BlockSpec((tq, Q_A_W), lambda b, i: (b * nq + i, 0)),
                   pl.BlockSpec((1, KV_A, tq, LANES), lambda b, i: (b, 0, i, 0))],
        out_shape=[jax.ShapeDtypeStruct((batch * seq, Q_A_W), BF16),
                   jax.ShapeDtypeStruct((batch, KV_A, seq, LANES), BF16)],
        compiler_params=_cparams("parallel", "parallel"),
        name="nsa_cmp_attn",
    )(z, kvc, z, inter_t)


def _scaled_q(q_ref, h, hd, scale):
    return (q_ref[:, h * hd:(h + 1) * hd].astype(F32) * scale).astype(BF16)


def _gated_heads(o, sig, g, branch, tq):
    outs = [o[h * tq:(h + 1) * tq] * _pick_lane(sig, (g * HPG_A + h) * 3 + branch) for h in range(HPG_A)]
    return jnp.concatenate(outs, axis=1)


def _nsa_slc_kernel(slopes_ref, cnt_ref, lst_ref, q_ref, k_ref, v_ref, kaug_ref, causal_ref, nsel_ref, gate_ref,
                    o_ref, *, tq):
    g = pl.program_id(1)
    i = pl.program_id(2)
    nq = pl.num_programs(2)
    plan = (pl.program_id(0) * KV_A + g) * nq + i
    rows = HPG_A * tq
    scale = HD_A ** -0.5
    nsel = nsel_ref[0, 0].astype(F32)
    center = (i * tq).astype(F32)
    qx = jnp.concatenate(
        [jnp.concatenate([_scaled_q(q_ref, h, HD_A, scale),
                          _query_aug(nsel, slopes_ref[g * HPG_A + h], center)], axis=1)
         for h in range(HPG_A)], axis=0)

    def scores(kt):
        k0 = pl.multiple_of(kt * tq, tq)
        kx = jnp.concatenate([k_ref[pl.ds(k0, tq), :], kaug_ref[pl.ds(k0, tq), :]], axis=1)
        return _dot_nt(qx, kx)

    def values(kt):
        return v_ref[pl.ds(pl.multiple_of(kt * tq, tq), tq), :]

    def body(n, carry):
        kt = lst_ref[plan * nq + n]
        return _softmax_update(scores(kt), values(kt), carry)

    m, l, acc = lax.fori_loop(0, cnt_ref[plan], body, _softmax_init(rows, HD_A))
    causal_bias = jnp.concatenate([causal_ref[...]] * HPG_A, axis=0)
    _, l, acc = _softmax_update(scores(i) + causal_bias, values(i), (m, l, acc))
    sig = jax.nn.sigmoid(gate_ref[...].astype(F32))
    o_ref[...] = _gated_heads(acc / l, sig, g, 1, tq).astype(o_ref.dtype)


def _slc_tile_plan(nsel, tq):
    b, g, s, _ = nsel.shape
    nq = s // tq
    per = tq // L_SEL
    chosen = (nsel[..., :nq * per].astype(F32) < 0.5).reshape(b, g, nq, tq, nq, per)
    need = jnp.any(chosen, axis=(3, 5))
    tile = jnp.arange(nq, dtype=jnp.int32)
    need = need & (tile[None, :] < tile[:, None])
    pos = jnp.cumsum(need.astype(jnp.int32), axis=-1) - 1
    hit = need[..., None] & (pos[..., None] == tile)
    lst = jnp.sum(jnp.where(hit, tile[:, None], 0), axis=-2)
    cnt = jnp.sum(need.astype(jnp.int32), axis=-1)
    return cnt.reshape(-1), lst.reshape(-1).astype(jnp.int32)


def _nsa_slc(z, nsel, kaug, slopes, batch, seq):
    tq = ATT_TQ
    nq = seq // tq
    cnt, lst = _slc_tile_plan(nsel, tq)
    qw = HPG_A * HD_A
    kblk = OFF_KVA // LANES + 1 * 2 * KV_A
    vblk = kblk + KV_A
    gate_blk = OFF_GATE_A // LANES
    causal = jnp.asarray(np.where(np.arange(tq)[:, None] >= np.arange(tq)[None, :], 0.0, -BIG).astype(np.float32))
    return pl.pallas_call(
        functools.partial(_nsa_slc_kernel, tq=tq),
        grid=(batch, KV_A, nq),
        in_specs=[pl.BlockSpec(memory_space=pltpu.SMEM),
                  pl.BlockSpec(memory_space=pltpu.SMEM),
                  pl.BlockSpec(memory_space=pltpu.SMEM),
                  pl.BlockSpec((tq, qw), lambda b, g, i: (b * nq + i, OFF_QA // qw + g)),
                  pl.BlockSpec((seq, LANES), lambda b, g, i: (b, kblk + g)),
                  pl.BlockSpec((seq, LANES), lambda b, g, i: (b, vblk + g)),
                  pl.BlockSpec((seq, LANES), lambda b, g, i: (0, 0)),
                  pl.BlockSpec((tq, tq), lambda b, g, i: (0, 0)),
                  pl.BlockSpec((1, 1, tq, LANES), lambda b, g, i: (b, g, i, 0)),
                  pl.BlockSpec((tq, LANES), lambda b, g, i: (b * nq + i, gate_blk))],
        out_specs=pl.BlockSpec((tq, qw), lambda b, g, i: (b * nq + i, g)),
        out_shape=jax.ShapeDtypeStruct((batch * seq, Q_A_W), BF16),
        compiler_params=_cparams("parallel", "parallel", "parallel"),
        name="nsa_slc_attn",
    )(slopes, cnt, lst, z, z, z, kaug, causal, nsel, z)


def _nsa_win_kernel(slopes_ref, q_ref, k_ref, v_ref, kaug_ref, band_ref, gate_ref, o_ref, kx_ref, vx_ref, *, tq, window, seq):
    g = pl.program_id(1)
    i = pl.program_id(2)

    @pl.when(i == 0)
    def _():
        kx_ref[0:window, 0:HD_A] = jnp.zeros((window, HD_A), BF16)
        kx_ref[window:window + seq, 0:HD_A] = k_ref[...]
        kx_ref[:, HD_A:2 * HD_A] = kaug_ref[...]
        vx_ref[0:window, :] = jnp.zeros((window, HD_A), BF16)
        vx_ref[window:window + seq, :] = v_ref[...]

    span = window + tq
    r0 = pl.multiple_of(i * tq, tq)
    center = (i * tq + window).astype(F32)
    base = jnp.zeros((tq, LANES), F32)
    sig = jax.nn.sigmoid(gate_ref[...].astype(F32))
    pair = 2
    band_bias = jnp.concatenate([band_ref[...]] * pair, axis=0)
    for h0 in range(0, HPG_A, pair):
        qx = jnp.concatenate(
            [jnp.concatenate([_scaled_q(q_ref, h, HD_A, HD_A ** -0.5),
                              _query_aug(base, slopes_ref[g * HPG_A + h], center, pad_flag=True)], axis=1)
             for h in range(h0, h0 + pair)], axis=0)
        s = _dot_nt(qx, kx_ref[pl.ds(r0, span), :]) + band_bias
        p = jnp.exp(s - jnp.max(s, axis=-1, keepdims=True))
        o = _dot(p.astype(BF16), vx_ref[pl.ds(r0, span), :]) / jnp.sum(p, axis=-1, keepdims=True)
        for h in range(h0, h0 + pair):
            gated = o[(h - h0) * tq:(h - h0 + 1) * tq] * _pick_lane(sig, (g * HPG_A + h) * 3 + 2)
            o_ref[:, h * HD_A:(h + 1) * HD_A] = gated.astype(o_ref.dtype)


def _band_bias(tq, window):
    r = np.arange(tq)[:, None]
    c = np.arange(window + tq)[None, :]
    return jnp.asarray(np.where((c > r) & (c <= r + window), 0.0, -BIG).astype(np.float32))


def _nsa_win(z, kaug_pad, slopes, batch, seq):
    tq = ATT_TQ
    nq = seq // tq
    qw = HPG_A * HD_A
    kblk = OFF_KVA // LANES + 2 * 2 * KV_A
    vblk = kblk + KV_A
    gate_blk = OFF_GATE_A // LANES
    return pl.pallas_call(
        functools.partial(_nsa_win_kernel, tq=tq, window=WIN_A, seq=seq),
        grid=(batch, KV_A, nq),
        in_specs=[pl.BlockSpec(memory_space=pltpu.SMEM),
                  pl.BlockSpec((tq, qw), lambda b, g, i: (b * nq + i, OFF_QA // qw + g)),
                  pl.BlockSpec((seq, LANES), lambda b, g, i: (b, kblk + g)),
                  pl.BlockSpec((seq, LANES), lambda b, g, i: (b, vblk + g)),
                  pl.BlockSpec((WIN_A + seq, LANES), lambda b, g, i: (0, 0)),
                  pl.BlockSpec((tq, WIN_A + tq), lambda b, g, i: (0, 0)),
                  pl.BlockSpec((tq, LANES), lambda b, g, i: (b * nq + i, gate_blk))],
        out_specs=pl.BlockSpec((tq, qw), lambda b, g, i: (b * nq + i, g)),
        out_shape=jax.ShapeDtypeStruct((batch * seq, Q_A_W), BF16),
        scratch_shapes=[pltpu.VMEM((WIN_A + seq, 2 * HD_A), BF16), pltpu.VMEM((WIN_A + seq, HD_A), BF16)],
        compiler_params=_cparams("parallel", "parallel", "arbitrary"),
        name="nsa_win_attn",
    )(slopes, z, z, z, kaug_pad, _band_bias(tq, WIN_A), z)


SW_POS_HI, SW_POS_LO, SW_ONE, SW_PAD = 0, 1, 2, 3


def _swa_kernel(sink_ref, q_ref, k_ref, v_ref, kaug_ref, band_ref, o_ref, kx_ref, vx_ref, *, tq, window, seq):
    i = pl.program_id(1)

    @pl.when(i == 0)
    def _():
        for g in range(KV_B):
            kx_ref[g, 0:window, 0:HD_B] = jnp.zeros((window, HD_B), BF16)
            kx_ref[g, window:window + seq, 0:HD_B] = k_ref[:, g * HD_B:(g + 1) * HD_B]
            kx_ref[g, :, HD_B:2 * HD_B] = kaug_ref[...]
            vx_ref[g, 0:window, :] = jnp.zeros((window, HD_B), BF16)
            vx_ref[g, window:window + seq, :] = v_ref[:, g * HD_B:(g + 1) * HD_B]

    span = window + tq
    rows = HPG_B * tq
    r0 = pl.multiple_of(i * tq, tq)
    center = (i * tq + window).astype(F32)
    band_bias = jnp.concatenate([band_ref[...]] * HPG_B, axis=0)
    lane = _iota((tq, HD_B), 1)
    hh = _iota((rows, 1), 0) // tq
    row_in_tile = (_iota((rows, 1), 0) - hh * tq).astype(F32)

    def head_column(values):
        col = jnp.full((rows, 1), values[HPG_B - 1], F32)
        for h in range(HPG_B - 2, -1, -1):
            col = jnp.where(hh == h, values[h], col)
        return col

    outs = []
    for g in range(KV_B):
        slopes = [2.0 ** (-8.0 * (g * HPG_B + h + 1) / H_B) for h in range(HPG_B)]
        parts = []
        for h in range(HPG_B):
            aug = jnp.where(lane == SW_POS_HI, slopes[h] * POS_SPLIT, jnp.where(lane == SW_POS_LO, slopes[h], 0.0))
            aug = jnp.where(lane == SW_ONE, -slopes[h] * center, jnp.where(lane == SW_PAD, 1.0, aug))
            parts.append(jnp.concatenate([_scaled_q(q_ref, g * HPG_B + h, HD_B, HD_B ** -0.5),
                                          aug.astype(BF16)], axis=1))
        qx = jnp.concatenate(parts, axis=0)
        s = _dot_nt(qx, kx_ref[g, pl.ds(r0, span), :]) + band_bias
        sink_shifted = (head_column([sink_ref[g * HPG_B + h] for h in range(HPG_B)])
                        + head_column(slopes) * row_in_tile)
        m = jnp.maximum(jnp.max(s, axis=-1, keepdims=True), sink_shifted)
        p = jnp.exp(s - m)
        denom = jnp.sum(p, axis=-1, keepdims=True) + jnp.exp(sink_shifted - m)
        o = _dot(p.astype(BF16), vx_ref[g, pl.ds(r0, span), :]) / denom
        outs += [o[h * tq:(h + 1) * tq] for h in range(HPG_B)]
    o_ref[...] = jnp.concatenate(outs, axis=1).astype(o_ref.dtype)


def _swa(z, kaug_sw, sink, layer, batch, seq):
    tq = SWA_TQ
    nq = seq // tq
    return pl.pallas_call(
        functools.partial(_swa_kernel, tq=tq, window=WIN_B, seq=seq),
        grid=(batch, nq),
        in_specs=[pl.BlockSpec(memory_space=pltpu.SMEM),
                  pl.BlockSpec((tq, Q_B_W), lambda b, i: (b * nq + i, OFF_QB // Q_B_W)),
                  pl.BlockSpec((seq, KV_B_W), lambda b, i: (b, OFF_KB // KV_B_W)),
                  pl.BlockSpec((seq, KV_B_W), lambda b, i: (b, OFF_VB // KV_B_W)),
                  pl.BlockSpec((WIN_B + seq, HD_B), lambda b, i: (0, 0)),
                  pl.BlockSpec((tq, WIN_B + tq), lambda b, i: (0, 0))],
        out_specs=pl.BlockSpec((tq, Q_B_W), lambda b, i: (b * nq + i, 0)),
        out_shape=jax.ShapeDtypeStruct((batch * seq, Q_B_W), BF16),
        scratch_shapes=[pltpu.VMEM((KV_B, WIN_B + seq, 2 * HD_B), BF16),
                        pltpu.VMEM((KV_B, WIN_B + seq, HD_B), BF16)],
        compiler_params=_cparams("parallel", "arbitrary"),
        name="swa_attn",
    )(sink[layer], z, z, z, kaug_sw, _band_bias(tq, WIN_B))


def _moba_kernel(slopes_ref, q_ref, k_ref, v_ref, kaug_ref, avg_ref, o_ref, km_ref, *, seq):
    i = pl.program_id(1)
    tq = BLK_C
    n_blk = seq // BLK_C
    scale = HD_C ** -0.5

    @pl.when(i == 0)
    def _():
        for h in range(H_C):
            k_mean = _dot(avg_ref[...], k_ref[:, h * HD_C:(h + 1) * HD_C])[0:SUBLANES]
            terms = [t.astype(F32) for t in _split3(k_mean)] + [jnp.zeros((SUBLANES, HD_C), F32)]
            km_ref[h] = jnp.concatenate(terms, axis=0).astype(BF16)

    center = (i * tq).astype(F32)
    blk = _iota((SUBLANES, tq), 0)
    past = blk < i
    qx = []
    for h in range(H_C):
        q = q_ref[:, h * HD_C:(h + 1) * HD_C]
        sc = _dot_nt(km_ref[h], q)
        score_t = sc[0:SUBLANES] + sc[SUBLANES:2 * SUBLANES] + sc[2 * SUBLANES:3 * SUBLANES]
        sel = _topk_rows(jnp.where(past, score_t, -jnp.inf), TOPK_C)
        not_selected = jnp.where(past & (sel < 0.5), 1.0, 0.0)
        qx.append(jnp.concatenate([_scaled_q(q_ref, h, HD_C, scale),
                                   _query_aug(_rows_to_lanes(not_selected, tq), slopes_ref[h], center)], axis=1))
    assert n_blk <= SUBLANES

    def scores(h, kt):
        k0 = pl.multiple_of(kt * tq, tq)
        kx = jnp.concatenate([k_ref[pl.ds(k0, tq), h * HD_C:(h + 1) * HD_C], kaug_ref[pl.ds(k0, tq), :]], axis=1)
        return _dot_nt(qx[h], kx)

    def values(h, kt):
        return v_ref[pl.ds(pl.multiple_of(kt * tq, tq), tq), h * HD_C:(h + 1) * HD_C]

    def body(kt, carry):
        out = []
        for h in range(H_C):
            s, m, l, acc = carry[h]
            s_next = scores(h, kt + 1)
            out.append((s_next,) + _softmax_update(s, values(h, kt), (m, l, acc)))
        return tuple(out)

    init = tuple((scores(h, 0),) + _softmax_init(tq, HD_C) for h in range(H_C))
    carry = lax.fori_loop(0, i, body, init)
    causal = _iota((tq, tq), 0) >= _iota((tq, tq), 1)
    outs = []
    for h in range(H_C):
        s, m, l, acc = carry[h]
        _, l, acc = _softmax_update(jnp.where(causal, s, -BIG), values(h, i), (m, l, acc))
        outs.append(acc / l)
    o_ref[...] = jnp.concatenate(outs, axis=1).astype(o_ref.dtype)


def _moba(z, kaug, avg, slopes, batch, seq):
    nq = seq // BLK_C
    return pl.pallas_call(
        functools.partial(_moba_kernel, seq=seq),
        grid=(batch, nq),
        in_specs=[pl.BlockSpec(memory_space=pltpu.SMEM),
                  pl.BlockSpec((BLK_C, C_W), lambda b, i: (b * nq + i, OFF_QC // C_W)),
                  pl.BlockSpec((seq, C_W), lambda b, i: (b, OFF_KC // C_W)),
                  pl.BlockSpec((seq, C_W), lambda b, i: (b, OFF_VC // C_W)),
                  pl.BlockSpec((seq, LANES), lambda b, i: (0, 0)),
                  pl.BlockSpec((2 * SUBLANES, seq), lambda b, i: (0, 0))],
        out_specs=pl.BlockSpec((BLK_C, C_W), lambda b, i: (b * nq + i, 0)),
        out_shape=jax.ShapeDtypeStruct((batch * seq, C_W), BF16),
        scratch_shapes=[pltpu.VMEM((H_C, 4 * SUBLANES, HD_C), BF16)],
        compiler_params=_cparams("parallel", "arbitrary"),
        name="moba_attn",
    )(slopes, z, z, z, kaug, avg)


def _merge_kernel(oc_ref, os_ref, ow_ref, ob_ref, om_ref, g0_ref, g1_ref, g2_ref, wa_ref, wb_ref, wc_ref, o_ref):
    o_a = (oc_ref[...].astype(F32) + os_ref[...].astype(F32) + ow_ref[...].astype(F32)).astype(BF16)
    merged = jax.nn.sigmoid(g0_ref[...].astype(F32)) * _dot(o_a, wa_ref[...])
    merged = merged + jax.nn.sigmoid(g1_ref[...].astype(F32)) * _dot(ob_ref[...], wb_ref[...])
    merged = merged + jax.nn.sigmoid(g2_ref[...].astype(F32)) * _dot(om_ref[...], wc_ref[...])
    o_ref[...] = merged.astype(o_ref.dtype)


def _merge(o_cmp, o_slc, o_win, o_b, o_c, z, wa, wb, wc):
    tm = 2 * ROW_TM
    t = z.shape[0]
    gblk = OFF_MERGE // D_MODEL
    row = lambda w: pl.BlockSpec((tm, w), lambda i: (i, 0))
    full = lambda a: pl.BlockSpec(a.shape, lambda i: (0, 0))
    gate = lambda r: pl.BlockSpec((tm, D_MODEL), lambda i: (i, gblk + r))
    return pl.pallas_call(
        _merge_kernel,
        grid=(t // tm,),
        in_specs=[row(Q_A_W), row(Q_A_W), row(Q_A_W), row(Q_B_W), row(C_W),
                  gate(0), gate(1), gate(2), full(wa), full(wb), full(wc)],
        out_specs=row(D_MODEL),
        out_shape=jax.ShapeDtypeStruct((t, D_MODEL), BF16),
        compiler_params=_cparams("parallel"),
        name="mixer_merge",
    )(o_cmp, o_slc, o_win, o_b, o_c, z, z, z, wa, wb, wc)


def _layer_norm(h, g_ref, b_ref):
    mu = jnp.mean(h, axis=-1, keepdims=True)
    xc = h - mu
    var = jnp.mean(xc * xc, axis=-1, keepdims=True)
    return xc * lax.rsqrt(var + LN_EPS) * g_ref[...] + b_ref[...]


def _proj_ln_kernel(m_ref, w_ref, x_ref, g_ref, b_ref, xo_ref, xb_ref):
    y = _dot(m_ref[...], w_ref[...])
    out = _layer_norm(ALPHA * x_ref[...] + y, g_ref, b_ref)
    xo_ref[...] = out
    xb_ref[...] = out.astype(BF16)


def _proj_ln(merged, w_out, x, g, b):
    tm = 2 * ROW_TM
    t = x.shape[0]
    row = pl.BlockSpec((tm, D_MODEL), lambda i: (i, 0))
    vec = pl.BlockSpec((1, D_MODEL), lambda i: (0, 0))
    return pl.pallas_call(
        _proj_ln_kernel,
        grid=(t // tm,),
        in_specs=[row, pl.BlockSpec((D_MODEL, D_MODEL), lambda i: (0, 0)), row, vec, vec],
        out_specs=[row, row],
        out_shape=[jax.ShapeDtypeStruct((t, D_MODEL), F32), jax.ShapeDtypeStruct((t, D_MODEL), BF16)],
        compiler_params=_cparams("parallel"),
        name="out_proj_ln",
    )(merged, w_out, x, g, b)


def _moe_ln_kernel(x_ref, ya_ref, yb_ref, w_ref, g_ref, b_ref, *rest):
    xo_ref, xb_ref = rest[-2:]
    w = w_ref[...]
    y = w[:, 0:1] * ya_ref[...].astype(F32) + w[:, 1:2] * yb_ref[...].astype(F32)
    out = _layer_norm(ALPHA * x_ref[...] + y, g_ref, b_ref)
    xo_ref[...] = out
    xb_ref[...] = out.astype(BF16)


def _moe_ln(x, y_groups, wts, g, b):
    tm = ROW_TM
    t = x.shape[0]
    blocks = t // tm // len(y_groups)
    vec = pl.BlockSpec((1, D_MODEL), lambda i: (0, 0))
    group_row = pl.BlockSpec((tm, D_MODEL), lambda i: (i, 0))
    outs = None
    for n, (ya, yb) in enumerate(y_groups):
        off = n * blocks
        row = pl.BlockSpec((tm, D_MODEL), lambda i, off=off: (off + i, 0))
        in_specs = [row, group_row, group_row, pl.BlockSpec((tm, LANES), lambda i, off=off: (off + i, 0)), vec, vec]
        args = [x, ya, yb, wts, g, b]
        aliases = {}
        if outs is not None:
            in_specs += [pl.BlockSpec(memory_space=pl.ANY)] * 2
            aliases = {len(args): 0, len(args) + 1: 1}
            args += list(outs)
        outs = pl.pallas_call(
            _moe_ln_kernel,
            grid=(blocks,),
            in_specs=in_specs,
            out_specs=[row, row],
            out_shape=[jax.ShapeDtypeStruct((t, D_MODEL), F32), jax.ShapeDtypeStruct((t, D_MODEL), BF16)],
            input_output_aliases=aliases,
            compiler_params=_cparams("parallel"),
            name="moe_combine_ln",
        )(*args)
    return outs


def _router_kernel(x_ref, w_ref, b_ref, id_ref, wt_ref):
    x = x_ref[...]
    x_hi = x.astype(BF16)
    x_lo = (x - x_hi.astype(F32)).astype(BF16)
    w = w_ref[...]
    w_hi = w.astype(BF16)
    w_lo = (w - w_hi.astype(F32)).astype(BF16)
    logits = _dot(x_hi, w_hi) + _dot(x_lo, w_hi) + _dot(x_hi, w_lo) + b_ref[...]
    lane = _iota(logits.shape, 1).astype(F32)
    first = lambda hit: jnp.min(jnp.where(hit, lane, float(LANES)), axis=-1, keepdims=True)
    gl = jnp.where(lane < N_GROUPS, logits, -jnp.inf)
    gm = jnp.max(gl, axis=-1, keepdims=True)
    g_w = 1.0 / jnp.sum(jnp.exp(gl - gm), axis=-1, keepdims=True)
    lo = N_GROUPS + first(gl == gm) * EXP_PER_GROUP
    el = jnp.where((lane >= lo) & (lane < lo + EXP_PER_GROUP), logits, -jnp.inf)
    m1 = jnp.max(el, axis=-1, keepdims=True)
    i1 = first(el == m1)
    el2 = jnp.where(lane == i1, -jnp.inf, el)
    m2 = jnp.max(el2, axis=-1, keepdims=True)
    i2 = first(el2 == m2)
    e2 = jnp.exp(m2 - m1)
    w1 = g_w / (1.0 + e2)
    w2 = g_w * e2 / (1.0 + e2)
    ids = jnp.where(lane == 0.0, i1 - N_GROUPS, jnp.where(lane == 1.0, i2 - N_GROUPS, 0.0))
    id_ref[...] = ids.astype(jnp.int32)
    wt_ref[...] = jnp.where(lane == 0.0, w1, jnp.where(lane == 1.0, w2, 0.0))


def _router(x, w_rt, b_rt):
    tm = ROUTE_TM
    t = x.shape[0]
    return pl.pallas_call(
        _router_kernel,
        grid=(t // tm,),
        in_specs=[pl.BlockSpec((tm, D_MODEL), lambda i: (i, 0)),
                  pl.BlockSpec((D_MODEL, LANES), lambda i: (0, 0)),
                  pl.BlockSpec((1, LANES), lambda i: (0, 0))],
        out_specs=[pl.BlockSpec((tm, LANES), lambda i: (i, 0)), pl.BlockSpec((tm, LANES), lambda i: (i, 0))],
        out_shape=[jax.ShapeDtypeStruct((t, LANES), jnp.int32), jax.ShapeDtypeStruct((t, LANES), F32)],
        compiler_params=_cparams("parallel"),
        name="moe_router",
    )(x, w_rt, b_rt)


def _expert_kernel(te_ref, na_ref, x_ref, wg_ref, wu_ref, wd_ref, *rest, tile0):
    y_ref = rest[-1]
    j = tile0 + pl.program_id(0)

    @pl.when(j < na_ref[0])
    def _():
        x = x_ref[...]
        hg = _dot(x, wg_ref[0, 0].astype(BF16))
        hu = _dot(x, wu_ref[0, 0].astype(BF16))
        a = hg * jax.nn.sigmoid(hg) * hu
        y_ref[...] = _dot(a.astype(BF16), wd_ref[0, 0].astype(BF16)).astype(y_ref.dtype)

    @pl.when(j >= na_ref[0])
    def _():
        y_ref[...] = jnp.zeros(y_ref.shape, y_ref.dtype)


def _experts(x_chunks, tile_expert, n_active, wg, wu, wd, layer):
    tm = EXP_TM
    tiles_per_chunk = x_chunks[0].shape[0] // tm
    n_tiles = len(x_chunks) * tiles_per_chunk
    y = None
    for c, x_c in enumerate(x_chunks):
        tile0 = c * tiles_per_chunk
        x_map = lambda j, te, na, tile0=tile0: (
            jnp.clip(jnp.minimum(tile0 + j, na[0] - 1) - tile0, 0, tiles_per_chunk - 1), 0)
        w_map = lambda j, te, na, tile0=tile0: (layer, te[tile0 + j], 0, 0)
        in_specs = [pl.BlockSpec((tm, D_MODEL), x_map),
                    pl.BlockSpec((1, 1, D_MODEL, D_EXPERT), w_map),
                    pl.BlockSpec((1, 1, D_MODEL, D_EXPERT), w_map),
                    pl.BlockSpec((1, 1, D_EXPERT, D_MODEL), w_map)]
        args = [tile_expert, n_active, x_c, wg, wu, wd]
        aliases = {}
        if y is not None:
            in_specs.append(pl.BlockSpec(memory_space=pl.ANY))
            aliases = {len(args): 0}
            args.append(y)
        grid_spec = pltpu.PrefetchScalarGridSpec(
            num_scalar_prefetch=2,
            grid=(tiles_per_chunk,),
            in_specs=in_specs,
            out_specs=pl.BlockSpec((tm, D_MODEL), lambda j, te, na, tile0=tile0: (tile0 + j, 0)),
        )
        y = pl.pallas_call(
            functools.partial(_expert_kernel, tile0=tile0),
            grid_spec=grid_spec,
            out_shape=jax.ShapeDtypeStruct((n_tiles * tm, D_MODEL), BF16),
            input_output_aliases=aliases,
            compiler_params=_cparams("arbitrary"),
            name="moe_experts",
        )(*args)
    return y


def _dispatch_plan(expert_ids, tm, n_tiles):
    t = expert_ids.shape[0]
    e_flat = expert_ids.reshape(-1)
    onehot = (e_flat[:, None] == jnp.arange(N_EXPERTS, dtype=jnp.int32)[None, :]).astype(jnp.int32)
    csum = jnp.cumsum(onehot, axis=0)
    rank = jnp.sum(csum * onehot, axis=1) - 1
    counts = csum[-1]
    padded = ((counts + tm - 1) // tm) * tm
    ends = jnp.cumsum(padded)
    dest = jnp.sum((ends - padded)[None, :] * onehot, axis=1) + rank
    filler = jnp.arange(n_tiles * tm, dtype=jnp.int32) % t
    scatter_add = lambda base, idx, vals: base.at[idx].add(vals, mode="promise_in_bounds", unique_indices=True)
    src_tok = compute_on("tpu_sparsecore")(jax.jit(scatter_add))(
        filler, dest, jnp.arange(2 * t, dtype=jnp.int32) // 2 - dest % t)
    n_active = ends[-1] // tm
    tile_start = jnp.minimum(jnp.arange(n_tiles, dtype=jnp.int32), n_active - 1) * tm
    te = jnp.sum((ends[None, :] <= tile_start[:, None]).astype(jnp.int32), axis=1)
    te = jnp.minimum(te, N_EXPERTS - 1)
    return dest.reshape(t, 2), src_tok, te, n_active.reshape(1).astype(jnp.int32)


def _bf16_const(a):
    a16 = a.astype(BF16)
    assert np.all(a16.astype(np.float32) == a)
    return jnp.asarray(a16)


def _alibi_np(n):
    slopes = np.asarray([2.0 ** (-8.0 * (i + 1) / n) for i in range(n)], np.float32)
    _bf16_const(slopes)
    return slopes


def _selection_overlap_t(seq):
    n_cmp = (seq - L_CMP) // STRIDE_CMP + 1
    n_sel = seq // L_SEL
    c_start = STRIDE_CMP * np.arange(n_cmp)
    s_start = L_SEL * np.arange(n_sel)
    inter = np.clip(np.minimum(c_start[:, None] + L_CMP, s_start[None, :] + L_SEL)
                    - np.maximum(c_start[:, None], s_start[None, :]), 0, None) / L_CMP
    out = np.zeros((LANES, LANES), np.float32)
    out[:n_sel, :n_cmp] = inter.T
    return _bf16_const(out)


def _key_aug(seq, block, pad_rows=0, width=LANES, lanes=(AUG_POS_HI, AUG_POS_LO, AUG_ONE, AUG_PAD)):
    rows = pad_rows + seq
    pos = np.arange(rows)
    key = pos - pad_rows
    real = key >= 0
    out = np.zeros((rows, width), np.float32)
    if block:
        out[pos[real], key[real] // block] = -BIG
    out[:, lanes[0]] = pos // POS_SPLIT
    out[:, lanes[1]] = pos % POS_SPLIT
    out[:, lanes[2]] = 1.0
    out[~real, lanes[3]] = -BIG
    return _bf16_const(out)


def _block_average(seq):
    out = np.zeros((2 * SUBLANES, seq), np.float32)
    for n in range(seq // BLK_C):
        out[n, n * BLK_C:(n + 1) * BLK_C] = 1.0 / BLK_C
    return _bf16_const(out)


def kernel(x, w_in, nsa_cmp_pos, nsa_cmp_w, sink_b, w_br_a, w_br_b, w_br_c, w_out, ln1_g, ln1_b,
           w_group, b_group, w_router, b_router, w_gate, w_up, w_down, ln2_g, ln2_b):
    batch, seq, d = x.shape
    t = batch * seq
    assert d == D_MODEL and w_in.shape[2] == D_IN and seq % BLK_C == 0 and t % MM_TM == 0
    assert seq // L_SEL <= AUG_POS_HI and (seq - L_CMP) // STRIDE_CMP + 1 < LANES
    assert seq // BLK_C <= SUBLANES and (WIN_A + seq) // POS_SPLIT < 256
    n_tiles = (TOPK_EXPERT * t) // EXP_TM + N_EXPERTS
    assert n_tiles % EXP_CHUNKS == 0 and (t // ROW_TM) % COMBINE_SPLIT == 0 and OFF_QA % Q_A_W == 0

    slopes_a = jnp.asarray(_alibi_np(H_A))
    slopes_c = jnp.asarray(_alibi_np(H_C))
    _alibi_np(H_B)
    inter_t = _selection_overlap_t(seq)
    kaug_slc = _key_aug(seq, L_SEL)
    kaug_win = _key_aug(seq, 0, pad_rows=WIN_A)
    kaug_swa = _key_aug(seq, 0, pad_rows=WIN_B, width=HD_B, lanes=(SW_POS_HI, SW_POS_LO, SW_ONE, SW_PAD))
    kaug_moba = _key_aug(seq, BLK_C)
    avg = _block_average(seq)
    take = lambda a, idx: a.at[idx].get(mode="promise_in_bounds")
    sc_take = compute_on("tpu_sparsecore")(jax.jit(take))

    w_in_t = jnp.swapaxes(w_in, 1, 2)
    xf = x.reshape(t, d)
    xb = xf.astype(BF16)
    for l in range(DEPTH):
        z = _matmul(xb, _prep_w_in(w_in_t, l), BF16)
        kvc = _cmp_kv(z, nsa_cmp_pos, nsa_cmp_w, l, batch, seq)
        o_cmp, nsel = _nsa_cmp(z, kvc, inter_t, batch, seq)
        o_slc = _nsa_slc(z, nsel, kaug_slc, slopes_a, batch, seq)
        o_win = _nsa_win(z, kaug_win, slopes_a, batch, seq)
        o_b = _swa(z, kaug_swa, sink_b, l, batch, seq)
        o_c = _moba(z, kaug_moba, avg, slopes_c, batch, seq)
        merged = _merge(o_cmp, o_slc, o_win, o_b, o_c, z,
                        w_br_a[l].astype(BF16), w_br_b[l].astype(BF16), w_br_c[l].astype(BF16))
        xf, xb = _proj_ln(merged, w_out[l].astype(BF16), xf, ln1_g[l][None, :], ln1_b[l][None, :])

        w_rt = jnp.concatenate([w_group[l], w_router[l],
                                jnp.zeros((d, LANES - N_GROUPS - N_EXPERTS), F32)], axis=1)
        b_rt = jnp.concatenate([b_group[l], b_router[l],
                                jnp.zeros((LANES - N_GROUPS - N_EXPERTS,), F32)])[None, :]
        ids, wts = _router(xf, w_rt, b_rt)
        dest, src_tok, tile_expert, n_active = _dispatch_plan(ids[:, :TOPK_EXPERT], EXP_TM, n_tiles)
        x_chunks = [sc_take(xb, c) for c in jnp.split(src_tok, EXP_CHUNKS)]
        y = _experts(x_chunks, tile_expert, n_active, w_gate, w_up, w_down, l)
        y_groups = [(take(y, d[:, 0]), take(y, d[:, 1])) for d in jnp.split(dest, COMBINE_SPLIT)]
        xf, xb = _moe_ln(xf, y_groups, wts, ln2_g[l][None, :], ln2_b[l][None, :])
    return xf.reshape(batch, seq, d)
```

```python
import functools

import numpy as np
import jax
import jax.numpy as jnp
from jax import lax
from jax.experimental import pallas as pl
from jax.experimental.pallas import tpu as pltpu
from jax.experimental.compute_on import compute_on

F32 = jnp.float32
BF16 = jnp.bfloat16

D_MODEL = 2048
DEPTH = 2
H_A, KV_A, HD_A = 8, 2, 128
HPG_A = H_A // KV_A
L_CMP, STRIDE_CMP, L_SEL, N_SEL, WIN_A = 32, 16, 64, 8, 512
H_B, KV_B, HD_B, WIN_B = 8, 2, 64, 128
HPG_B = H_B // KV_B
H_C, HD_C, BLK_C, TOPK_C = 4, 128, 256, 3
N_GROUPS, EXP_PER_GROUP, D_EXPERT, TOPK_EXPERT = 4, 8, 256, 2
N_EXPERTS = N_GROUPS * EXP_PER_GROUP
LN_EPS = 1e-5
NEG_INF = -1e30
ALPHA = (2.0 * DEPTH) ** 0.25

LANES = 128
SUBLANES = 8
VMEM_LIMIT = 48 * 1024 * 1024
BIG = 2.0 ** 100

Q_A_W = H_A * HD_A
KV_A_W = 3 * 2 * KV_A * HD_A
GATE_A_W = 3 * H_A
Q_B_W = H_B * HD_B
KV_B_W = KV_B * HD_B
C_W = H_C * HD_C
MERGE_W = 3 * D_MODEL
D_IN = Q_A_W + KV_A_W + GATE_A_W + Q_B_W + 2 * KV_B_W + 3 * C_W + MERGE_W
OFF_MERGE = 0
OFF_QA = OFF_MERGE + MERGE_W
OFF_QB = OFF_QA + Q_A_W
OFF_QC = OFF_QB + Q_B_W
OFF_KC = OFF_QC + C_W
OFF_VC = OFF_KC + C_W
OFF_KVA = OFF_VC + C_W
OFF_KB = OFF_KVA + KV_A_W
OFF_VB = OFF_KB + KV_B_W
OFF_GATE_A = OFF_VB + KV_B_W
GATE_A_PAD = 256
Z_W = OFF_GATE_A + GATE_A_PAD
assert OFF_QA % (HPG_A * HD_A) == 0 and OFF_QB % Q_B_W == 0 and OFF_MERGE % D_MODEL == 0
assert OFF_QC % C_W == 0 and OFF_KC % C_W == 0 and OFF_VC % C_W == 0

AUG_POS_HI, AUG_POS_LO, AUG_ONE, AUG_PAD = 32, 33, 34, 35
POS_SPLIT = 256

RELAYOUT_COLS = 256
MM_TM, MM_TN = 2048, 1024
ROW_TM = 256
ROUTE_TM = 512
EXP_TM = 256
EXP_CHUNKS = 3
ATT_TQ = 256
SWA_TQ = 128
COMBINE_SPLIT = 2


def _cparams(*sem):
    return pltpu.CompilerParams(dimension_semantics=sem, vmem_limit_bytes=VMEM_LIMIT)


def _dot(a, b):
    return jnp.dot(a, b, preferred_element_type=F32)


def _dot_nt(a, b):
    return lax.dot_general(a, b, (((1,), (1,)), ((), ())), preferred_element_type=F32)


def _iota(shape, dim):
    return lax.broadcasted_iota(jnp.int32, shape, dim)


def _split3(x):
    hi = x.astype(BF16)
    r1 = x - hi.astype(F32)
    mid = r1.astype(BF16)
    lo = (r1 - mid.astype(F32)).astype(BF16)
    return hi, mid, lo


def _pick_lane(x, lane_idx):
    lane = _iota(x.shape, 1)
    return jnp.sum(jnp.where(lane == lane_idx, x, 0.0), axis=-1, keepdims=True)


def _topk_rows(vals, k):
    row = _iota(vals.shape, 0).astype(F32)
    sel = jnp.zeros(vals.shape, F32)
    for _ in range(k):
        m = jnp.max(vals, axis=0, keepdims=True)
        idx = jnp.min(jnp.where(vals == m, row, float(LANES)), axis=0, keepdims=True)
        pick = row == idx
        sel = jnp.where(pick, 1.0, sel)
        vals = jnp.where(pick, -jnp.inf, vals)
    return sel


def _rows_to_lanes(x_t, tq):
    pad = jnp.zeros((LANES - x_t.shape[0], tq), F32)
    return jnp.concatenate([x_t, pad], axis=0).T


def _query_aug(base, slope, center, pad_flag=False):
    lane = _iota(base.shape, 1)
    aug = jnp.where(lane == AUG_POS_HI, slope * float(POS_SPLIT), base)
    aug = jnp.where(lane == AUG_POS_LO, slope, aug)
    aug = jnp.where(lane == AUG_ONE, -slope * center, aug)
    if pad_flag:
        aug = jnp.where(lane == AUG_PAD, 1.0, aug)
    return aug.astype(BF16)


def _softmax_update(s, v, carry):
    m, l, acc = carry
    m_new = jnp.maximum(m, jnp.max(s, axis=-1, keepdims=True))
    alpha = jnp.exp(m - m_new)
    p = jnp.exp(s - m_new)
    l = alpha * l + jnp.sum(p, axis=-1, keepdims=True)
    acc = alpha * acc + _dot(p.astype(BF16), v)
    return m_new, l, acc


def _softmax_init(rows, hd):
    return (jnp.full((rows, 1), NEG_INF, F32), jnp.zeros((rows, 1), F32), jnp.zeros((rows, hd), F32))


def _z_column_sources():
    s_kva = Q_A_W
    s_gate = s_kva + KV_A_W
    s_qb = s_gate + GATE_A_W
    s_kb = s_qb + Q_B_W
    s_qc = s_kb + 2 * KV_B_W
    s_merge = s_qc + 3 * C_W
    assert GATE_A_PAD == RELAYOUT_COLS
    segments = [(OFF_MERGE, s_merge, MERGE_W), (OFF_QA, 0, Q_A_W), (OFF_QB, s_qb, Q_B_W),
                (OFF_QC, s_qc, 3 * C_W), (OFF_KVA, s_kva, KV_A_W), (OFF_KB, s_kb, 2 * KV_B_W)]
    src = np.zeros((Z_W // RELAYOUT_COLS,), np.int32)
    valid = np.zeros((Z_W // RELAYOUT_COLS,), np.int32)
    for dst, start, width in segments:
        assert dst % RELAYOUT_COLS == 0 and width % RELAYOUT_COLS == 0
        for b in range(width // RELAYOUT_COLS):
            src[dst // RELAYOUT_COLS + b] = start + b * RELAYOUT_COLS
            valid[dst // RELAYOUT_COLS + b] = RELAYOUT_COLS
    src[OFF_GATE_A // RELAYOUT_COLS] = s_gate
    valid[OFF_GATE_A // RELAYOUT_COLS] = GATE_A_W
    assert np.all(src % SUBLANES == 0) and np.all(src + RELAYOUT_COLS <= D_IN)
    return jnp.asarray(src // SUBLANES), jnp.asarray(valid)


def _prep_w_in_kernel(src_ref, valid_ref, w_ref, o_ref):
    j = pl.program_id(0)
    w = w_ref[0]
    w = jnp.where(_iota(w.shape, 0) < valid_ref[j], w, 0.0)
    o_ref[...] = w.T.astype(BF16)


def _prep_w_in(w_in_t, layer):
    src, valid = _z_column_sources()
    grid_spec = pltpu.PrefetchScalarGridSpec(
        num_scalar_prefetch=2,
        grid=(Z_W // RELAYOUT_COLS,),
        in_specs=[pl.BlockSpec((pl.Element(1), pl.Element(RELAYOUT_COLS), pl.Element(D_MODEL)),
                               lambda j, src, valid: (layer, src[j] * SUBLANES, 0))],
        out_specs=pl.BlockSpec((D_MODEL, RELAYOUT_COLS), lambda j, src, valid: (0, j)),
    )
    return pl.pallas_call(
        _prep_w_in_kernel,
        grid_spec=grid_spec,
        out_shape=jax.ShapeDtypeStruct((D_MODEL, Z_W), BF16),
        compiler_params=_cparams("parallel"),
        name="w_in_relayout",
    )(src, valid, w_in_t)


def _mm_kernel(a_ref, b_ref, o_ref):
    o_ref[...] = _dot(a_ref[...], b_ref[...]).astype(o_ref.dtype)


def _matmul(a, b, out_dtype):
    m, k = a.shape
    n = b.shape[1]
    return pl.pallas_call(
        _mm_kernel,
        grid=(m // MM_TM, n // MM_TN),
        in_specs=[pl.BlockSpec((MM_TM, k), lambda i, j: (i, 0)),
                  pl.BlockSpec((k, MM_TN), lambda i, j: (0, j))],
        out_specs=pl.BlockSpec((MM_TM, MM_TN), lambda i, j: (i, j)),
        out_shape=jax.ShapeDtypeStruct((m, n), out_dtype),
        compiler_params=_cparams("parallel", "parallel"),
        name="in_proj",
    )(a, b)


def _cmp_kv_kernel(k_ref, pos_ref, w_ref, o_ref, kf_ref, *, seq):
    kf_ref[0:seq, :] = k_ref[...].astype(F32)
    kf_ref[seq:seq + LANES, :] = jnp.zeros((LANES, HD_A), F32)
    acc = jnp.zeros((LANES, HD_A), F32)
    for l in range(L_CMP):
        rows = kf_ref[pl.ds(l, LANES, stride=STRIDE_CMP), :] + pos_ref[0, 0, l:l + 1, :]
        acc = acc + _dot(rows.astype(BF16), w_ref[0, 0, l].astype(BF16))
    o_ref[0, 0] = acc.astype(o_ref.dtype)


def _cmp_kv(z, cmp_pos, cmp_w, layer, batch, seq):
    blk0 = OFF_KVA // LANES
    return pl.pallas_call(
        functools.partial(_cmp_kv_kernel, seq=seq),
        grid=(batch, 2 * KV_A),
        in_specs=[pl.BlockSpec((seq, LANES), lambda b, j: (b, blk0 + j)),
                  pl.BlockSpec((1, 1, L_CMP, HD_A), lambda b, j: (layer, j // KV_A, 0, 0)),
                  pl.BlockSpec((1, 1, L_CMP, HD_A, HD_A), lambda b, j: (layer, j // KV_A, 0, 0, 0))],
        out_specs=pl.BlockSpec((1, 1, LANES, HD_A), lambda b, j: (b, j, 0, 0)),
        out_shape=jax.ShapeDtypeStruct((batch, 2 * KV_A, LANES, HD_A), BF16),
        scratch_shapes=[pltpu.VMEM((seq + LANES, HD_A), F32)],
        compiler_params=_cparams("parallel", "parallel"),
        name="nsa_cmp_kv",
    )(z, cmp_pos, cmp_w)


def _nsa_cmp_kernel(q_ref, kvc_ref, gate_ref, inter_t_ref, o_ref, nsel_ref, *, tq):
    i = pl.program_id(1)
    shape = (tq, LANES)
    t = i * tq + _iota(shape, 0)
    lane = _iota(shape, 1)
    dist_i = t - (lane * STRIDE_CMP + (L_CMP - 1))
    ok = dist_i >= 0
    okf = ok.astype(F32)
    dist = dist_i.astype(F32)
    sig = jax.nn.sigmoid(gate_ref[...].astype(F32))
    scale = HD_A ** -0.5
    n_sel_rows = 32
    inter_t = inter_t_ref[...]
    shape_t = (n_sel_rows, tq)
    j = _iota(shape_t, 0)
    blk_t = (i * tq + _iota(shape_t, 1)) // L_SEL
    valid = j <= blk_t
    forced = (j == 0) | (j == blk_t) | (j == blk_t - 1)
    for g in range(KV_A):
        kc = kvc_ref[0, g]
        vc = kvc_ref[0, KV_A + g]
        psum = jnp.zeros(shape, F32)
        outs = []
        for h in range(HPG_A):
            head = g * HPG_A + h
            q = q_ref[:, head * HD_A:(head + 1) * HD_A]
            s = _dot_nt(q, kc) * scale - (2.0 ** (-8.0 * (head + 1) / H_A)) * dist
            s = jnp.where(ok, s, NEG_INF)
            e = jnp.exp(s - jnp.max(s, axis=-1, keepdims=True))
            p = e / jnp.sum(e, axis=-1, keepdims=True) * okf
            psum = psum + p
            o = _dot(p.astype(BF16), vc)
            outs.append(o * sig[:, head * 3:head * 3 + 1])
        o_ref[:, g * HPG_A * HD_A:(g + 1) * HPG_A * HD_A] = jnp.concatenate(outs, axis=1).astype(o_ref.dtype)

        imp_t = sum(_dot_nt(inter_t, part) for part in _split3(psum))[0:n_sel_rows]
        vals = jnp.where(forced, jnp.inf, jnp.where(valid, imp_t, -jnp.inf))
        sel = _topk_rows(vals, N_SEL)
        not_selected = jnp.where(valid & (sel > 0.5), 0.0, 1.0)
        nsel_ref[0, g] = _rows_to_lanes(not_selected, tq).astype(nsel_ref.dtype)


def _nsa_cmp(z, kvc, inter_t, batch, seq):
    tq = ATT_TQ
    nq = seq // tq
    gate_blk = OFF_GATE_A // LANES
    return pl.pallas_call(
        functools.partial(_nsa_cmp_kernel, tq=tq),
        grid=(batch, nq),
        in_specs=[pl.BlockSpec((tq, Q_A_W), lambda b, i: (b * nq + i, OFF_QA // Q_A_W)),
                  pl.BlockSpec((1, 2 * KV_A, LANES, HD_A), lambda b, i: (b, 0, 0, 0)),
                  pl.BlockSpec((tq, LANES), lambda b, i: (b * nq + i, gate_blk)),
                  pl.BlockSpec((LANES, LANES), lambda b, i: (0, 0))],
        out_specs=[pl.BlockSpec((tq, Q_A_W), lambda b, i: (b * nq + i, 0)),
                   pl.BlockSpec((1, KV_A, tq, LANES), lambda b, i: (b, 0, i, 0))],
        out_shape=[jax.ShapeDtypeStruct((batch * seq, Q_A_W), BF16),
                   jax.ShapeDtypeStruct((batch, KV_A, seq, LANES), BF16)],
        compiler_params=_cparams("parallel", "parallel"),
        name="nsa_cmp_attn",
    )(z, kvc, z, inter_t)


def _scaled_q(q_ref, h, hd, scale):
    return (q_ref[:, h * hd:(h + 1) * hd].astype(F32) * scale).astype(BF16)


def _gated_heads(o, sig, g, branch, tq):
    outs = [o[h * tq:(h + 1) * tq] * _pick_lane(sig, (g * HPG_A + h) * 3 + branch) for h in range(HPG_A)]
    return jnp.concatenate(outs, axis=1)


def _nsa_slc_kernel(slopes_ref, cnt_ref, lst_ref, q_ref, k_ref, v_ref, kaug_ref, causal_ref, nsel_ref, gate_ref,
                    o_ref, *, tq):
    g = pl.program_id(1)
    i = pl.program_id(2)
    nq = pl.num_programs(2)
    plan = (pl.program_id(0) * KV_A + g) * nq + i
    rows = HPG_A * tq
    scale = HD_A ** -0.5
    nsel = nsel_ref[0, 0].astype(F32)
    center = (i * tq).astype(F32)
    qx = jnp.concatenate(
        [jnp.concatenate([_scaled_q(q_ref, h, HD_A, scale),
                          _query_aug(nsel, slopes_ref[g * HPG_A + h], center)], axis=1)
         for h in range(HPG_A)], axis=0)

    def scores(kt):
        k0 = pl.multiple_of(kt * tq, tq)
        kx = jnp.concatenate([k_ref[pl.ds(k0, tq), :], kaug_ref[pl.ds(k0, tq), :]], axis=1)
        return _dot_nt(qx, kx)

    def values(kt):
        return v_ref[pl.ds(pl.multiple_of(kt * tq, tq), tq), :]

    def body(n, carry):
        kt = lst_ref[plan * nq + n]
        return _softmax_update(scores(kt), values(kt), carry)

    m, l, acc = lax.fori_loop(0, cnt_ref[plan], body, _softmax_init(rows, HD_A))
    causal_bias = jnp.concatenate([causal_ref[...]] * HPG_A, axis=0)
    _, l, acc = _softmax_update(scores(i) + causal_bias, values(i), (m, l, acc))
    sig = jax.nn.sigmoid(gate_ref[...].astype(F32))
    o_ref[...] = _gated_heads(acc / l, sig, g, 1, tq).astype(o_ref.dtype)


def _slc_tile_plan(nsel, tq):
    b, g, s, _ = nsel.shape
    nq = s // tq
    per = tq // L_SEL
    chosen = (nsel[..., :nq * per].astype(F32) < 0.5).reshape(b, g, nq, tq, nq, per)
    need = jnp.any(chosen, axis=(3, 5))
    tile = jnp.arange(nq, dtype=jnp.int32)
    need = need & (tile[None, :] < tile[:, None])
    pos = jnp.cumsum(need.astype(jnp.int32), axis=-1) - 1
    hit = need[..., None] & (pos[..., None] == tile)
    lst = jnp.sum(jnp.where(hit, tile[:, None], 0), axis=-2)
    cnt = jnp.sum(need.astype(jnp.int32), axis=-1)
    return cnt.reshape(-1), lst.reshape(-1).astype(jnp.int32)


def _nsa_slc(z, nsel, kaug, slopes, batch, seq):
    tq = ATT_TQ
    nq = seq // tq
    cnt, lst = _slc_tile_plan(nsel, tq)
    qw = HPG_A * HD_A
    kblk = OFF_KVA // LANES + 1 * 2 * KV_A
    vblk = kblk + KV_A
    gate_blk = OFF_GATE_A // LANES
    causal = jnp.asarray(np.where(np.arange(tq)[:, None] >= np.arange(tq)[None, :], 0.0, -BIG).astype(np.float32))
    return pl.pallas_call(
        functools.partial(_nsa_slc_kernel, tq=tq),
        grid=(batch, KV_A, nq),
        in_specs=[pl.BlockSpec(memory_space=pltpu.SMEM),
                  pl.BlockSpec(memory_space=pltpu.SMEM),
                  pl.BlockSpec(memory_space=pltpu.SMEM),
                  pl.BlockSpec((tq, qw), lambda b, g, i: (b * nq + i, OFF_QA // qw + g)),
                  pl.BlockSpec((seq, LANES), lambda b, g, i: (b, kblk + g)),
                  pl.BlockSpec((seq, LANES), lambda b, g, i: (b, vblk + g)),
                  pl.BlockSpec((seq, LANES), lambda b, g, i: (0, 0)),
                  pl.BlockSpec((tq, tq), lambda b, g, i: (0, 0)),
                  pl.BlockSpec((1, 1, tq, LANES), lambda b, g, i: (b, g, i, 0)),
                  pl.BlockSpec((tq, LANES), lambda b, g, i: (b * nq + i, gate_blk))],
        out_specs=pl.BlockSpec((tq, qw), lambda b, g, i: (b * nq + i, g)),
        out_shape=jax.ShapeDtypeStruct((batch * seq, Q_A_W), BF16),
        compiler_params=_cparams("parallel", "parallel", "parallel"),
        name="nsa_slc_attn",
    )(slopes, cnt, lst, z, z, z, kaug, causal, nsel, z)


def _nsa_win_kernel(slopes_ref, q_ref, k_ref, v_ref, kaug_ref, band_ref, gate_ref, o_ref, kx_ref, vx_ref, *, tq, window, seq):
    g = pl.program_id(1)
    i = pl.program_id(2)

    @pl.when(i == 0)
    def _():
        kx_ref[0:window, 0:HD_A] = jnp.zeros((window, HD_A), BF16)
        kx_ref[window:window + seq, 0:HD_A] = k_ref[...]
        kx_ref[:, HD_A:2 * HD_A] = kaug_ref[...]
        vx_ref[0:window, :] = jnp.zeros((window, HD_A), BF16)
        vx_ref[window:window + seq, :] = v_ref[...]

    span = window + tq
    r0 = pl.multiple_of(i * tq, tq)
    center = (i * tq + window).astype(F32)
    base = jnp.zeros((tq, LANES), F32)
    sig = jax.nn.sigmoid(gate_ref[...].astype(F32))
    pair = 2
    band_bias = jnp.concatenate([band_ref[...]] * pair, axis=0)
    for h0 in range(0, HPG_A, pair):
        qx = jnp.concatenate(
            [jnp.concatenate([_scaled_q(q_ref, h, HD_A, HD_A ** -0.5),
                              _query_aug(base, slopes_ref[g * HPG_A + h], center, pad_flag=True)], axis=1)
             for h in range(h0, h0 + pair)], axis=0)
        s = _dot_nt(qx, kx_ref[pl.ds(r0, span), :]) + band_bias
        p = jnp.exp(s - jnp.max(s, axis=-1, keepdims=True))
        o = _dot(p.astype(BF16), vx_ref[pl.ds(r0, span), :]) / jnp.sum(p, axis=-1, keepdims=True)
        for h in range(h0, h0 + pair):
            gated = o[(h - h0) * tq:(h - h0 + 1) * tq] * _pick_lane(sig, (g * HPG_A + h) * 3 + 2)
            o_ref[:, h * HD_A:(h + 1) * HD_A] = gated.astype(o_ref.dtype)


def _band_bias(tq, window):
    r = np.arange(tq)[:, None]
    c = np.arange(window + tq)[None, :]
    return jnp.asarray(np.where((c > r) & (c <= r + window), 0.0, -BIG).astype(np.float32))


def _nsa_win(z, kaug_pad, slopes, batch, seq):
    tq = ATT_TQ
    nq = seq // tq
    qw = HPG_A * HD_A
    kblk = OFF_KVA // LANES + 2 * 2 * KV_A
    vblk = kblk + KV_A
    gate_blk = OFF_GATE_A // LANES
    return pl.pallas_call(
        functools.partial(_nsa_win_kernel, tq=tq, window=WIN_A, seq=seq),
        grid=(batch, KV_A, nq),
        in_specs=[pl.BlockSpec(memory_space=pltpu.SMEM),
                  pl.BlockSpec((tq, qw), lambda b, g, i: (b * nq + i, OFF_QA // qw + g)),
                  pl.BlockSpec((seq, LANES), lambda b, g, i: (b, kblk + g)),
                  pl.BlockSpec((seq, LANES), lambda b, g, i: (b, vblk + g)),
                  pl.BlockSpec((WIN_A + seq, LANES), lambda b, g, i: (0, 0)),
                  pl.BlockSpec((tq, WIN_A + tq), lambda b, g, i: (0, 0)),
                  pl.BlockSpec((tq, LANES), lambda b, g, i: (b * nq + i, gate_blk))],
        out_specs=pl.BlockSpec((tq, qw), lambda b, g, i: (b * nq + i, g)),
        out_shape=jax.ShapeDtypeStruct((batch * seq, Q_A_W), BF16),
        scratch_shapes=[pltpu.VMEM((WIN_A + seq, 2 * HD_A), BF16), pltpu.VMEM((WIN_A + seq, HD_A), BF16)],
        compiler_params=_cparams("parallel", "parallel", "arbitrary"),
        name="nsa_win_attn",
    )(slopes, z, z, z, kaug_pad, _band_bias(tq, WIN_A), z)


SW_POS_HI, SW_POS_LO, SW_ONE, SW_PAD = 0, 1, 2, 3


def _swa_kernel(sink_ref, q_ref, k_ref, v_ref, kaug_ref, band_ref, o_ref, kx_ref, vx_ref, *, tq, window, seq):
    i = pl.program_id(1)

    @pl.when(i == 0)
    def _():
        for g in range(KV_B):
            kx_ref[g, 0:window, 0:HD_B] = jnp.zeros((window, HD_B), BF16)
            kx_ref[g, window:window + seq, 0:HD_B] = k_ref[:, g * HD_B:(g + 1) * HD_B]
            kx_ref[g, :, HD_B:2 * HD_B] = kaug_ref[...]
            vx_ref[g, 0:window, :] = jnp.zeros((window, HD_B), BF16)
            vx_ref[g, window:window + seq, :] = v_ref[:, g * HD_B:(g + 1) * HD_B]

    span = window + tq
    rows = HPG_B * tq
    r0 = pl.multiple_of(i * tq, tq)
    center = (i * tq + window).astype(F32)
    band_bias = jnp.concatenate([band_ref[...]] * HPG_B, axis=0)
    lane = _iota((tq, HD_B), 1)
    hh = _iota((rows, 1), 0) // tq
    row_in_tile = (_iota((rows, 1), 0) - hh * tq).astype(F32)

    def head_column(values):
        col = jnp.full((rows, 1), values[HPG_B - 1], F32)
        for h in range(HPG_B - 2, -1, -1):
            col = jnp.where(hh == h, values[h], col)
        return col

    outs = []
    for g in range(KV_B):
        slopes = [2.0 ** (-8.0 * (g * HPG_B + h + 1) / H_B) for h in range(HPG_B)]
        parts = []
        for h in range(HPG_B):
            aug = jnp.where(lane == SW_POS_HI, slopes[h] * POS_SPLIT, jnp.where(lane == SW_POS_LO, slopes[h], 0.0))
            aug = jnp.where(lane == SW_ONE, -slopes[h] * center, jnp.where(lane == SW_PAD, 1.0, aug))
            parts.append(jnp.concatenate([_scaled_q(q_ref, g * HPG_B + h, HD_B, HD_B ** -0.5),
                                          aug.astype(BF16)], axis=1))
        qx = jnp.concatenate(parts, axis=0)
        s = _dot_nt(qx, kx_ref[g, pl.ds(r0, span), :]) + band_bias
        sink_shifted = (head_column([sink_ref[g * HPG_B + h] for h in range(HPG_B)])
                        + head_column(slopes) * row_in_tile)
        m = jnp.maximum(jnp.max(s, axis=-1, keepdims=True), sink_shifted)
        p = jnp.exp(s - m)
        denom = jnp.sum(p, axis=-1, keepdims=True) + jnp.exp(sink_shifted - m)
        o = _dot(p.astype(BF16), vx_ref[g, pl.ds(r0, span), :]) / denom
        outs += [o[h * tq:(h + 1) * tq] for h in range(HPG_B)]
    o_ref[...] = jnp.concatenate(outs, axis=1).astype(o_ref.dtype)


def _swa(z, kaug_sw, sink, layer, batch, seq):
    tq = SWA_TQ
    nq = seq // tq
    return pl.pallas_call(
        functools.partial(_swa_kernel, tq=tq, window=WIN_B, seq=seq),
        grid=(batch, nq),
        in_specs=[pl.BlockSpec(memory_space=pltpu.SMEM),
                  pl.BlockSpec((tq, Q_B_W), lambda b, i: (b * nq + i, OFF_QB // Q_B_W)),
                  pl.BlockSpec((seq, KV_B_W), lambda b, i: (b, OFF_KB // KV_B_W)),
                  pl.BlockSpec((seq, KV_B_W), lambda b, i: (b, OFF_VB // KV_B_W)),
                  pl.BlockSpec((WIN_B + seq, HD_B), lambda b, i: (0, 0)),
                  pl.BlockSpec((tq, WIN_B + tq), lambda b, i: (0, 0))],
        out_specs=pl.BlockSpec((tq, Q_B_W), lambda b, i: (b * nq + i, 0)),
        out_shape=jax.ShapeDtypeStruct((batch * seq, Q_B_W), BF16),
        scratch_shapes=[pltpu.VMEM((KV_B, WIN_B + seq, 2 * HD_B), BF16),
                        pltpu.VMEM((KV_B, WIN_B + seq, HD_B), BF16)],
        compiler_params=_cparams("parallel", "arbitrary"),
        name="swa_attn",
    )(sink[layer], z, z, z, kaug_sw, _band_bias(tq, WIN_B))


def _moba_kernel(slopes_ref, q_ref, k_ref, v_ref, kaug_ref, avg_ref, o_ref, km_ref, *, seq):
    i = pl.program_id(1)
    tq = BLK_C
    n_blk = seq // BLK_C
    scale = HD_C ** -0.5

    @pl.when(i == 0)
    def _():
        for h in range(H_C):
            k_mean = _dot(avg_ref[...], k_ref[:, h * HD_C:(h + 1) * HD_C])[0:SUBLANES]
            terms = [t.astype(F32) for t in _split3(k_mean)] + [jnp.zeros((SUBLANES, HD_C), F32)]
            km_ref[h] = jnp.concatenate(terms, axis=0).astype(BF16)

    center = (i * tq).astype(F32)
    blk = _iota((SUBLANES, tq), 0)
    past = blk < i
    qx = []
    for h in range(H_C):
        q = q_ref[:, h * HD_C:(h + 1) * HD_C]
        sc = _dot_nt(km_ref[h], q)
        score_t = sc[0:SUBLANES] + sc[SUBLANES:2 * SUBLANES] + sc[2 * SUBLANES:3 * SUBLANES]
        sel = _topk_rows(jnp.where(past, score_t, -jnp.inf), TOPK_C)
        not_selected = jnp.where(past & (sel < 0.5), 1.0, 0.0)
        qx.append(jnp.concatenate([_scaled_q(q_ref, h, HD_C, scale),
                                   _query_aug(_rows_to_lanes(not_selected, tq), slopes_ref[h], center)], axis=1))
    assert n_blk <= SUBLANES

    def scores(h, kt):
        k0 = pl.multiple_of(kt * tq, tq)
        kx = jnp.concatenate([k_ref[pl.ds(k0, tq), h * HD_C:(h + 1) * HD_C], kaug_ref[pl.ds(k0, tq), :]], axis=1)
        return _dot_nt(qx[h], kx)

    def values(h, kt):
        return v_ref[pl.ds(pl.multiple_of(kt * tq, tq), tq), h * HD_C:(h + 1) * HD_C]

    def body(kt, carry):
        out = []
        for h in range(H_C):
            s, m, l, acc = carry[h]
            s_next = scores(h, kt + 1)
            out.append((s_next,) + _softmax_update(s, values(h, kt), (m, l, acc)))
        return tuple(out)

    init = tuple((scores(h, 0),) + _softmax_init(tq, HD_C) for h in range(H_C))
    carry = lax.fori_loop(0, i, body, init)
    causal = _iota((tq, tq), 0) >= _iota((tq, tq), 1)
    outs = []
    for h in range(H_C):
        s, m, l, acc = carry[h]
        _, l, acc = _softmax_update(jnp.where(causal, s, -BIG), values(h, i), (m, l, acc))
        outs.append(acc / l)
    o_ref[...] = jnp.concatenate(outs, axis=1).astype(o_ref.dtype)


def _moba(z, kaug, avg, slopes, batch, seq):
    nq = seq // BLK_C
    return pl.pallas_call(
        functools.partial(_moba_kernel, seq=seq),
        grid=(batch, nq),
        in_specs=[pl.BlockSpec(memory_space=pltpu.SMEM),
                  pl.BlockSpec((BLK_C, C_W), lambda b, i: (b * nq + i, OFF_QC // C_W)),
                  pl.BlockSpec((seq, C_W), lambda b, i: (b, OFF_KC // C_W)),
                  pl.BlockSpec((seq, C_W), lambda b, i: (b, OFF_VC // C_W)),
                  pl.BlockSpec((seq, LANES), lambda b, i: (0, 0)),
                  pl.BlockSpec((2 * SUBLANES, seq), lambda b, i: (0, 0))],
        out_specs=pl.BlockSpec((BLK_C, C_W), lambda b, i: (b * nq + i, 0)),
        out_shape=jax.ShapeDtypeStruct((batch * seq, C_W), BF16),
        scratch_shapes=[pltpu.VMEM((H_C, 4 * SUBLANES, HD_C), BF16)],
        compiler_params=_cparams("parallel", "arbitrary"),
        name="moba_attn",
    )(slopes, z, z, z, kaug, avg)


def _merge_kernel(oc_ref, os_ref, ow_ref, ob_ref, om_ref, g0_ref, g1_ref, g2_ref, wa_ref, wb_ref, wc_ref, o_ref):
    o_a = (oc_ref[...].astype(F32) + os_ref[...].astype(F32) + ow_ref[...].astype(F32)).astype(BF16)
    merged = jax.nn.sigmoid(g0_ref[...].astype(F32)) * _dot(o_a, wa_ref[...])
    merged = merged + jax.nn.sigmoid(g1_ref[...].astype(F32)) * _dot(ob_ref[...], wb_ref[...])
    merged = merged + jax.nn.sigmoid(g2_ref[...].astype(F32)) * _dot(om_ref[...], wc_ref[...])
    o_ref[...] = merged.astype(o_ref.dtype)


def _merge(o_cmp, o_slc, o_win, o_b, o_c, z, wa, wb, wc):
    tm = 2 * ROW_TM
    t = z.shape[0]
    gblk = OFF_MERGE // D_MODEL
    row = lambda w: pl.BlockSpec((tm, w), lambda i: (i, 0))
    full = lambda a: pl.BlockSpec(a.shape, lambda i: (0, 0))
    gate = lambda r: pl.BlockSpec((tm, D_MODEL), lambda i: (i, gblk + r))
    return pl.pallas_call(
        _merge_kernel,
        grid=(t // tm,),
        in_specs=[row(Q_A_W), row(Q_A_W), row(Q_A_W), row(Q_B_W), row(C_W),
                  gate(0), gate(1), gate(2), full(wa), full(wb), full(wc)],
        out_specs=row(D_MODEL),
        out_shape=jax.ShapeDtypeStruct((t, D_MODEL), BF16),
        compiler_params=_cparams("parallel"),
        name="mixer_merge",
    )(o_cmp, o_slc, o_win, o_b, o_c, z, z, z, wa, wb, wc)


def _layer_norm(h, g_ref, b_ref):
    mu = jnp.mean(h, axis=-1, keepdims=True)
    xc = h - mu
    var = jnp.mean(xc * xc, axis=-1, keepdims=True)
    return xc * lax.rsqrt(var + LN_EPS) * g_ref[...] + b_ref[...]


def _proj_ln_kernel(m_ref, w_ref, x_ref, g_ref, b_ref, xo_ref, xb_ref):
    y = _dot(m_ref[...], w_ref[...])
    out = _layer_norm(ALPHA * x_ref[...] + y, g_ref, b_ref)
    xo_ref[...] = out
    xb_ref[...] = out.astype(BF16)


def _proj_ln(merged, w_out, x, g, b):
    tm = 2 * ROW_TM
    t = x.shape[0]
    row = pl.BlockSpec((tm, D_MODEL), lambda i: (i, 0))
    vec = pl.BlockSpec((1, D_MODEL), lambda i: (0, 0))
    return pl.pallas_call(
        _proj_ln_kernel,
        grid=(t // tm,),
        in_specs=[row, pl.BlockSpec((D_MODEL, D_MODEL), lambda i: (0, 0)), row, vec, vec],
        out_specs=[row, row],
        out_shape=[jax.ShapeDtypeStruct((t, D_MODEL), F32), jax.ShapeDtypeStruct((t, D_MODEL), BF16)],
        compiler_params=_cparams("parallel"),
        name="out_proj_ln",
    )(merged, w_out, x, g, b)


def _moe_ln_kernel(x_ref, ya_ref, yb_ref, w_ref, g_ref, b_ref, *rest):
    xo_ref, xb_ref = rest[-2:]
    w = w_ref[...]
    y = w[:, 0:1] * ya_ref[...].astype(F32) + w[:, 1:2] * yb_ref[...].astype(F32)
    out = _layer_norm(ALPHA * x_ref[...] + y, g_ref, b_ref)
    xo_ref[...] = out
    xb_ref[...] = out.astype(BF16)


def _moe_ln(x, y_groups, wts, g, b):
    tm = ROW_TM
    t = x.shape[0]
    blocks = t // tm // len(y_groups)
    vec = pl.BlockSpec((1, D_MODEL), lambda i: (0, 0))
    group_row = pl.BlockSpec((tm, D_MODEL), lambda i: (i, 0))
    outs = None
    for n, (ya, yb) in enumerate(y_groups):
        off = n * blocks
        row = pl.BlockSpec((tm, D_MODEL), lambda i, off=off: (off + i, 0))
        in_specs = [row, group_row, group_row, pl.BlockSpec((tm, LANES), lambda i, off=off: (off + i, 0)), vec, vec]
        args = [x, ya, yb, wts, g, b]
        aliases = {}
        if outs is not None:
            in_specs += [pl.BlockSpec(memory_space=pl.ANY)] * 2
            aliases = {len(args): 0, len(args) + 1: 1}
            args += list(outs)
        outs = pl.pallas_call(
            _moe_ln_kernel,
            grid=(blocks,),
            in_specs=in_specs,
            out_specs=[row, row],
            out_shape=[jax.ShapeDtypeStruct((t, D_MODEL), F32), jax.ShapeDtypeStruct((t, D_MODEL), BF16)],
            input_output_aliases=aliases,
            compiler_params=_cparams("parallel"),
            name="moe_combine_ln",
        )(*args)
    return outs


def _router_kernel(x_ref, w_ref, b_ref, id_ref, wt_ref):
    x = x_ref[...]
    x_hi = x.astype(BF16)
    x_lo = (x - x_hi.astype(F32)).astype(BF16)
    w = w_ref[...]
    w_hi = w.astype(BF16)
    w_lo = (w - w_hi.astype(F32)).astype(BF16)
    logits = _dot(x_hi, w_hi) + _dot(x_lo, w_hi) + _dot(x_hi, w_lo) + b_ref[...]
    lane = _iota(logits.shape, 1).astype(F32)
    first = lambda hit: jnp.min(jnp.where(hit, lane, float(LANES)), axis=-1, keepdims=True)
    gl = jnp.where(lane < N_GROUPS, logits, -jnp.inf)
    gm = jnp.max(gl, axis=-1, keepdims=True)
    g_w = 1.0 / jnp.sum(jnp.exp(gl - gm), axis=-1, keepdims=True)
    lo = N_GROUPS + first(gl == gm) * EXP_PER_GROUP
    el = jnp.where((lane >= lo) & (lane < lo + EXP_PER_GROUP), logits, -jnp.inf)
    m1 = jnp.max(el, axis=-1, keepdims=True)
    i1 = first(el == m1)
    el2 = jnp.where(lane == i1, -jnp.inf, el)
    m2 = jnp.max(el2, axis=-1, keepdims=True)
    i2 = first(el2 == m2)
    e2 = jnp.exp(m2 - m1)
    w1 = g_w / (1.0 + e2)
    w2 = g_w * e2 / (1.0 + e2)
    ids = jnp.where(lane == 0.0, i1 - N_GROUPS, jnp.where(lane == 1.0, i2 - N_GROUPS, 0.0))
    id_ref[...] = ids.astype(jnp.int32)
    wt_ref[...] = jnp.where(lane == 0.0, w1, jnp.where(lane == 1.0, w2, 0.0))


def _router(x, w_rt, b_rt):
    tm = ROUTE_TM
    t = x.shape[0]
    return pl.pallas_call(
        _router_kernel,
        grid=(t // tm,),
        in_specs=[pl.BlockSpec((tm, D_MODEL), lambda i: (i, 0)),
                  pl.BlockSpec((D_MODEL, LANES), lambda i: (0, 0)),
                  pl.BlockSpec((1, LANES), lambda i: (0, 0))],
        out_specs=[pl.BlockSpec((tm, LANES), lambda i: (i, 0)), pl.BlockSpec((tm, LANES), lambda i: (i, 0))],
        out_shape=[jax.ShapeDtypeStruct((t, LANES), jnp.int32), jax.ShapeDtypeStruct((t, LANES), F32)],
        compiler_params=_cparams("parallel"),
        name="moe_router",
    )(x, w_rt, b_rt)


def _expert_kernel(te_ref, na_ref, x_ref, wg_ref, wu_ref, wd_ref, *rest, tile0):
    y_ref = rest[-1]
    j = tile0 + pl.program_id(0)

    @pl.when(j < na_ref[0])
    def _():
        x = x_ref[...]
        hg = _dot(x, wg_ref[0, 0].astype(BF16))
        hu = _dot(x, wu_ref[0, 0].astype(BF16))
        a = hg * jax.nn.sigmoid(hg) * hu
        y_ref[...] = _dot(a.astype(BF16), wd_ref[0, 0].astype(BF16)).astype(y_ref.dtype)

    @pl.when(j >= na_ref[0])
    def _():
        y_ref[...] = jnp.zeros(y_ref.shape, y_ref.dtype)


def _experts(x_chunks, tile_expert, n_active, wg, wu, wd, layer):
    tm = EXP_TM
    tiles_per_chunk = x_chunks[0].shape[0] // tm
    n_tiles = len(x_chunks) * tiles_per_chunk
    y = None
    for c, x_c in enumerate(x_chunks):
        tile0 = c * tiles_per_chunk
        x_map = lambda j, te, na, tile0=tile0: (
            jnp.clip(jnp.minimum(tile0 + j, na[0] - 1) - tile0, 0, tiles_per_chunk - 1), 0)
        w_map = lambda j, te, na, tile0=tile0: (layer, te[tile0 + j], 0, 0)
        in_specs = [pl.BlockSpec((tm, D_MODEL), x_map),
                    pl.BlockSpec((1, 1, D_MODEL, D_EXPERT), w_map),
                    pl.BlockSpec((1, 1, D_MODEL, D_EXPERT), w_map),
                    pl.BlockSpec((1, 1, D_EXPERT, D_MODEL), w_map)]
        args = [tile_expert, n_active, x_c, wg, wu, wd]
        aliases = {}
        if y is not None:
            in_specs.append(pl.BlockSpec(memory_space=pl.ANY))
            aliases = {len(args): 0}
            args.append(y)
        grid_spec = pltpu.PrefetchScalarGridSpec(
            num_scalar_prefetch=2,
            grid=(tiles_per_chunk,),
            in_specs=in_specs,
            out_specs=pl.BlockSpec((tm, D_MODEL), lambda j, te, na, tile0=tile0: (tile0 + j, 0)),
        )
        y = pl.pallas_call(
            functools.partial(_expert_kernel, tile0=tile0),
            grid_spec=grid_spec,
            out_shape=jax.ShapeDtypeStruct((n_tiles * tm, D_MODEL), BF16),
            input_output_aliases=aliases,
            compiler_params=_cparams("arbitrary"),
            name="moe_experts",
        )(*args)
    return y


def _dispatch_plan(expert_ids, tm, n_tiles):
    t = expert_ids.shape[0]
    e_flat = expert_ids.reshape(-1)
    onehot = (e_flat[:, None] == jnp.arange(N_EXPERTS, dtype=jnp.int32)[None, :]).astype(jnp.int32)
    csum = jnp.cumsum(onehot, axis=0)
    rank = jnp.sum(csum * onehot, axis=1) - 1
    counts = csum[-1]
    padded = ((counts + tm - 1) // tm) * tm
    ends = jnp.cumsum(padded)
    dest = jnp.sum((ends - padded)[None, :] * onehot, axis=1) + rank
    filler = jnp.arange(n_tiles * tm, dtype=jnp.int32) % t
    scatter_add = lambda base, idx, vals: base.at[idx].add(vals, mode="promise_in_bounds", unique_indices=True)
    src_tok = compute_on("tpu_sparsecore")(jax.jit(scatter_add))(
        filler, dest, jnp.arange(2 * t, dtype=jnp.int32) // 2 - dest % t)
    n_active = ends[-1] // tm
    tile_start = jnp.minimum(jnp.arange(n_tiles, dtype=jnp.int32), n_active - 1) * tm
    te = jnp.sum((ends[None, :] <= tile_start[:, None]).astype(jnp.int32), axis=1)
    te = jnp.minimum(te, N_EXPERTS - 1)
    return dest.reshape(t, 2), src_tok, te, n_active.reshape(1).astype(jnp.int32)


def _bf16_const(a):
    a16 = a.astype(BF16)
    assert np.all(a16.astype(np.float32) == a)
    return jnp.asarray(a16)


def _alibi_np(n):
    slopes = np.asarray([2.0 ** (-8.0 * (i + 1) / n) for i in range(n)], np.float32)
    _bf16_const(slopes)
    return slopes


def _selection_overlap_t(seq):
    n_cmp = (seq - L_CMP) // STRIDE_CMP + 1
    n_sel = seq // L_SEL
    c_start = STRIDE_CMP * np.arange(n_cmp)
    s_start = L_SEL * np.arange(n_sel)
    inter = np.clip(np.minimum(c_start[:, None] + L_CMP, s_start[None, :] + L_SEL)
                    - np.maximum(c_start[:, None], s_start[None, :]), 0, None) / L_CMP
    out = np.zeros((LANES, LANES), np.float32)
    out[:n_sel, :n_cmp] = inter.T
    return _bf16_const(out)


def _key_aug(seq, block, pad_rows=0, width=LANES, lanes=(AUG_POS_HI, AUG_POS_LO, AUG_ONE, AUG_PAD)):
    rows = pad_rows + seq
    pos = np.arange(rows)
    key = pos - pad_rows
    real = key >= 0
    out = np.zeros((rows, width), np.float32)
    if block:
        out[pos[real], key[real] // block] = -BIG
    out[:, lanes[0]] = pos // POS_SPLIT
    out[:, lanes[1]] = pos % POS_SPLIT
    out[:, lanes[2]] = 1.0
    out[~real, lanes[3]] = -BIG
    return _bf16_const(out)


def _block_average(seq):
    out = np.zeros((2 * SUBLANES, seq), np.float32)
    for n in range(seq // BLK_C):
        out[n, n * BLK_C:(n + 1) * BLK_C] = 1.0 / BLK_C
    return _bf16_const(out)


def kernel(x, w_in, nsa_cmp_pos, nsa_cmp_w, sink_b, w_br_a, w_br_b, w_br_c, w_out, ln1_g, ln1_b,
           w_group, b_group, w_router, b_router, w_gate, w_up, w_down, ln2_g, ln2_b):
    batch, seq, d = x.shape
    t = batch * seq
    assert d == D_MODEL and w_in.shape[2] == D_IN and seq % BLK_C == 0 and t % MM_TM == 0
    assert seq // L_SEL <= AUG_POS_HI and (seq - L_CMP) // STRIDE_CMP + 1 < LANES
    assert seq // BLK_C <= SUBLANES and (WIN_A + seq) // POS_SPLIT < 256
    n_tiles = (TOPK_EXPERT * t) // EXP_TM + N_EXPERTS
    assert n_tiles % EXP_CHUNKS == 0 and (t // ROW_TM) % COMBINE_SPLIT == 0 and OFF_QA % Q_A_W == 0

    slopes_a = jnp.asarray(_alibi_np(H_A))
    slopes_c = jnp.asarray(_alibi_np(H_C))
    _alibi_np(H_B)
    inter_t = _selection_overlap_t(seq)
    kaug_slc = _key_aug(seq, L_SEL)
    kaug_win = _key_aug(seq, 0, pad_rows=WIN_A)
    kaug_swa = _key_aug(seq, 0, pad_rows=WIN_B, width=HD_B, lanes=(SW_POS_HI, SW_POS_LO, SW_ONE, SW_PAD))
    kaug_moba = _key_aug(seq, BLK_C)
    avg = _block_average(seq)
    take = lambda a, idx: a.at[idx].get(mode="promise_in_bounds")
    sc_take = compute_on("tpu_sparsecore")(jax.jit(take))

    w_in_t = jnp.swapaxes(w_in, 1, 2)
    xf = x.reshape(t, d)
    xb = xf.astype(BF16)
    for l in range(DEPTH):
        z = _matmul(xb, _prep_w_in(w_in_t, l), BF16)
        kvc = _cmp_kv(z, nsa_cmp_pos, nsa_cmp_w, l, batch, seq)
        o_cmp, nsel = _nsa_cmp(z, kvc, inter_t, batch, seq)
        o_slc = _nsa_slc(z, nsel, kaug_slc, slopes_a, batch, seq)
        o_win = _nsa_win(z, kaug_win, slopes_a, batch, seq)
        o_b = _swa(z, kaug_swa, sink_b, l, batch, seq)
        o_c = _moba(z, kaug_moba, avg, slopes_c, batch, seq)
        merged = _merge(o_cmp, o_slc, o_win, o_b, o_c, z,
                        w_br_a[l].astype(BF16), w_br_b[l].astype(BF16), w_br_c[l].astype(BF16))
        xf, xb = _proj_ln(merged, w_out[l].astype(BF16), xf, ln1_g[l][None, :], ln1_b[l][None, :])

        w_rt = jnp.concatenate([w_group[l], w_router[l],
                                jnp.zeros((d, LANES - N_GROUPS - N_EXPERTS), F32)], axis=1)
        b_rt = jnp.concatenate([b_group[l], b_router[l],
                                jnp.zeros((LANES - N_GROUPS - N_EXPERTS,), F32)])[None, :]
        ids, wts = _router(xf, w_rt, b_rt)
        dest, src_tok, tile_expert, n_active = _dispatch_plan(ids[:, :TOPK_EXPERT], EXP_TM, n_tiles)
        x_chunks = [sc_take(xb, c) for c in jnp.split(src_tok, EXP_CHUNKS)]
        y = _experts(x_chunks, tile_expert, n_active, w_gate, w_up, w_down, l)
        y_groups = [(take(y, d[:, 0]), take(y, d[:, 1])) for d in jnp.split(dest, COMBINE_SPLIT)]
        xf, xb = _moe_ln(xf, y_groups, wts, ln2_g[l][None, :], ln2_b[l][None, :])
    return xf.reshape(batch, seq, d)
```

```python
import functools

import numpy as np
import jax
import jax.numpy as jnp
from jax import lax
from jax.experimental import pallas as pl
from jax.experimental.pallas import tpu as pltpu
from jax.experimental.compute_on import compute_on

F32 = jnp.float32
BF16 = jnp.bfloat16

D_MODEL = 2048
DEPTH = 2
H_A, KV_A, HD_A = 8, 2, 128
HPG_A = H_A // KV_A
L_CMP, STRIDE_CMP, L_SEL, N_SEL, WIN_A = 32, 16, 64, 8, 512
H_B, KV_B, HD_B, WIN_B = 8, 2, 64, 128
HPG_B = H_B // KV_B
H_C, HD_C, BLK_C, TOPK_C = 4, 128, 256, 3
N_GROUPS, EXP_PER_GROUP, D_EXPERT, TOPK_EXPERT = 4, 8, 256, 2
N_EXPERTS = N_GROUPS * EXP_PER_GROUP
LN_EPS = 1e-5
NEG_INF = -1e30
ALPHA = (2.0 * DEPTH) ** 0.25

LANES = 128
SUBLANES = 8
VMEM_LIMIT = 48 * 1024 * 1024
BIG = 2.0 ** 100

Q_A_W = H_A * HD_A
KV_A_W = 3 * 2 * KV_A * HD_A
GATE_A_W = 3 * H_A
Q_B_W = H_B * HD_B
KV_B_W = KV_B * HD_B
C_W = H_C * HD_C
MERGE_W = 3 * D_MODEL
D_IN = Q_A_W + KV_A_W + GATE_A_W + Q_B_W + 2 * KV_B_W + 3 * C_W + MERGE_W
OFF_MERGE = 0
OFF_QA = OFF_MERGE + MERGE_W
OFF_QB = OFF_QA + Q_A_W
OFF_QC = OFF_QB + Q_B_W
OFF_KC = OFF_QC + C_W
OFF_VC = OFF_KC + C_W
OFF_KVA = OFF_VC + C_W
OFF_KB = OFF_KVA + KV_A_W
OFF_VB = OFF_KB + KV_B_W
OFF_GATE_A = OFF_VB + KV_B_W
GATE_A_PAD = 256
Z_W = OFF_GATE_A + GATE_A_PAD
assert OFF_QA % (HPG_A * HD_A) == 0 and OFF_QB % Q_B_W == 0 and OFF_MERGE % D_MODEL == 0
assert OFF_QC % C_W == 0 and OFF_KC % C_W == 0 and OFF_VC % C_W == 0

AUG_POS_HI, AUG_POS_LO, AUG_ONE, AUG_PAD = 32, 33, 34, 35
POS_SPLIT = 256

RELAYOUT_COLS = 256
MM_TM, MM_TN = 2048, 1024
ROW_TM = 256
ROUTE_TM = 512
EXP_TM = 256
EXP_CHUNKS = 6
ATT_TQ = 256
SWA_TQ = 128
COMBINE_SPLIT = 4


def _cparams(*sem):
    return pltpu.CompilerParams(dimension_semantics=sem, vmem_limit_bytes=VMEM_LIMIT)


def _dot(a, b):
    return jnp.dot(a, b, preferred_element_type=F32)


def _dot_nt(a, b):
    return lax.dot_general(a, b, (((1,), (1,)), ((), ())), preferred_element_type=F32)


def _iota(shape, dim):
    return lax.broadcasted_iota(jnp.int32, shape, dim)


def _split3(x):
    hi = x.astype(BF16)
    r1 = x - hi.astype(F32)
    mid = r1.astype(BF16)
    lo = (r1 - mid.astype(F32)).astype(BF16)
    return hi, mid, lo


def _pick_lane(x, lane_idx):
    lane = _iota(x.shape, 1)
    return jnp.sum(jnp.where(lane == lane_idx, x, 0.0), axis=-1, keepdims=True)


def _topk_rows(vals, k):
    row = _iota(vals.shape, 0).astype(F32)
    sel = jnp.zeros(vals.shape, F32)
    for _ in range(k):
        m = jnp.max(vals, axis=0, keepdims=True)
        idx = jnp.min(jnp.where(vals == m, row, float(LANES)), axis=0, keepdims=True)
        pick = row == idx
        sel = jnp.where(pick, 1.0, sel)
        vals = jnp.where(pick, -jnp.inf, vals)
    return sel


def _rows_to_lanes(x_t, tq):
    pad = jnp.zeros((LANES - x_t.shape[0], tq), F32)
    return jnp.concatenate([x_t, pad], axis=0).T


def _query_aug(base, slope, center, pad_flag=False):
    lane = _iota(base.shape, 1)
    aug = jnp.where(lane == AUG_POS_HI, slope * float(POS_SPLIT), base)
    aug = jnp.where(lane == AUG_POS_LO, slope, aug)
    aug = jnp.where(lane == AUG_ONE, -slope * center, aug)
    if pad_flag:
        aug = jnp.where(lane == AUG_PAD, 1.0, aug)
    return aug.astype(BF16)


def _softmax_update(s, v, carry):
    m, l, acc = carry
    m_new = jnp.maximum(m, jnp.max(s, axis=-1, keepdims=True))
    alpha = jnp.exp(m - m_new)
    p = jnp.exp(s - m_new)
    l = alpha * l + jnp.sum(p, axis=-1, keepdims=True)
    acc = alpha * acc + _dot(p.astype(BF16), v)
    return m_new, l, acc


def _softmax_init(rows, hd):
    return (jnp.full((rows, 1), NEG_INF, F32), jnp.zeros((rows, 1), F32), jnp.zeros((rows, hd), F32))


def _z_column_sources():
    s_kva = Q_A_W
    s_gate = s_kva + KV_A_W
    s_qb = s_gate + GATE_A_W
    s_kb = s_qb + Q_B_W
    s_qc = s_kb + 2 * KV_B_W
    s_merge = s_qc + 3 * C_W
    assert GATE_A_PAD == RELAYOUT_COLS
    segments = [(OFF_MERGE, s_merge, MERGE_W), (OFF_QA, 0, Q_A_W), (OFF_QB, s_qb, Q_B_W),
                (OFF_QC, s_qc, 3 * C_W), (OFF_KVA, s_kva, KV_A_W), (OFF_KB, s_kb, 2 * KV_B_W)]
    src = np.zeros((Z_W // RELAYOUT_COLS,), np.int32)
    valid = np.zeros((Z_W // RELAYOUT_COLS,), np.int32)
    for dst, start, width in segments:
        assert dst % RELAYOUT_COLS == 0 and width % RELAYOUT_COLS == 0
        for b in range(width // RELAYOUT_COLS):
            src[dst // RELAYOUT_COLS + b] = start + b * RELAYOUT_COLS
            valid[dst // RELAYOUT_COLS + b] = RELAYOUT_COLS
    src[OFF_GATE_A // RELAYOUT_COLS] = s_gate
    valid[OFF_GATE_A // RELAYOUT_COLS] = GATE_A_W
    assert np.all(src % SUBLANES == 0) and np.all(src + RELAYOUT_COLS <= D_IN)
    return jnp.asarray(src // SUBLANES), jnp.asarray(valid)


def _prep_w_in_kernel(src_ref, valid_ref, w_ref, o_ref):
    j = pl.program_id(0)
    w = w_ref[0]
    w = jnp.where(_iota(w.shape, 0) < valid_ref[j], w, 0.0)
    o_ref[...] = w.T.astype(BF16)


def _prep_w_in(w_in_t, layer):
    src, valid = _z_column_sources()
    grid_spec = pltpu.PrefetchScalarGridSpec(
        num_scalar_prefetch=2,
        grid=(Z_W // RELAYOUT_COLS,),
        in_specs=[pl.BlockSpec((pl.Element(1), pl.Element(RELAYOUT_COLS), pl.Element(D_MODEL)),
                               lambda j, src, valid: (layer, src[j] * SUBLANES, 0))],
        out_specs=pl.BlockSpec((D_MODEL, RELAYOUT_COLS), lambda j, src, valid: (0, j)),
    )
    return pl.pallas_call(
        _prep_w_in_kernel,
        grid_spec=grid_spec,
        out_shape=jax.ShapeDtypeStruct((D_MODEL, Z_W), BF16),
        compiler_params=_cparams("parallel"),
        name="w_in_relayout",
    )(src, valid, w_in_t)


def _mm_kernel(a_ref, b_ref, o_ref):
    o_ref[...] = _dot(a_ref[...], b_ref[...]).astype(o_ref.dtype)


def _matmul(a, b, out_dtype):
    m, k = a.shape
    n = b.shape[1]
    return pl.pallas_call(
        _mm_kernel,
        grid=(m // MM_TM, n // MM_TN),
        in_specs=[pl.BlockSpec((MM_TM, k), lambda i, j: (i, 0)),
                  pl.BlockSpec((k, MM_TN), lambda i, j: (0, j))],
        out_specs=pl.BlockSpec((MM_TM, MM_TN), lambda i, j: (i, j)),
        out_shape=jax.ShapeDtypeStruct((m, n), out_dtype),
        compiler_params=_cparams("parallel", "parallel"),
        name="in_proj",
    )(a, b)


def _cmp_kv_kernel(k_ref, pos_ref, w_ref, o_ref, kf_ref, *, seq):
    kf_ref[0:seq, :] = k_ref[...].astype(F32)
    kf_ref[seq:seq + LANES, :] = jnp.zeros((LANES, HD_A), F32)
    acc = jnp.zeros((LANES, HD_A), F32)
    for l in range(L_CMP):
        rows = kf_ref[pl.ds(l, LANES, stride=STRIDE_CMP), :] + pos_ref[0, 0, l:l + 1, :]
        acc = acc + _dot(rows.astype(BF16), w_ref[0, 0, l].astype(BF16))
    o_ref[0, 0] = acc.astype(o_ref.dtype)


def _cmp_kv(z, cmp_pos, cmp_w, layer, batch, seq):
    blk0 = OFF_KVA // LANES
    return pl.pallas_call(
        functools.partial(_cmp_kv_kernel, seq=seq),
        grid=(batch, 2 * KV_A),
        in_specs=[pl.BlockSpec((seq, LANES), lambda b, j: (b, blk0 + j)),
                  pl.BlockSpec((1, 1, L_CMP, HD_A), lambda b, j: (layer, j // KV_A, 0, 0)),
                  pl.BlockSpec((1, 1, L_CMP, HD_A, HD_A), lambda b, j: (layer, j // KV_A, 0, 0, 0))],
        out_specs=pl.BlockSpec((1, 1, LANES, HD_A), lambda b, j: (b, j, 0, 0)),
        out_shape=jax.ShapeDtypeStruct((batch, 2 * KV_A, LANES, HD_A), BF16),
        scratch_shapes=[pltpu.VMEM((seq + LANES, HD_A), F32)],
        compiler_params=_cparams("parallel", "parallel"),
        name="nsa_cmp_kv",
    )(z, cmp_pos, cmp_w)


def _nsa_cmp_kernel(q_ref, kvc_ref, gate_ref, inter_t_ref, o_ref, nsel_ref, *, tq):
    i = pl.program_id(1)
    shape = (tq, LANES)
    t = i * tq + _iota(shape, 0)
    lane = _iota(shape, 1)
    dist_i = t - (lane * STRIDE_CMP + (L_CMP - 1))
    ok = dist_i >= 0
    okf = ok.astype(F32)
    dist = dist_i.astype(F32)
    sig = jax.nn.sigmoid(gate_ref[...].astype(F32))
    scale = HD_A ** -0.5
    n_sel_rows = 32
    inter_t = inter_t_ref[...]
    shape_t = (n_sel_rows, tq)
    j = _iota(shape_t, 0)
    blk_t = (i * tq + _iota(shape_t, 1)) // L_SEL
    valid = j <= blk_t
    forced = (j == 0) | (j == blk_t) | (j == blk_t - 1)
    for g in range(KV_A):
        kc = kvc_ref[0, g]
        vc = kvc_ref[0, KV_A + g]
        psum = jnp.zeros(shape, F32)
        outs = []
        for h in range(HPG_A):
            head = g * HPG_A + h
            q = q_ref[:, head * HD_A:(head + 1) * HD_A]
            s = _dot_nt(q, kc) * scale - (2.0 ** (-8.0 * (head + 1) / H_A)) * dist
            s = jnp.where(ok, s, NEG_INF)
            e = jnp.exp(s - jnp.max(s, axis=-1, keepdims=True))
            p = e / jnp.sum(e, axis=-1, keepdims=True) * okf
            psum = psum + p
            o = _dot(p.astype(BF16), vc)
            outs.append(o * sig[:, head * 3:head * 3 + 1])
        o_ref[:, g * HPG_A * HD_A:(g + 1) * HPG_A * HD_A] = jnp.concatenate(outs, axis=1).astype(o_ref.dtype)

        imp_t = sum(_dot_nt(inter_t, part) for part in _split3(psum))[0:n_sel_rows]
        vals = jnp.where(forced, jnp.inf, jnp.where(valid, imp_t, -jnp.inf))
        sel = _topk_rows(vals, N_SEL)
        not_selected = jnp.where(valid & (sel > 0.5), 0.0, 1.0)
        nsel_ref[0, g] = _rows_to_lanes(not_selected, tq).astype(nsel_ref.dtype)


def _nsa_cmp(z, kvc, inter_t, batch, seq):
    tq = ATT_TQ
    nq = seq // tq
    gate_blk = OFF_GATE_A // LANES
    return pl.pallas_call(
        functools.partial(_nsa_cmp_kernel, tq=tq),
        grid=(batch, nq),
        in_specs=[pl.BlockSpec((tq, Q_A_W), lambda b, i: (b * nq + i, OFF_QA // Q_A_W)),
                  pl.BlockSpec((1, 2 * KV_A, LANES, HD_A), lambda b, i: (b, 0, 0, 0)),
                  pl.BlockSpec((tq, LANES), lambda b, i: (b * nq + i, gate_blk)),
                  pl.BlockSpec((LANES, LANES), lambda b, i: (0, 0))],
        out_specs=[pl.BlockSpec((tq, Q_A_W), lambda b, i: (b * nq + i, 0)),
                   pl.BlockSpec((1, KV_A, tq, LANES), lambda b, i: (b, 0, i, 0))],
        out_shape=[jax.ShapeDtypeStruct((batch * seq, Q_A_W), BF16),
                   jax.ShapeDtypeStruct((batch, KV_A, seq, LANES), BF16)],
        compiler_params=_cparams("parallel", "parallel"),
        name="nsa_cmp_attn",
    )(z, kvc, z, inter_t)


def _scaled_q(q_ref, h, hd, scale):
    return (q_ref[:, h * hd:(h + 1) * hd].astype(F32) * scale).astype(BF16)


def _gated_heads(o, sig, g, branch, tq):
    outs = [o[h * tq:(h + 1) * tq] * _pick_lane(sig, (g * HPG_A + h) * 3 + branch) for h in range(HPG_A)]
    return jnp.concatenate(outs, axis=1)


def _nsa_slc_kernel(slopes_ref, cnt_ref, lst_ref, q_ref, k_ref, v_ref, kaug_ref, causal_ref, nsel_ref, gate_ref,
                    o_ref, *, tq):
    g = pl.program_id(1)
    i = pl.program_id(2)
    nq = pl.num_programs(2)
    plan = (pl.program_id(0) * KV_A + g) * nq + i
    rows = HPG_A * tq
    scale = HD_A ** -0.5
    nsel = nsel_ref[0, 0].astype(F32)
    center = (i * tq).astype(F32)
    qx = jnp.concatenate(
        [jnp.concatenate([_scaled_q(q_ref, h, HD_A, scale),
                          _query_aug(nsel, slopes_ref[g * HPG_A + h], center)], axis=1)
         for h in range(HPG_A)], axis=0)

    def scores(kt):
        k0 = pl.multiple_of(kt * tq, tq)
        kx = jnp.concatenate([k_ref[pl.ds(k0, tq), :], kaug_ref[pl.ds(k0, tq), :]], axis=1)
        return _dot_nt(qx, kx)

    def values(kt):
        return v_ref[pl.ds(pl.multiple_of(kt * tq, tq), tq), :]

    def body(n, carry):
        kt = lst_ref[plan * nq + n]
        return _softmax_update(scores(kt), values(kt), carry)

    m, l, acc = lax.fori_loop(0, cnt_ref[plan], body, _softmax_init(rows, HD_A))
    causal_bias = jnp.concatenate([causal_ref[...]] * HPG_A, axis=0)
    _, l, acc = _softmax_update(scores(i) + causal_bias, values(i), (m, l, acc))
    sig = jax.nn.sigmoid(gate_ref[...].astype(F32))
    o_ref[...] = _gated_heads(acc / l, sig, g, 1, tq).astype(o_ref.dtype)


def _slc_tile_plan(nsel, tq):
    b, g, s, _ = nsel.shape
    nq = s // tq
    per = tq // L_SEL
    chosen = (nsel[..., :nq * per].astype(F32) < 0.5).reshape(b, g, nq, tq, nq, per)
    need = jnp.any(chosen, axis=(3, 5))
    tile = jnp.arange(nq, dtype=jnp.int32)
    need = need & (tile[None, :] < tile[:, None])
    pos = jnp.cumsum(need.astype(jnp.int32), axis=-1) - 1
    hit = need[..., None] & (pos[..., None] == tile)
    lst = jnp.sum(jnp.where(hit, tile[:, None], 0), axis=-2)
    cnt = jnp.sum(need.astype(jnp.int32), axis=-1)
    return cnt.reshape(-1), lst.reshape(-1).astype(jnp.int32)


def _nsa_slc(z, nsel, kaug, slopes, batch, seq):
    tq = ATT_TQ
    nq = seq // tq
    cnt, lst = _slc_tile_plan(nsel, tq)
    qw = HPG_A * HD_A
    kblk = OFF_KVA // LANES + 1 * 2 * KV_A
    vblk = kblk + KV_A
    gate_blk = OFF_GATE_A // LANES
    causal = jnp.asarray(np.where(np.arange(tq)[:, None] >= np.arange(tq)[None, :], 0.0, -BIG).astype(np.float32))
    return pl.pallas_call(
        functools.partial(_nsa_slc_kernel, tq=tq),
        grid=(batch, KV_A, nq),
        in_specs=[pl.BlockSpec(memory_space=pltpu.SMEM),
                  pl.BlockSpec(memory_space=pltpu.SMEM),
                  pl.BlockSpec(memory_space=pltpu.SMEM),
                  pl.BlockSpec((tq, qw), lambda b, g, i: (b * nq + i, OFF_QA // qw + g)),
                  pl.BlockSpec((seq, LANES), lambda b, g, i: (b, kblk + g)),
                  pl.BlockSpec((seq, LANES), lambda b, g, i: (b, vblk + g)),
                  pl.BlockSpec((seq, LANES), lambda b, g, i: (0, 0)),
                  pl.BlockSpec((tq, tq), lambda b, g, i: (0, 0)),
                  pl.BlockSpec((1, 1, tq, LANES), lambda b, g, i: (b, g, i, 0)),
                  pl.BlockSpec((tq, LANES), lambda b, g, i: (b * nq + i, gate_blk))],
        out_specs=pl.BlockSpec((tq, qw), lambda b, g, i: (b * nq + i, g)),
        out_shape=jax.ShapeDtypeStruct((batch * seq, Q_A_W), BF16),
        compiler_params=_cparams("parallel", "parallel", "parallel"),
        name="nsa_slc_attn",
    )(slopes, cnt, lst, z, z, z, kaug, causal, nsel, z)


def _nsa_win_kernel(slopes_ref, q_ref, k_ref, v_ref, kaug_ref, band_ref, gate_ref, o_ref, kx_ref, vx_ref, *, tq, window, seq):
    g = pl.program_id(1)
    i = pl.program_id(2)

    @pl.when(i == 0)
    def _():
        kx_ref[0:window, 0:HD_A] = jnp.zeros((window, HD_A), BF16)
        kx_ref[window:window + seq, 0:HD_A] = k_ref[...]
        kx_ref[:, HD_A:2 * HD_A] = kaug_ref[...]
        vx_ref[0:window, :] = jnp.zeros((window, HD_A), BF16)
        vx_ref[window:window + seq, :] = v_ref[...]

    span = window + tq
    r0 = pl.multiple_of(i * tq, tq)
    center = (i * tq + window).astype(F32)
    base = jnp.zeros((tq, LANES), F32)
    sig = jax.nn.sigmoid(gate_ref[...].astype(F32))
    pair = 2
    band_bias = jnp.concatenate([band_ref[...]] * pair, axis=0)
    for h0 in range(0, HPG_A, pair):
        qx = jnp.concatenate(
            [jnp.concatenate([_scaled_q(q_ref, h, HD_A, HD_A ** -0.5),
                              _query_aug(base, slopes_ref[g * HPG_A + h], center, pad_flag=True)], axis=1)
             for h in range(h0, h0 + pair)], axis=0)
        s = _dot_nt(qx, kx_ref[pl.ds(r0, span), :]) + band_bias
        p = jnp.exp(s - jnp.max(s, axis=-1, keepdims=True))
        o = _dot(p.astype(BF16), vx_ref[pl.ds(r0, span), :]) / jnp.sum(p, axis=-1, keepdims=True)
        for h in range(h0, h0 + pair):
            gated = o[(h - h0) * tq:(h - h0 + 1) * tq] * _pick_lane(sig, (g * HPG_A + h) * 3 + 2)
            o_ref[:, h * HD_A:(h + 1) * HD_A] = gated.astype(o_ref.dtype)


def _band_bias(tq, window):
    r = np.arange(tq)[:, None]
    c = np.arange(window + tq)[None, :]
    return jnp.asarray(np.where((c > r) & (c <= r + window), 0.0, -BIG).astype(np.float32))


def _nsa_win(z, kaug_pad, slopes, batch, seq):
    tq = ATT_TQ
    nq = seq // tq
    qw = HPG_A * HD_A
    kblk = OFF_KVA // LANES + 2 * 2 * KV_A
    vblk = kblk + KV_A
    gate_blk = OFF_GATE_A // LANES
    return pl.pallas_call(
        functools.partial(_nsa_win_kernel, tq=tq, window=WIN_A, seq=seq),
        grid=(batch, KV_A, nq),
        in_specs=[pl.BlockSpec(memory_space=pltpu.SMEM),
                  pl.BlockSpec((tq, qw), lambda b, g, i: (b * nq + i, OFF_QA // qw + g)),
                  pl.BlockSpec((seq, LANES), lambda b, g, i: (b, kblk + g)),
                  pl.BlockSpec((seq, LANES), lambda b, g, i: (b, vblk + g)),
                  pl.BlockSpec((WIN_A + seq, LANES), lambda b, g, i: (0, 0)),
                  pl.BlockSpec((tq, WIN_A + tq), lambda b, g, i: (0, 0)),
                  pl.BlockSpec((tq, LANES), lambda b, g, i: (b * nq + i, gate_blk))],
        out_specs=pl.BlockSpec((tq, qw), lambda b, g, i: (b * nq + i, g)),
        out_shape=jax.ShapeDtypeStruct((batch * seq, Q_A_W), BF16),
        scratch_shapes=[pltpu.VMEM((WIN_A + seq, 2 * HD_A), BF16), pltpu.VMEM((WIN_A + seq, HD_A), BF16)],
        compiler_params=_cparams("parallel", "parallel", "arbitrary"),
        name="nsa_win_attn",
    )(slopes, z, z, z, kaug_pad, _band_bias(tq, WIN_A), z)


SW_POS_HI, SW_POS_LO, SW_ONE, SW_PAD = 0, 1, 2, 3


def _swa_kernel(sink_ref, q_ref, k_ref, v_ref, kaug_ref, band_ref, o_ref, kx_ref, vx_ref, *, tq, window, seq):
    i = pl.program_id(1)

    @pl.when(i == 0)
    def _():
        for g in range(KV_B):
            kx_ref[g, 0:window, 0:HD_B] = jnp.zeros((window, HD_B), BF16)
            kx_ref[g, window:window + seq, 0:HD_B] = k_ref[:, g * HD_B:(g + 1) * HD_B]
            kx_ref[g, :, HD_B:2 * HD_B] = kaug_ref[...]
            vx_ref[g, 0:window, :] = jnp.zeros((window, HD_B), BF16)
            vx_ref[g, window:window + seq, :] = v_ref[:, g * HD_B:(g + 1) * HD_B]

    span = window + tq
    rows = HPG_B * tq
    r0 = pl.multiple_of(i * tq, tq)
    center = (i * tq + window).astype(F32)
    band_bias = jnp.concatenate([band_ref[...]] * HPG_B, axis=0)
    lane = _iota((tq, HD_B), 1)
    hh = _iota((rows, 1), 0) // tq
    row_in_tile = (_iota((rows, 1), 0) - hh * tq).astype(F32)

    def head_column(values):
        col = jnp.full((rows, 1), values[HPG_B - 1], F32)
        for h in range(HPG_B - 2, -1, -1):
            col = jnp.where(hh == h, values[h], col)
        return col

    outs = []
    for g in range(KV_B):
        slopes = [2.0 ** (-8.0 * (g * HPG_B + h + 1) / H_B) for h in range(HPG_B)]
        parts = []
        for h in range(HPG_B):
            aug = jnp.where(lane == SW_POS_HI, slopes[h] * POS_SPLIT, jnp.where(lane == SW_POS_LO, slopes[h], 0.0))
            aug = jnp.where(lane == SW_ONE, -slopes[h] * center, jnp.where(lane == SW_PAD, 1.0, aug))
            parts.append(jnp.concatenate([_scaled_q(q_ref, g * HPG_B + h, HD_B, HD_B ** -0.5),
                                          aug.astype(BF16)], axis=1))
        qx = jnp.concatenate(parts, axis=0)
        s = _dot_nt(qx, kx_ref[g, pl.ds(r0, span), :]) + band_bias
        sink_shifted = (head_column([sink_ref[g * HPG_B + h] for h in range(HPG_B)])
                        + head_column(slopes) * row_in_tile)
        m = jnp.maximum(jnp.max(s, axis=-1, keepdims=True), sink_shifted)
        p = jnp.exp(s - m)
        denom = jnp.sum(p, axis=-1, keepdims=True) + jnp.exp(sink_shifted - m)
        o = _dot(p.astype(BF16), vx_ref[g, pl.ds(r0, span), :]) / denom
        outs += [o[h * tq:(h + 1) * tq] for h in range(HPG_B)]
    o_ref[...] = jnp.concatenate(outs, axis=1).astype(o_ref.dtype)


def _swa(z, kaug_sw, sink, layer, batch, seq):
    tq = SWA_TQ
    nq = seq // tq
    return pl.pallas_call(
        functools.partial(_swa_kernel, tq=tq, window=WIN_B, seq=seq),
        grid=(batch, nq),
        in_specs=[pl.BlockSpec(memory_space=pltpu.SMEM),
                  pl.BlockSpec((tq, Q_B_W), lambda b, i: (b * nq + i, OFF_QB // Q_B_W)),
                  pl.BlockSpec((seq, KV_B_W), lambda b, i: (b, OFF_KB // KV_B_W)),
                  pl.BlockSpec((seq, KV_B_W), lambda b, i: (b, OFF_VB // KV_B_W)),
                  pl.BlockSpec((WIN_B + seq, HD_B), lambda b, i: (0, 0)),
                  pl.BlockSpec((tq, WIN_B + tq), lambda b, i: (0, 0))],
        out_specs=pl.BlockSpec((tq, Q_B_W), lambda b, i: (b * nq + i, 0)),
        out_shape=jax.ShapeDtypeStruct((batch * seq, Q_B_W), BF16),
        scratch_shapes=[pltpu.VMEM((KV_B, WIN_B + seq, 2 * HD_B), BF16),
                        pltpu.VMEM((KV_B, WIN_B + seq, HD_B), BF16)],
        compiler_params=_cparams("parallel", "arbitrary"),
        name="swa_attn",
    )(sink[layer], z, z, z, kaug_sw, _band_bias(tq, WIN_B))


def _moba_kernel(slopes_ref, q_ref, k_ref, v_ref, kaug_ref, avg_ref, o_ref, km_ref, *, seq):
    i = pl.program_id(1)
    tq = BLK_C
    n_blk = seq // BLK_C
    scale = HD_C ** -0.5

    @pl.when(i == 0)
    def _():
        for h in range(H_C):
            k_mean = _dot(avg_ref[...], k_ref[:, h * HD_C:(h + 1) * HD_C])[0:SUBLANES]
            terms = [t.astype(F32) for t in _split3(k_mean)] + [jnp.zeros((SUBLANES, HD_C), F32)]
            km_ref[h] = jnp.concatenate(terms, axis=0).astype(BF16)

    center = (i * tq).astype(F32)
    blk = _iota((SUBLANES, tq), 0)
    past = blk < i
    qx = []
    for h in range(H_C):
        q = q_ref[:, h * HD_C:(h + 1) * HD_C]
        sc = _dot_nt(km_ref[h], q)
        score_t = sc[0:SUBLANES] + sc[SUBLANES:2 * SUBLANES] + sc[2 * SUBLANES:3 * SUBLANES]
        sel = _topk_rows(jnp.where(past, score_t, -jnp.inf), TOPK_C)
        not_selected = jnp.where(past & (sel < 0.5), 1.0, 0.0)
        qx.append(jnp.concatenate([_scaled_q(q_ref, h, HD_C, scale),
                                   _query_aug(_rows_to_lanes(not_selected, tq), slopes_ref[h], center)], axis=1))
    assert n_blk <= SUBLANES

    def scores(h, kt):
        k0 = pl.multiple_of(kt * tq, tq)
        kx = jnp.concatenate([k_ref[pl.ds(k0, tq), h * HD_C:(h + 1) * HD_C], kaug_ref[pl.ds(k0, tq), :]], axis=1)
        return _dot_nt(qx[h], kx)

    def values(h, kt):
        return v_ref[pl.ds(pl.multiple_of(kt * tq, tq), tq), h * HD_C:(h + 1) * HD_C]

    def body(kt, carry):
        out = []
        for h in range(H_C):
            s, m, l, acc = carry[h]
            s_next = scores(h, kt + 1)
            out.append((s_next,) + _softmax_update(s, values(h, kt), (m, l, acc)))
        return tuple(out)

    init = tuple((scores(h, 0),) + _softmax_init(tq, HD_C) for h in range(H_C))
    carry = lax.fori_loop(0, i, body, init)
    causal = _iota((tq, tq), 0) >= _iota((tq, tq), 1)
    outs = []
    for h in range(H_C):
        s, m, l, acc = carry[h]
        _, l, acc = _softmax_update(jnp.where(causal, s, -BIG), values(h, i), (m, l, acc))
        outs.append(acc / l)
    o_ref[...] = jnp.concatenate(outs, axis=1).astype(o_ref.dtype)


def _moba(z, kaug, avg, slopes, batch, seq):
    nq = seq // BLK_C
    return pl.pallas_call(
        functools.partial(_moba_kernel, seq=seq),
        grid=(batch, nq),
        in_specs=[pl.BlockSpec(memory_space=pltpu.SMEM),
                  pl.BlockSpec((BLK_C, C_W), lambda b, i: (b * nq + i, OFF_QC // C_W)),
                  pl.BlockSpec((seq, C_W), lambda b, i: (b, OFF_KC // C_W)),
                  pl.BlockSpec((seq, C_W), lambda b, i: (b, OFF_VC // C_W)),
                  pl.BlockSpec((seq, LANES), lambda b, i: (0, 0)),
                  pl.BlockSpec((2 * SUBLANES, seq), lambda b, i: (0, 0))],
        out_specs=pl.BlockSpec((BLK_C, C_W), lambda b, i: (b * nq + i, 0)),
        out_shape=jax.ShapeDtypeStruct((batch * seq, C_W), BF16),
        scratch_shapes=[pltpu.VMEM((H_C, 4 * SUBLANES, HD_C), BF16)],
        compiler_params=_cparams("parallel", "arbitrary"),
        name="moba_attn",
    )(slopes, z, z, z, kaug, avg)


def _merge_kernel(oc_ref, os_ref, ow_ref, ob_ref, om_ref, g0_ref, g1_ref, g2_ref, wa_ref, wb_ref, wc_ref, o_ref):
    o_a = (oc_ref[...].astype(F32) + os_ref[...].astype(F32) + ow_ref[...].astype(F32)).astype(BF16)
    merged = jax.nn.sigmoid(g0_ref[...].astype(F32)) * _dot(o_a, wa_ref[...])
    merged = merged + jax.nn.sigmoid(g1_ref[...].astype(F32)) * _dot(ob_ref[...], wb_ref[...])
    merged = merged + jax.nn.sigmoid(g2_ref[...].astype(F32)) * _dot(om_ref[...], wc_ref[...])
    o_ref[...] = merged.astype(o_ref.dtype)


def _merge(o_cmp, o_slc, o_win, o_b, o_c, z, wa, wb, wc):
    tm = 2 * ROW_TM
    t = z.shape[0]
    gblk = OFF_MERGE // D_MODEL
    row = lambda w: pl.BlockSpec((tm, w), lambda i: (i, 0))
    full = lambda a: pl.BlockSpec(a.shape, lambda i: (0, 0))
    gate = lambda r: pl.BlockSpec((tm, D_MODEL), lambda i: (i, gblk + r))
    return pl.pallas_call(
        _merge_kernel,
        grid=(t // tm,),
        in_specs=[row(Q_A_W), row(Q_A_W), row(Q_A_W), row(Q_B_W), row(C_W),
                  gate(0), gate(1), gate(2), full(wa), full(wb), full(wc)],
        out_specs=row(D_MODEL),
        out_shape=jax.ShapeDtypeStruct((t, D_MODEL), BF16),
        compiler_params=_cparams("parallel"),
        name="mixer_merge",
    )(o_cmp, o_slc, o_win, o_b, o_c, z, z, z, wa, wb, wc)


def _layer_norm(h, g_ref, b_ref):
    mu = jnp.mean(h, axis=-1, keepdims=True)
    xc = h - mu
    var = jnp.mean(xc * xc, axis=-1, keepdims=True)
    return xc * lax.rsqrt(var + LN_EPS) * g_ref[...] + b_ref[...]


def _proj_ln_kernel(m_ref, w_ref, x_ref, g_ref, b_ref, xo_ref, xb_ref):
    y = _dot(m_ref[...], w_ref[...])
    out = _layer_norm(ALPHA * x_ref[...] + y, g_ref, b_ref)
    xo_ref[...] = out
    xb_ref[...] = out.astype(BF16)


def _proj_ln(merged, w_out, x, g, b):
    tm = 2 * ROW_TM
    t = x.shape[0]
    row = pl.BlockSpec((tm, D_MODEL), lambda i: (i, 0))
    vec = pl.BlockSpec((1, D_MODEL), lambda i: (0, 0))
    return pl.pallas_call(
        _proj_ln_kernel,
        grid=(t // tm,),
        in_specs=[row, pl.BlockSpec((D_MODEL, D_MODEL), lambda i: (0, 0)), row, vec, vec],
        out_specs=[row, row],
        out_shape=[jax.ShapeDtypeStruct((t, D_MODEL), F32), jax.ShapeDtypeStruct((t, D_MODEL), BF16)],
        compiler_params=_cparams("parallel"),
        name="out_proj_ln",
    )(merged, w_out, x, g, b)


def _moe_ln_kernel(x_ref, ya_ref, yb_ref, w_ref, g_ref, b_ref, *rest):
    xo_ref, xb_ref = rest[-2:]
    w = w_ref[...]
    y = w[:, 0:1] * ya_ref[...].astype(F32) + w[:, 1:2] * yb_ref[...].astype(F32)
    out = _layer_norm(ALPHA * x_ref[...] + y, g_ref, b_ref)
    xo_ref[...] = out
    xb_ref[...] = out.astype(BF16)


def _moe_ln(x, y_groups, wts, g, b):
    tm = ROW_TM
    t = x.shape[0]
    blocks = t // tm // len(y_groups)
    vec = pl.BlockSpec((1, D_MODEL), lambda i: (0, 0))
    group_row = pl.BlockSpec((tm, D_MODEL), lambda i: (i, 0))
    outs = None
    for n, (ya, yb) in enumerate(y_groups):
        off = n * blocks
        row = pl.BlockSpec((tm, D_MODEL), lambda i, off=off: (off + i, 0))
        in_specs = [row, group_row, group_row, pl.BlockSpec((tm, LANES), lambda i, off=off: (off + i, 0)), vec, vec]
        args = [x, ya, yb, wts, g, b]
        aliases = {}
        if outs is not None:
            in_specs += [pl.BlockSpec(memory_space=pl.ANY)] * 2
            aliases = {len(args): 0, len(args) + 1: 1}
            args += list(outs)
        outs = pl.pallas_call(
            _moe_ln_kernel,
            grid=(blocks,),
            in_specs=in_specs,
            out_specs=[row, row],
            out_shape=[jax.ShapeDtypeStruct((t, D_MODEL), F32), jax.ShapeDtypeStruct((t, D_MODEL), BF16)],
            input_output_aliases=aliases,
            compiler_params=_cparams("parallel"),
            name="moe_combine_ln",
        )(*args)
    return outs


def _router_kernel(x_ref, w_ref, b_ref, id_ref, wt_ref):
    x = x_ref[...]
    x_hi = x.astype(BF16)
    x_lo = (x - x_hi.astype(F32)).astype(BF16)
    w = w_ref[...]
    w_hi = w.astype(BF16)
    w_lo = (w - w_hi.astype(F32)).astype(BF16)
    logits = _dot(x_hi, w_hi) + _dot(x_lo, w_hi) + _dot(x_hi, w_lo) + b_ref[...]
    lane = _iota(logits.shape, 1).astype(F32)
    first = lambda hit: jnp.min(jnp.where(hit, lane, float(LANES)), axis=-1, keepdims=True)
    gl = jnp.where(lane < N_GROUPS, logits, -jnp.inf)
    gm = jnp.max(gl, axis=-1, keepdims=True)
    g_w = 1.0 / jnp.sum(jnp.exp(gl - gm), axis=-1, keepdims=True)
    lo = N_GROUPS + first(gl == gm) * EXP_PER_GROUP
    el = jnp.where((lane >= lo) & (lane < lo + EXP_PER_GROUP), logits, -jnp.inf)
    m1 = jnp.max(el, axis=-1, keepdims=True)
    i1 = first(el == m1)
    el2 = jnp.where(lane == i1, -jnp.inf, el)
    m2 = jnp.max(el2, axis=-1, keepdims=True)
    i2 = first(el2 == m2)
    e2 = jnp.exp(m2 - m1)
    w1 = g_w / (1.0 + e2)
    w2 = g_w * e2 / (1.0 + e2)
    ids = jnp.where(lane == 0.0, i1 - N_GROUPS, jnp.where(lane == 1.0, i2 - N_GROUPS, 0.0))
    id_ref[...] = ids.astype(jnp.int32)
    wt_ref[...] = jnp.where(lane == 0.0, w1, jnp.where(lane == 1.0, w2, 0.0))


def _router(x, w_rt, b_rt):
    tm = ROUTE_TM
    t = x.shape[0]
    return pl.pallas_call(
        _router_kernel,
        grid=(t // tm,),
        in_specs=[pl.BlockSpec((tm, D_MODEL), lambda i: (i, 0)),
                  pl.BlockSpec((D_MODEL, LANES), lambda i: (0, 0)),
                  pl.BlockSpec((1, LANES), lambda i: (0, 0))],
        out_specs=[pl.BlockSpec((tm, LANES), lambda i: (i, 0)), pl.BlockSpec((tm, LANES), lambda i: (i, 0))],
        out_shape=[jax.ShapeDtypeStruct((t, LANES), jnp.int32), jax.ShapeDtypeStruct((t, LANES), F32)],
        compiler_params=_cparams("parallel"),
        name="moe_router",
    )(x, w_rt, b_rt)


def _expert_kernel(te_ref, na_ref, x_ref, wg_ref, wu_ref, wd_ref, *rest, tile0):
    y_ref = rest[-1]
    j = tile0 + pl.program_id(0)

    @pl.when(j < na_ref[0])
    def _():
        x = x_ref[...]
        hg = _dot(x, wg_ref[0, 0].astype(BF16))
        hu = _dot(x, wu_ref[0, 0].astype(BF16))
        a = hg * jax.nn.sigmoid(hg) * hu
        y_ref[...] = _dot(a.astype(BF16), wd_ref[0, 0].astype(BF16)).astype(y_ref.dtype)

    @pl.when(j >= na_ref[0])
    def _():
        y_ref[...] = jnp.zeros(y_ref.shape, y_ref.dtype)


def _experts(x_chunks, tile_expert, n_active, wg, wu, wd, layer):
    tm = EXP_TM
    tiles_per_chunk = x_chunks[0].shape[0] // tm
    n_tiles = len(x_chunks) * tiles_per_chunk
    y = None
    for c, x_c in enumerate(x_chunks):
        tile0 = c * tiles_per_chunk
        x_map = lambda j, te, na, tile0=tile0: (
            jnp.clip(jnp.minimum(tile0 + j, na[0] - 1) - tile0, 0, tiles_per_chunk - 1), 0)
        w_map = lambda j, te, na, tile0=tile0: (layer, te[tile0 + j], 0, 0)
        in_specs = [pl.BlockSpec((tm, D_MODEL), x_map),
                    pl.BlockSpec((1, 1, D_MODEL, D_EXPERT), w_map),
                    pl.BlockSpec((1, 1, D_MODEL, D_EXPERT), w_map),
                    pl.BlockSpec((1, 1, D_EXPERT, D_MODEL), w_map)]
        args = [tile_expert, n_active, x_c, wg, wu, wd]
        aliases = {}
        if y is not None:
            in_specs.append(pl.BlockSpec(memory_space=pl.ANY))
            aliases = {len(args): 0}
            args.append(y)
        grid_spec = pltpu.PrefetchScalarGridSpec(
            num_scalar_prefetch=2,
            grid=(tiles_per_chunk,),
            in_specs=in_specs,
            out_specs=pl.BlockSpec((tm, D_MODEL), lambda j, te, na, tile0=tile0: (tile0 + j, 0)),
        )
        y = pl.pallas_call(
            functools.partial(_expert_kernel, tile0=tile0),
            grid_spec=grid_spec,
            out_shape=jax.ShapeDtypeStruct((n_tiles * tm, D_MODEL), BF16),
            input_output_aliases=aliases,
            compiler_params=_cparams("arbitrary"),
            name="moe_experts",
        )(*args)
    return y


def _dispatch_plan(expert_ids, tm, n_tiles):
    t = expert_ids.shape[0]
    e_flat = expert_ids.reshape(-1)
    onehot = (e_flat[:, None] == jnp.arange(N_EXPERTS, dtype=jnp.int32)[None, :]).astype(jnp.int32)
    csum = jnp.cumsum(onehot, axis=0)
    rank = jnp.sum(csum * onehot, axis=1) - 1
    counts = csum[-1]
    padded = ((counts + tm - 1) // tm) * tm
    ends = jnp.cumsum(padded)
    dest = jnp.sum((ends - padded)[None, :] * onehot, axis=1) + rank
    filler = jnp.arange(n_tiles * tm, dtype=jnp.int32) % t
    scatter_add = lambda base, idx, vals: base.at[idx].add(vals, mode="promise_in_bounds", unique_indices=True)
    src_tok = compute_on("tpu_sparsecore")(jax.jit(scatter_add))(
        filler, dest, jnp.arange(2 * t, dtype=jnp.int32) // 2 - dest % t)
    n_active = ends[-1] // tm
    tile_start = jnp.minimum(jnp.arange(n_tiles, dtype=jnp.int32), n_active - 1) * tm
    te = jnp.sum((ends[None, :] <= tile_start[:, None]).astype(jnp.int32), axis=1)
    te = jnp.minimum(te, N_EXPERTS - 1)
    return dest.reshape(t, 2), src_tok, te, n_active.reshape(1).astype(jnp.int32)


def _bf16_const(a):
    a16 = a.astype(BF16)
    assert np.all(a16.astype(np.float32) == a)
    return jnp.asarray(a16)


def _alibi_np(n):
    slopes = np.asarray([2.0 ** (-8.0 * (i + 1) / n) for i in range(n)], np.float32)
    _bf16_const(slopes)
    return slopes


def _selection_overlap_t(seq):
    n_cmp = (seq - L_CMP) // STRIDE_CMP + 1
    n_sel = seq // L_SEL
    c_start = STRIDE_CMP * np.arange(n_cmp)
    s_start = L_SEL * np.arange(n_sel)
    inter = np.clip(np.minimum(c_start[:, None] + L_CMP, s_start[None, :] + L_SEL)
                    - np.maximum(c_start[:, None], s_start[None, :]), 0, None) / L_CMP
    out = np.zeros((LANES, LANES), np.float32)
    out[:n_sel, :n_cmp] = inter.T
    return _bf16_const(out)


def _key_aug(seq, block, pad_rows=0, width=LANES, lanes=(AUG_POS_HI, AUG_POS_LO, AUG_ONE, AUG_PAD)):
    rows = pad_rows + seq
    pos = np.arange(rows)
    key = pos - pad_rows
    real = key >= 0
    out = np.zeros((rows, width), np.float32)
    if block:
        out[pos[real], key[real] // block] = -BIG
    out[:, lanes[0]] = pos // POS_SPLIT
    out[:, lanes[1]] = pos % POS_SPLIT
    out[:, lanes[2]] = 1.0
    out[~real, lanes[3]] = -BIG
    return _bf16_const(out)


def _block_average(seq):
    out = np.zeros((2 * SUBLANES, seq), np.float32)
    for n in range(seq // BLK_C):
        out[n, n * BLK_C:(n + 1) * BLK_C] = 1.0 / BLK_C
    return _bf16_const(out)


def kernel(x, w_in, nsa_cmp_pos, nsa_cmp_w, sink_b, w_br_a, w_br_b, w_br_c, w_out, ln1_g, ln1_b,
           w_group, b_group, w_router, b_router, w_gate, w_up, w_down, ln2_g, ln2_b):
    batch, seq, d = x.shape
    t = batch * seq
    assert d == D_MODEL and w_in.shape[2] == D_IN and seq % BLK_C == 0 and t % MM_TM == 0
    assert seq // L_SEL <= AUG_POS_HI and (seq - L_CMP) // STRIDE_CMP + 1 < LANES
    assert seq // BLK_C <= SUBLANES and (WIN_A + seq) // POS_SPLIT < 256
    n_tiles = (TOPK_EXPERT * t) // EXP_TM + N_EXPERTS
    assert n_tiles % EXP_CHUNKS == 0 and (t // ROW_TM) % COMBINE_SPLIT == 0 and OFF_QA % Q_A_W == 0

    slopes_a = jnp.asarray(_alibi_np(H_A))
    slopes_c = jnp.asarray(_alibi_np(H_C))
    _alibi_np(H_B)
    inter_t = _selection_overlap_t(seq)
    kaug_slc = _key_aug(seq, L_SEL)
    kaug_win = _key_aug(seq, 0, pad_rows=WIN_A)
    kaug_swa = _key_aug(seq, 0, pad_rows=WIN_B, width=HD_B, lanes=(SW_POS_HI, SW_POS_LO, SW_ONE, SW_PAD))
    kaug_moba = _key_aug(seq, BLK_C)
    avg = _block_average(seq)
    take = lambda a, idx: a.at[idx].get(mode="promise_in_bounds")
    sc_take = compute_on("tpu_sparsecore")(jax.jit(take))

    w_in_t = jnp.swapaxes(w_in, 1, 2)
    xf = x.reshape(t, d)
    xb = xf.astype(BF16)
    for l in range(DEPTH):
        z = _matmul(xb, _prep_w_in(w_in_t, l), BF16)
        kvc = _cmp_kv(z, nsa_cmp_pos, nsa_cmp_w, l, batch, seq)
        o_cmp, nsel = _nsa_cmp(z, kvc, inter_t, batch, seq)
        o_slc = _nsa_slc(z, nsel, kaug_slc, slopes_a, batch, seq)
        o_win = _nsa_win(z, kaug_win, slopes_a, batch, seq)
        o_b = _swa(z, kaug_swa, sink_b, l, batch, seq)
        o_c = _moba(z, kaug_moba, avg, slopes_c, batch, seq)
        merged = _merge(o_cmp, o_slc, o_win, o_b, o_c, z,
                        w_br_a[l].astype(BF16), w_br_b[l].astype(BF16), w_br_c[l].astype(BF16))
        xf, xb = _proj_ln(merged, w_out[l].astype(BF16), xf, ln1_g[l][None, :], ln1_b[l][None, :])

        w_rt = jnp.concatenate([w_group[l], w_router[l],
                                jnp.zeros((d, LANES - N_GROUPS - N_EXPERTS), F32)], axis=1)
        b_rt = jnp.concatenate([b_group[l], b_router[l],
                                jnp.zeros((LANES - N_GROUPS - N_EXPERTS,), F32)])[None, :]
        ids, wts = _router(xf, w_rt, b_rt)
        dest, src_tok, tile_expert, n_active = _dispatch_plan(ids[:, :TOPK_EXPERT], EXP_TM, n_tiles)
        x_chunks = [sc_take(xb, c) for c in jnp.split(src_tok, EXP_CHUNKS)]
        y = _experts(x_chunks, tile_expert, n_active, w_gate, w_up, w_down, l)
        y_groups = [(take(y, d[:, 0]), take(y, d[:, 1])) for d in jnp.split(dest, COMBINE_SPLIT)]
        xf, xb = _moe_ln(xf, y_groups, wts, ln2_g[l][None, :], ln2_b[l][None, :])
    return xf.reshape(batch, seq, d)
```

```python
import functools

import numpy as np
import jax
import jax.numpy as jnp
from jax import lax
from jax.experimental import pallas as pl
from jax.experimental.pallas import tpu as pltpu
from jax.experimental.compute_on import compute_on

F32 = jnp.float32
BF16 = jnp.bfloat16

D_MODEL = 2048
DEPTH = 2
H_A, KV_A, HD_A = 8, 2, 128
HPG_A = H_A // KV_A
L_CMP, STRIDE_CMP, L_SEL, N_SEL, WIN_A = 32, 16, 64, 8, 512
H_B, KV_B, HD_B, WIN_B = 8, 2, 64, 128
HPG_B = H_B // KV_B
H_C, HD_C, BLK_C, TOPK_C = 4, 128, 256, 3
N_GROUPS, EXP_PER_GROUP, D_EXPERT, TOPK_EXPERT = 4, 8, 256, 2
N_EXPERTS = N_GROUPS * EXP_PER_GROUP
LN_EPS = 1e-5
NEG_INF = -1e30
ALPHA = (2.0 * DEPTH) ** 0.25

LANES = 128
SUBLANES = 8
VMEM_LIMIT = 48 * 1024 * 1024
BIG = 2.0 ** 100

Q_A_W = H_A * HD_A
KV_A_W = 3 * 2 * KV_A * HD_A
GATE_A_W = 3 * H_A
Q_B_W = H_B * HD_B
KV_B_W = KV_B * HD_B
C_W = H_C * HD_C
MERGE_W = 3 * D_MODEL
D_IN = Q_A_W + KV_A_W + GATE_A_W + Q_B_W + 2 * KV_B_W + 3 * C_W + MERGE_W
OFF_MERGE = 0
OFF_QA = OFF_MERGE + MERGE_W
OFF_QB = OFF_QA + Q_A_W
OFF_QC = OFF_QB + Q_B_W
OFF_KC = OFF_QC + C_W
OFF_VC = OFF_KC + C_W
OFF_KVA = OFF_VC + C_W
OFF_KB = OFF_KVA + KV_A_W
OFF_VB = OFF_KB + KV_B_W
OFF_GATE_A = OFF_VB + KV_B_W
GATE_A_PAD = 256
Z_W = OFF_GATE_A + GATE_A_PAD
assert OFF_QA % (HPG_A * HD_A) == 0 and OFF_QB % Q_B_W == 0 and OFF_MERGE % D_MODEL == 0
assert OFF_QC % C_W == 0 and OFF_KC % C_W == 0 and OFF_VC % C_W == 0

AUG_POS_HI, AUG_POS_LO, AUG_ONE, AUG_PAD = 32, 33, 34, 35
POS_SPLIT = 256

RELAYOUT_COLS = 256
MM_TM, MM_TN = 2048, 1024
ROW_TM = 256
EXP_TM = 256
EXP_CHUNKS = 3
ATT_TQ = 256
SWA_TQ = 128
COMBINE_SPLIT = 2


def _cparams(*sem):
    return pltpu.CompilerParams(dimension_semantics=sem, vmem_limit_bytes=VMEM_LIMIT)


def _dot(a, b):
    return jnp.dot(a, b, preferred_element_type=F32)


def _dot_nt(a, b):
    return lax.dot_general(a, b, (((1,), (1,)), ((), ())), preferred_element_type=F32)


def _iota(shape, dim):
    return lax.broadcasted_iota(jnp.int32, shape, dim)


def _split3(x):
    hi = x.astype(BF16)
    r1 = x - hi.astype(F32)
    mid = r1.astype(BF16)
    lo = (r1 - mid.astype(F32)).astype(BF16)
    return hi, mid, lo


def _pick_lane(x, lane_idx):
    lane = _iota(x.shape, 1)
    return jnp.sum(jnp.where(lane == lane_idx, x, 0.0), axis=-1, keepdims=True)


def _topk_rows(vals, k):
    row = _iota(vals.shape, 0).astype(F32)
    sel = jnp.zeros(vals.shape, F32)
    for _ in range(k):
        m = jnp.max(vals, axis=0, keepdims=True)
        idx = jnp.min(jnp.where(vals == m, row, float(LANES)), axis=0, keepdims=True)
        pick = row == idx
        sel = jnp.where(pick, 1.0, sel)
        vals = jnp.where(pick, -jnp.inf, vals)
    return sel


def _rows_to_lanes(x_t, tq):
    pad = jnp.zeros((LANES - x_t.shape[0], tq), F32)
    return jnp.concatenate([x_t, pad], axis=0).T


def _query_aug(base, slope, center, pad_flag=False):
    lane = _iota(base.shape, 1)
    aug = jnp.where(lane == AUG_POS_HI, slope * float(POS_SPLIT), base)
    aug = jnp.where(lane == AUG_POS_LO, slope, aug)
    aug = jnp.where(lane == AUG_ONE, -slope * center, aug)
    if pad_flag:
        aug = jnp.where(lane == AUG_PAD, 1.0, aug)
    return aug.astype(BF16)


def _softmax_update(s, v, carry):
    m, l, acc = carry
    m_new = jnp.maximum(m, jnp.max(s, axis=-1, keepdims=True))
    alpha = jnp.exp(m - m_new)
    p = jnp.exp(s - m_new)
    l = alpha * l + jnp.sum(p, axis=-1, keepdims=True)
    acc = alpha * acc + _dot(p.astype(BF16), v)
    return m_new, l, acc


def _softmax_init(rows, hd):
    return (jnp.full((rows, 1), NEG_INF, F32), jnp.zeros((rows, 1), F32), jnp.zeros((rows, hd), F32))


def _z_column_sources():
    s_kva = Q_A_W
    s_gate = s_kva + KV_A_W
    s_qb = s_gate + GATE_A_W
    s_kb = s_qb + Q_B_W
    s_qc = s_kb + 2 * KV_B_W
    s_merge = s_qc + 3 * C_W
    assert GATE_A_PAD == RELAYOUT_COLS
    segments = [(OFF_MERGE, s_merge, MERGE_W), (OFF_QA, 0, Q_A_W), (OFF_QB, s_qb, Q_B_W),
                (OFF_QC, s_qc, 3 * C_W), (OFF_KVA, s_kva, KV_A_W), (OFF_KB, s_kb, 2 * KV_B_W)]
    src = np.zeros((Z_W // RELAYOUT_COLS,), np.int32)
    valid = np.zeros((Z_W // RELAYOUT_COLS,), np.int32)
    for dst, start, width in segments:
        assert dst % RELAYOUT_COLS == 0 and width % RELAYOUT_COLS == 0
        for b in range(width // RELAYOUT_COLS):
            src[dst // RELAYOUT_COLS + b] = start + b * RELAYOUT_COLS
            valid[dst // RELAYOUT_COLS + b] = RELAYOUT_COLS
    src[OFF_GATE_A // RELAYOUT_COLS] = s_gate
    valid[OFF_GATE_A // RELAYOUT_COLS] = GATE_A_W
    assert np.all(src % SUBLANES == 0) and np.all(src + RELAYOUT_COLS <= D_IN)
    return jnp.asarray(src // SUBLANES), jnp.asarray(valid)


def _prep_w_in_kernel(src_ref, valid_ref, w_ref, o_ref):
    j = pl.program_id(0)
    w = w_ref[0]
    w = jnp.where(_iota(w.shape, 0) < valid_ref[j], w, 0.0)
    o_ref[...] = w.T.astype(BF16)


def _prep_w_in(w_in_t, layer):
    src, valid = _z_column_sources()
    grid_spec = pltpu.PrefetchScalarGridSpec(
        num_scalar_prefetch=2,
        grid=(Z_W // RELAYOUT_COLS,),
        in_specs=[pl.BlockSpec((pl.Element(1), pl.Element(RELAYOUT_COLS), pl.Element(D_MODEL)),
                               lambda j, src, valid: (layer, src[j] * SUBLANES, 0))],
        out_specs=pl.BlockSpec((D_MODEL, RELAYOUT_COLS), lambda j, src, valid: (0, j)),
    )
    return pl.pallas_call(
        _prep_w_in_kernel,
        grid_spec=grid_spec,
        out_shape=jax.ShapeDtypeStruct((D_MODEL, Z_W), BF16),
        compiler_params=_cparams("parallel"),
        name="w_in_relayout",
    )(src, valid, w_in_t)


def _mm_kernel(a_ref, b_ref, o_ref):
    o_ref[...] = _dot(a_ref[...], b_ref[...]).astype(o_ref.dtype)


def _matmul(a, b, out_dtype):
    m, k = a.shape
    n = b.shape[1]
    return pl.pallas_call(
        _mm_kernel,
        grid=(m // MM_TM, n // MM_TN),
        in_specs=[pl.BlockSpec((MM_TM, k), lambda i, j: (i, 0)),
                  pl.BlockSpec((k, MM_TN), lambda i, j: (0, j))],
        out_specs=pl.BlockSpec((MM_TM, MM_TN), lambda i, j: (i, j)),
        out_shape=jax.ShapeDtypeStruct((m, n), out_dtype),
        compiler_params=_cparams("parallel", "parallel"),
        name="in_proj",
    )(a, b)


def _cmp_kv_kernel(k_ref, pos_ref, w_ref, o_ref, kf_ref, *, seq):
    kf_ref[0:seq, :] = k_ref[...].astype(F32)
    kf_ref[seq:seq + LANES, :] = jnp.zeros((LANES, HD_A), F32)
    acc = jnp.zeros((LANES, HD_A), F32)
    for l in range(L_CMP):
        rows = kf_ref[pl.ds(l, LANES, stride=STRIDE_CMP), :] + pos_ref[0, 0, l:l + 1, :]
        acc = acc + _dot(rows.astype(BF16), w_ref[0, 0, l].astype(BF16))
    o_ref[0, 0] = acc.astype(o_ref.dtype)


def _cmp_kv(z, cmp_pos, cmp_w, layer, batch, seq):
    blk0 = OFF_KVA // LANES
    return pl.pallas_call(
        functools.partial(_cmp_kv_kernel, seq=seq),
        grid=(batch, 2 * KV_A),
        in_specs=[pl.BlockSpec((seq, LANES), lambda b, j: (b, blk0 + j)),
                  pl.BlockSpec((1, 1, L_CMP, HD_A), lambda b, j: (layer, j // KV_A, 0, 0)),
                  pl.BlockSpec((1, 1, L_CMP, HD_A, HD_A), lambda b, j: (layer, j // KV_A, 0, 0, 0))],
        out_specs=pl.BlockSpec((1, 1, LANES, HD_A), lambda b, j: (b, j, 0, 0)),
        out_shape=jax.ShapeDtypeStruct((batch, 2 * KV_A, LANES, HD_A), BF16),
        scratch_shapes=[pltpu.VMEM((seq + LANES, HD_A), F32)],
        compiler_params=_cparams("parallel", "parallel"),
        name="nsa_cmp_kv",
    )(z, cmp_pos, cmp_w)


def _nsa_cmp_kernel(q_ref, kvc_ref, gate_ref, inter_t_ref, o_ref, nsel_ref, *, tq):
    i = pl.program_id(1)
    shape = (tq, LANES)
    t = i * tq + _iota(shape, 0)
    lane = _iota(shape, 1)
    dist_i = t - (lane * STRIDE_CMP + (L_CMP - 1))
    ok = dist_i >= 0
    okf = ok.astype(F32)
    dist = dist_i.astype(F32)
    sig = jax.nn.sigmoid(gate_ref[...].astype(F32))
    scale = HD_A ** -0.5
    n_sel_rows = 32
    inter_t = inter_t_ref[...]
    shape_t = (n_sel_rows, tq)
    j = _iota(shape_t, 0)
    blk_t = (i * tq + _iota(shape_t, 1)) // L_SEL
    valid = j <= blk_t
    forced = (j == 0) | (j == blk_t) | (j == blk_t - 1)
    for g in range(KV_A):
        kc = kvc_ref[0, g]
        vc = kvc_ref[0, KV_A + g]
        psum = jnp.zeros(shape, F32)
        outs = []
        for h in range(HPG_A):
            head = g * HPG_A + h
            q = q_ref[:, head * HD_A:(head + 1) * HD_A]
            s = _dot_nt(q, kc) * scale - (2.0 ** (-8.0 * (head + 1) / H_A)) * dist
            s = jnp.where(ok, s, NEG_INF)
            e = jnp.exp(s - jnp.max(s, axis=-1, keepdims=True))
            p = e / jnp.sum(e, axis=-1, keepdims=True) * okf
            psum = psum + p
            o = _dot(p.astype(BF16), vc)
            outs.append(o * sig[:, head * 3:head * 3 + 1])
        o_ref[:, g * HPG_A * HD_A:(g + 1) * HPG_A * HD_A] = jnp.concatenate(outs, axis=1).astype(o_ref.dtype)

        imp_t = sum(_dot_nt(inter_t, part) for part in _split3(psum))[0:n_sel_rows]
        vals = jnp.where(forced, jnp.inf, jnp.where(valid, imp_t, -jnp.inf))
        sel = _topk_rows(vals, N_SEL)
        not_selected = jnp.where(valid & (sel > 0.5), 0.0, 1.0)
        nsel_ref[0, g] = _rows_to_lanes(not_selected, tq).astype(nsel_ref.dtype)


def _nsa_cmp(z, kvc, inter_t, batch, seq):
    tq = ATT_TQ
    nq = seq // tq
    gate_blk = OFF_GATE_A // LANES
    return pl.pallas_call(
        functools.partial(_nsa_cmp_kernel, tq=tq),
        grid=(batch, nq),
        in_specs=[pl.BlockSpec((tq, Q_A_W), lambda b, i: (b * nq + i, OFF_QA // Q_A_W)),
                  pl.BlockSpec((1, 2 * KV_A, LANES, HD_A), lambda b, i: (b, 0, 0, 0)),
                  pl.BlockSpec((tq, LANES), lambda b, i: (b * nq + i, gate_blk)),
                  pl.BlockSpec((LANES, LANES), lambda b, i: (0, 0))],
        out_specs=[pl.BlockSpec((tq, Q_A_W), lambda b, i: (b * nq + i, 0)),
                   pl.BlockSpec((1, KV_A, tq, LANES), lambda b, i: (b, 0, i, 0))],
        out_shape=[jax.ShapeDtypeStruct((batch * seq, Q_A_W), BF16),
                   jax.ShapeDtypeStruct((batch, KV_A, seq, LANES), BF16)],
        compiler_params=_cparams("parallel", "parallel"),
        name="nsa_cmp_attn",
    )(z, kvc, z, inter_t)


def _scaled_q(q_ref, h, hd, scale):
    return (q_ref[:, h * hd:(h + 1) * hd].astype(F32) * scale).astype(BF16)


def _gated_heads(o, sig, g, branch, tq):
    outs = [o[h * tq:(h + 1) * tq] * _pick_lane(sig, (g * HPG_A + h) * 3 + branch) for h in range(HPG_A)]
    return jnp.concatenate(outs, axis=1)


def _nsa_slc_kernel(slopes_ref, cnt_ref, lst_ref, q_ref, k_ref, v_ref, kaug_ref, causal_ref, nsel_ref, gate_ref,
                    o_ref, *, tq):
    g = pl.program_id(1)
    i = pl.program_id(2)
    nq = pl.num_programs(2)
    plan = (pl.program_id(0) * KV_A + g) * nq + i
    rows = HPG_A * tq
    scale = HD_A ** -0.5
    nsel = nsel_ref[0, 0].astype(F32)
    center = (i * tq).astype(F32)
    qx = jnp.concatenate(
        [jnp.concatenate([_scaled_q(q_ref, h, HD_A, scale),
                          _query_aug(nsel, slopes_ref[g * HPG_A + h], center)], axis=1)
         for h in range(HPG_A)], axis=0)

    def scores(kt):
        k0 = pl.multiple_of(kt * tq, tq)
        kx = jnp.concatenate([k_ref[pl.ds(k0, tq), :], kaug_ref[pl.ds(k0, tq), :]], axis=1)
        return _dot_nt(qx, kx)

    def values(kt):
        return v_ref[pl.ds(pl.multiple_of(kt * tq, tq), tq), :]

    def body(n, carry):
        kt = lst_ref[plan * nq + n]
        return _softmax_update(scores(kt), values(kt), carry)

    m, l, acc = lax.fori_loop(0, cnt_ref[plan], body, _softmax_init(rows, HD_A))
    causal_bias = jnp.concatenate([causal_ref[...]] * HPG_A, axis=0)
    _, l, acc = _softmax_update(scores(i) + causal_bias, values(i), (m, l, acc))
    sig = jax.nn.sigmoid(gate_ref[...].astype(F32))
    o_ref[...] = _gated_heads(acc / l, sig, g, 1, tq).astype(o_ref.dtype)


def _slc_tile_plan(nsel, tq):
    b, g, s, _ = nsel.shape
    nq = s // tq
    per = tq // L_SEL
    chosen = (nsel[..., :nq * per].astype(F32) < 0.5).reshape(b, g, nq, tq, nq, per)
    need = jnp.any(chosen, axis=(3, 5))
    tile = jnp.arange(nq, dtype=jnp.int32)
    need = need & (tile[None, :] < tile[:, None])
    pos = jnp.cumsum(need.astype(jnp.int32), axis=-1) - 1
    hit = need[..., None] & (pos[..., None] == tile)
    lst = jnp.sum(jnp.where(hit, tile[:, None], 0), axis=-2)
    cnt = jnp.sum(need.astype(jnp.int32), axis=-1)
    return cnt.reshape(-1), lst.reshape(-1).astype(jnp.int32)


def _nsa_slc(z, nsel, kaug, slopes, batch, seq):
    tq = ATT_TQ
    nq = seq // tq
    cnt, lst = _slc_tile_plan(nsel, tq)
    qw = HPG_A * HD_A
    kblk = OFF_KVA // LANES + 1 * 2 * KV_A
    vblk = kblk + KV_A
    gate_blk = OFF_GATE_A // LANES
    causal = jnp.asarray(np.where(np.arange(tq)[:, None] >= np.arange(tq)[None, :], 0.0, -BIG).astype(np.float32))
    return pl.pallas_call(
        functools.partial(_nsa_slc_kernel, tq=tq),
        grid=(batch, KV_A, nq),
        in_specs=[pl.BlockSpec(memory_space=pltpu.SMEM),
                  pl.BlockSpec(memory_space=pltpu.SMEM),
                  pl.BlockSpec(memory_space=pltpu.SMEM),
                  pl.BlockSpec((tq, qw), lambda b, g, i: (b * nq + i, OFF_QA // qw + g)),
                  pl.BlockSpec((seq, LANES), lambda b, g, i: (b, kblk + g)),
                  pl.BlockSpec((seq, LANES), lambda b, g, i: (b, vblk + g)),
                  pl.BlockSpec((seq, LANES), lambda b, g, i: (0, 0)),
                  pl.BlockSpec((tq, tq), lambda b, g, i: (0, 0)),
                  pl.BlockSpec((1, 1, tq, LANES), lambda b, g, i: (b, g, i, 0)),
                  pl.BlockSpec((tq, LANES), lambda b, g, i: (b * nq + i, gate_blk))],
        out_specs=pl.BlockSpec((tq, qw), lambda b, g, i: (b * nq + i, g)),
        out_shape=jax.ShapeDtypeStruct((batch * seq, Q_A_W), BF16),
        compiler_params=_cparams("parallel", "parallel", "parallel"),
        name="nsa_slc_attn",
    )(slopes, cnt, lst, z, z, z, kaug, causal, nsel, z)


def _nsa_win_kernel(slopes_ref, q_ref, k_ref, v_ref, kaug_ref, band_ref, gate_ref, o_ref, kx_ref, vx_ref, *, tq, window, seq):
    g = pl.program_id(1)
    i = pl.program_id(2)

    @pl.when(i == 0)
    def _():
        kx_ref[0:window, 0:HD_A] = jnp.zeros((window, HD_A), BF16)
        kx_ref[window:window + seq, 0:HD_A] = k_ref[...]
        kx_ref[:, HD_A:2 * HD_A] = kaug_ref[...]
        vx_ref[0:window, :] = jnp.zeros((window, HD_A), BF16)
        vx_ref[window:window + seq, :] = v_ref[...]

    span = window + tq
    r0 = pl.multiple_of(i * tq, tq)
    center = (i * tq + window).astype(F32)
    base = jnp.zeros((tq, LANES), F32)
    sig = jax.nn.sigmoid(gate_ref[...].astype(F32))
    pair = 2
    band_bias = jnp.concatenate([band_ref[...]] * pair, axis=0)
    for h0 in range(0, HPG_A, pair):
        qx = jnp.concatenate(
            [jnp.concatenate([_scaled_q(q_ref, h, HD_A, HD_A ** -0.5),
                              _query_aug(base, slopes_ref[g * HPG_A + h], center, pad_flag=True)], axis=1)
             for h in range(h0, h0 + pair)], axis=0)
        s = _dot_nt(qx, kx_ref[pl.ds(r0, span), :]) + band_bias
        p = jnp.exp(s - jnp.max(s, axis=-1, keepdims=True))
        o = _dot(p.astype(BF16), vx_ref[pl.ds(r0, span), :]) / jnp.sum(p, axis=-1, keepdims=True)
        for h in range(h0, h0 + pair):
            gated = o[(h - h0) * tq:(h - h0 + 1) * tq] * _pick_lane(sig, (g * HPG_A + h) * 3 + 2)
            o_ref[:, h * HD_A:(h + 1) * HD_A] = gated.astype(o_ref.dtype)


def _band_bias(tq, window):
    r = np.arange(tq)[:, None]
    c = np.arange(window + tq)[None, :]
    return jnp.asarray(np.where((c > r) & (c <= r + window), 0.0, -BIG).astype(np.float32))


def _nsa_win(z, kaug_pad, slopes, batch, seq):
    tq = ATT_TQ
    nq = seq // tq
    qw = HPG_A * HD_A
    kblk = OFF_KVA // LANES + 2 * 2 * KV_A
    vblk = kblk + KV_A
    gate_blk = OFF_GATE_A // LANES
    return pl.pallas_call(
        functools.partial(_nsa_win_kernel, tq=tq, window=WIN_A, seq=seq),
        grid=(batch, KV_A, nq),
        in_specs=[pl.BlockSpec(memory_space=pltpu.SMEM),
                  pl.BlockSpec((tq, qw), lambda b, g, i: (b * nq + i, OFF_QA // qw + g)),
                  pl.BlockSpec((seq, LANES), lambda b, g, i: (b, kblk + g)),
                  pl.BlockSpec((seq, LANES), lambda b, g, i: (b, vblk + g)),
                  pl.BlockSpec((WIN_A + seq, LANES), lambda b, g, i: (0, 0)),
                  pl.BlockSpec((tq, WIN_A + tq), lambda b, g, i: (0, 0)),
                  pl.BlockSpec((tq, LANES), lambda b, g, i: (b * nq + i, gate_blk))],
        out_specs=pl.BlockSpec((tq, qw), lambda b, g, i: (b * nq + i, g)),
        out_shape=jax.ShapeDtypeStruct((batch * seq, Q_A_W), BF16),
        scratch_shapes=[pltpu.VMEM((WIN_A + seq, 2 * HD_A), BF16), pltpu.VMEM((WIN_A + seq, HD_A), BF16)],
        compiler_params=_cparams("parallel", "parallel", "arbitrary"),
        name="nsa_win_attn",
    )(slopes, z, z, z, kaug_pad, _band_bias(tq, WIN_A), z)


SW_POS_HI, SW_POS_LO, SW_ONE, SW_PAD = 0, 1, 2, 3


def _swa_kernel(sink_ref, q_ref, k_ref, v_ref, kaug_ref, band_ref, o_ref, kx_ref, vx_ref, *, tq, window, seq):
    i = pl.program_id(1)

    @pl.when(i == 0)
    def _():
        for g in range(KV_B):
            kx_ref[g, 0:window, 0:HD_B] = jnp.zeros((window, HD_B), BF16)
            kx_ref[g, window:window + seq, 0:HD_B] = k_ref[:, g * HD_B:(g + 1) * HD_B]
            kx_ref[g, :, HD_B:2 * HD_B] = kaug_ref[...]
            vx_ref[g, 0:window, :] = jnp.zeros((window, HD_B), BF16)
            vx_ref[g, window:window + seq, :] = v_ref[:, g * HD_B:(g + 1) * HD_B]

    span = window + tq
    rows = HPG_B * tq
    r0 = pl.multiple_of(i * tq, tq)
    center = (i * tq + window).astype(F32)
    band_bias = jnp.concatenate([band_ref[...]] * HPG_B, axis=0)
    lane = _iota((tq, HD_B), 1)
    hh = _iota((rows, 1), 0) // tq
    row_in_tile = (_iota((rows, 1), 0) - hh * tq).astype(F32)

    def head_column(values):
        col = jnp.full((rows, 1), values[HPG_B - 1], F32)
        for h in range(HPG_B - 2, -1, -1):
            col = jnp.where(hh == h, values[h], col)
        return col

    outs = []
    for g in range(KV_B):
        slopes = [2.0 ** (-8.0 * (g * HPG_B + h + 1) / H_B) for h in range(HPG_B)]
        parts = []
        for h in range(HPG_B):
            aug = jnp.where(lane == SW_POS_HI, slopes[h] * POS_SPLIT, jnp.where(lane == SW_POS_LO, slopes[h], 0.0))
            aug = jnp.where(lane == SW_ONE, -slopes[h] * center, jnp.where(lane == SW_PAD, 1.0, aug))
            parts.append(jnp.concatenate([_scaled_q(q_ref, g * HPG_B + h, HD_B, HD_B ** -0.5),
                                          aug.astype(BF16)], axis=1))
        qx = jnp.concatenate(parts, axis=0)
        s = _dot_nt(qx, kx_ref[g, pl.ds(r0, span), :]) + band_bias
        sink_shifted = (head_column([sink_ref[g * HPG_B + h] for h in range(HPG_B)])
                        + head_column(slopes) * row_in_tile)
        m = jnp.maximum(jnp.max(s, axis=-1, keepdims=True), sink_shifted)
        p = jnp.exp(s - m)
        denom = jnp.sum(p, axis=-1, keepdims=True) + jnp.exp(sink_shifted - m)
        o = _dot(p.astype(BF16), vx_ref[g, pl.ds(r0, span), :]) / denom
        outs += [o[h * tq:(h + 1) * tq] for h in range(HPG_B)]
    o_ref[...] = jnp.concatenate(outs, axis=1).astype(o_ref.dtype)


def _swa(z, kaug_sw, sink, layer, batch, seq):
    tq = SWA_TQ
    nq = seq // tq
    return pl.pallas_call(
        functools.partial(_swa_kernel, tq=tq, window=WIN_B, seq=seq),
        grid=(batch, nq),
        in_specs=[pl.BlockSpec(memory_space=pltpu.SMEM),
                  pl.BlockSpec((tq, Q_B_W), lambda b, i: (b * nq + i, OFF_QB // Q_B_W)),
                  pl.BlockSpec((seq, KV_B_W), lambda b, i: (b, OFF_KB // KV_B_W)),
                  pl.BlockSpec((seq, KV_B_W), lambda b, i: (b, OFF_VB // KV_B_W)),
                  pl.BlockSpec((WIN_B + seq, HD_B), lambda b, i: (0, 0)),
                  pl.BlockSpec((tq, WIN_B + tq), lambda b, i: (0, 0))],
        out_specs=pl.BlockSpec((tq, Q_B_W), lambda b, i: (b * nq + i, 0)),
        out_shape=jax.ShapeDtypeStruct((batch * seq, Q_B_W), BF16),
        scratch_shapes=[pltpu.VMEM((KV_B, WIN_B + seq, 2 * HD_B), BF16),
                        pltpu.VMEM((KV_B, WIN_B + seq, HD_B), BF16)],
        compiler_params=_cparams("parallel", "arbitrary"),
        name="swa_attn",
    )(sink[layer], z, z, z, kaug_sw, _band_bias(tq, WIN_B))


def _moba_kernel(slopes_ref, q_ref, k_ref, v_ref, kaug_ref, avg_ref, o_ref, km_ref, *, seq):
    i = pl.program_id(1)
    tq = BLK_C
    n_blk = seq // BLK_C
    scale = HD_C ** -0.5

    @pl.when(i == 0)
    def _():
        for h in range(H_C):
            k_mean = _dot(avg_ref[...], k_ref[:, h * HD_C:(h + 1) * HD_C])[0:SUBLANES]
            terms = [t.astype(F32) for t in _split3(k_mean)] + [jnp.zeros((SUBLANES, HD_C), F32)]
            km_ref[h] = jnp.concatenate(terms, axis=0).astype(BF16)

    center = (i * tq).astype(F32)
    blk = _iota((SUBLANES, tq), 0)
    past = blk < i
    qx = []
    for h in range(H_C):
        q = q_ref[:, h * HD_C:(h + 1) * HD_C]
        sc = _dot_nt(km_ref[h], q)
        score_t = sc[0:SUBLANES] + sc[SUBLANES:2 * SUBLANES] + sc[2 * SUBLANES:3 * SUBLANES]
        sel = _topk_rows(jnp.where(past, score_t, -jnp.inf), TOPK_C)
        not_selected = jnp.where(past & (sel < 0.5), 1.0, 0.0)
        qx.append(jnp.concatenate([_scaled_q(q_ref, h, HD_C, scale),
                                   _query_aug(_rows_to_lanes(not_selected, tq), slopes_ref[h], center)], axis=1))
    assert n_blk <= SUBLANES

    def scores(h, kt):
        k0 = pl.multiple_of(kt * tq, tq)
        kx = jnp.concatenate([k_ref[pl.ds(k0, tq), h * HD_C:(h + 1) * HD_C], kaug_ref[pl.ds(k0, tq), :]], axis=1)
        return _dot_nt(qx[h], kx)

    def values(h, kt):
        return v_ref[pl.ds(pl.multiple_of(kt * tq, tq), tq), h * HD_C:(h + 1) * HD_C]

    def body(kt, carry):
        out = []
        for h in range(H_C):
            s, m, l, acc = carry[h]
            s_next = scores(h, kt + 1)
            out.append((s_next,) + _softmax_update(s, values(h, kt), (m, l, acc)))
        return tuple(out)

    init = tuple((scores(h, 0),) + _softmax_init(tq, HD_C) for h in range(H_C))
    carry = lax.fori_loop(0, i, body, init)
    causal = _iota((tq, tq), 0) >= _iota((tq, tq), 1)
    outs = []
    for h in range(H_C):
        s, m, l, acc = carry[h]
        _, l, acc = _softmax_update(jnp.where(causal, s, -BIG), values(h, i), (m, l, acc))
        outs.append(acc / l)
    o_ref[...] = jnp.concatenate(outs, axis=1).astype(o_ref.dtype)


def _moba(z, kaug, avg, slopes, batch, seq):
    nq = seq // BLK_C
    return pl.pallas_call(
        functools.partial(_moba_kernel, seq=seq),
        grid=(batch, nq),
        in_specs=[pl.BlockSpec(memory_space=pltpu.SMEM),
                  pl.BlockSpec((BLK_C, C_W), lambda b, i: (b * nq + i, OFF_QC // C_W)),
                  pl.BlockSpec((seq, C_W), lambda b, i: (b, OFF_KC // C_W)),
                  pl.BlockSpec((seq, C_W), lambda b, i: (b, OFF_VC // C_W)),
                  pl.BlockSpec((seq, LANES), lambda b, i: (0, 0)),
                  pl.BlockSpec((2 * SUBLANES, seq), lambda b, i: (0, 0))],
        out_specs=pl.BlockSpec((BLK_C, C_W), lambda b, i: (b * nq + i, 0)),
        out_shape=jax.ShapeDtypeStruct((batch * seq, C_W), BF16),
        scratch_shapes=[pltpu.VMEM((H_C, 4 * SUBLANES, HD_C), BF16)],
        compiler_params=_cparams("parallel", "arbitrary"),
        name="moba_attn",
    )(slopes, z, z, z, kaug, avg)


def _merge_kernel(oc_ref, os_ref, ow_ref, ob_ref, om_ref, g0_ref, g1_ref, g2_ref, wa_ref, wb_ref, wc_ref, o_ref):
    o_a = (oc_ref[...].astype(F32) + os_ref[...].astype(F32) + ow_ref[...].astype(F32)).astype(BF16)
    merged = jax.nn.sigmoid(g0_ref[...].astype(F32)) * _dot(o_a, wa_ref[...])
    merged = merged + jax.nn.sigmoid(g1_ref[...].astype(F32)) * _dot(ob_ref[...], wb_ref[...])
    merged = merged + jax.nn.sigmoid(g2_ref[...].astype(F32)) * _dot(om_ref[...], wc_ref[...])
    o_ref[...] = merged.astype(o_ref.dtype)


def _merge(o_cmp, o_slc, o_win, o_b, o_c, z, wa, wb, wc):
    tm = 2 * ROW_TM
    t = z.shape[0]
    gblk = OFF_MERGE // D_MODEL
    row = lambda w: pl.BlockSpec((tm, w), lambda i: (i, 0))
    full = lambda a: pl.BlockSpec(a.shape, lambda i: (0, 0))
    gate = lambda r: pl.BlockSpec((tm, D_MODEL), lambda i: (i, gblk + r))
    return pl.pallas_call(
        _merge_kernel,
        grid=(t // tm,),
        in_specs=[row(Q_A_W), row(Q_A_W), row(Q_A_W), row(Q_B_W), row(C_W),
                  gate(0), gate(1), gate(2), full(wa), full(wb), full(wc)],
        out_specs=row(D_MODEL),
        out_shape=jax.ShapeDtypeStruct((t, D_MODEL), BF16),
        compiler_params=_cparams("parallel"),
        name="mixer_merge",
    )(o_cmp, o_slc, o_win, o_b, o_c, z, z, z, wa, wb, wc)


def _layer_norm(h, g_ref, b_ref):
    mu = jnp.mean(h, axis=-1, keepdims=True)
    xc = h - mu
    var = jnp.mean(xc * xc, axis=-1, keepdims=True)
    return xc * lax.rsqrt(var + LN_EPS) * g_ref[...] + b_ref[...]


def _proj_ln_kernel(m_ref, w_ref, x_ref, g_ref, b_ref, wrt_ref, brt_ref, xo_ref, xb_ref, id_ref, wt_ref):
    y = _dot(m_ref[...], w_ref[...])
    out = _layer_norm(ALPHA * x_ref[...] + y, g_ref, b_ref)
    xo_ref[...] = out
    xb_ref[...] = out.astype(BF16)
    _route(out, wrt_ref, brt_ref, id_ref, wt_ref)


def _proj_ln(merged, w_out, x, g, b, w_rt, b_rt):
    tm = 2 * ROW_TM
    t = x.shape[0]
    row = pl.BlockSpec((tm, D_MODEL), lambda i: (i, 0))
    vec = pl.BlockSpec((1, D_MODEL), lambda i: (0, 0))
    slab = pl.BlockSpec((tm, LANES), lambda i: (i, 0))
    return pl.pallas_call(
        _proj_ln_kernel,
        grid=(t // tm,),
        in_specs=[row, pl.BlockSpec((D_MODEL, D_MODEL), lambda i: (0, 0)), row, vec, vec,
                  pl.BlockSpec((D_MODEL, LANES), lambda i: (0, 0)), pl.BlockSpec((1, LANES), lambda i: (0, 0))],
        out_specs=[row, row, slab, slab],
        out_shape=[jax.ShapeDtypeStruct((t, D_MODEL), F32), jax.ShapeDtypeStruct((t, D_MODEL), BF16),
                   jax.ShapeDtypeStruct((t, LANES), jnp.int32), jax.ShapeDtypeStruct((t, LANES), F32)],
        compiler_params=_cparams("parallel"),
        name="out_proj_ln_route",
    )(merged, w_out, x, g, b, w_rt, b_rt)


def _moe_ln_kernel(x_ref, ya_ref, yb_ref, w_ref, g_ref, b_ref, *rest):
    xo_ref, xb_ref = rest[-2:]
    w = w_ref[...]
    y = w[:, 0:1] * ya_ref[...].astype(F32) + w[:, 1:2] * yb_ref[...].astype(F32)
    out = _layer_norm(ALPHA * x_ref[...] + y, g_ref, b_ref)
    xo_ref[...] = out
    xb_ref[...] = out.astype(BF16)


def _moe_ln(x, y_groups, wts, g, b):
    tm = ROW_TM
    t = x.shape[0]
    blocks = t // tm // len(y_groups)
    vec = pl.BlockSpec((1, D_MODEL), lambda i: (0, 0))
    group_row = pl.BlockSpec((tm, D_MODEL), lambda i: (i, 0))
    outs = None
    for n, (ya, yb) in enumerate(y_groups):
        off = n * blocks
        row = pl.BlockSpec((tm, D_MODEL), lambda i, off=off: (off + i, 0))
        in_specs = [row, group_row, group_row, pl.BlockSpec((tm, LANES), lambda i, off=off: (off + i, 0)), vec, vec]
        args = [x, ya, yb, wts, g, b]
        aliases = {}
        if outs is not None:
            in_specs += [pl.BlockSpec(memory_space=pl.ANY)] * 2
            aliases = {len(args): 0, len(args) + 1: 1}
            args += list(outs)
        outs = pl.pallas_call(
            _moe_ln_kernel,
            grid=(blocks,),
            in_specs=in_specs,
            out_specs=[row, row],
            out_shape=[jax.ShapeDtypeStruct((t, D_MODEL), F32), jax.ShapeDtypeStruct((t, D_MODEL), BF16)],
            input_output_aliases=aliases,
            compiler_params=_cparams("parallel"),
            name="moe_combine_ln",
        )(*args)
    return outs


def _route(x, w_ref, b_ref, id_ref, wt_ref):
    x_hi = x.astype(BF16)
    x_lo = (x - x_hi.astype(F32)).astype(BF16)
    w = w_ref[...]
    w_hi = w.astype(BF16)
    w_lo = (w - w_hi.astype(F32)).astype(BF16)
    logits = _dot(x_hi, w_hi) + _dot(x_lo, w_hi) + _dot(x_hi, w_lo) + b_ref[...]
    lane = _iota(logits.shape, 1).astype(F32)
    first = lambda hit: jnp.min(jnp.where(hit, lane, float(LANES)), axis=-1, keepdims=True)
    gl = jnp.where(lane < N_GROUPS, logits, -jnp.inf)
    gm = jnp.max(gl, axis=-1, keepdims=True)
    g_w = 1.0 / jnp.sum(jnp.exp(gl - gm), axis=-1, keepdims=True)
    lo = N_GROUPS + first(gl == gm) * EXP_PER_GROUP
    el = jnp.where((lane >= lo) & (lane < lo + EXP_PER_GROUP), logits, -jnp.inf)
    m1 = jnp.max(el, axis=-1, keepdims=True)
    i1 = first(el == m1)
    el2 = jnp.where(lane == i1, -jnp.inf, el)
    m2 = jnp.max(el2, axis=-1, keepdims=True)
    i2 = first(el2 == m2)
    e2 = jnp.exp(m2 - m1)
    w1 = g_w / (1.0 + e2)
    w2 = g_w * e2 / (1.0 + e2)
    ids = jnp.where(lane == 0.0, i1 - N_GROUPS, jnp.where(lane == 1.0, i2 - N_GROUPS, 0.0))
    id_ref[...] = ids.astype(jnp.int32)
    wt_ref[...] = jnp.where(lane == 0.0, w1, jnp.where(lane == 1.0, w2, 0.0))


def _expert_kernel(te_ref, na_ref, x_ref, wg_ref, wu_ref, wd_ref, *rest, tile0):
    y_ref = rest[-1]
    j = tile0 + pl.program_id(0)

    @pl.when(j < na_ref[0])
    def _():
        x = x_ref[...]
        hg = _dot(x, wg_ref[0, 0].astype(BF16))
        hu = _dot(x, wu_ref[0, 0].astype(BF16))
        a = hg * jax.nn.sigmoid(hg) * hu
        y_ref[...] = _dot(a.astype(BF16), wd_ref[0, 0].astype(BF16)).astype(y_ref.dtype)

    @pl.when(j >= na_ref[0])
    def _():
        y_ref[...] = jnp.zeros(y_ref.shape, y_ref.dtype)


def _experts(x_chunks, tile_expert, n_active, wg, wu, wd, layer):
    tm = EXP_TM
    tiles_per_chunk = x_chunks[0].shape[0] // tm
    n_tiles = len(x_chunks) * tiles_per_chunk
    y = None
    for c, x_c in enumerate(x_chunks):
        tile0 = c * tiles_per_chunk
        x_map = lambda j, te, na, tile0=tile0: (
            jnp.clip(jnp.minimum(tile0 + j, na[0] - 1) - tile0, 0, tiles_per_chunk - 1), 0)
        w_map = lambda j, te, na, tile0=tile0: (layer, te[tile0 + j], 0, 0)
        in_specs = [pl.BlockSpec((tm, D_MODEL), x_map),
                    pl.BlockSpec((1, 1, D_MODEL, D_EXPERT), w_map),
                    pl.BlockSpec((1, 1, D_MODEL, D_EXPERT), w_map),
                    pl.BlockSpec((1, 1, D_EXPERT, D_MODEL), w_map)]
        args = [tile_expert, n_active, x_c, wg, wu, wd]
        aliases = {}
        if y is not None:
            in_specs.append(pl.BlockSpec(memory_space=pl.ANY))
            aliases = {len(args): 0}
            args.append(y)
        grid_spec = pltpu.PrefetchScalarGridSpec(
            num_scalar_prefetch=2,
            grid=(tiles_per_chunk,),
            in_specs=in_specs,
            out_specs=pl.BlockSpec((tm, D_MODEL), lambda j, te, na, tile0=tile0: (tile0 + j, 0)),
        )
        y = pl.pallas_call(
            functools.partial(_expert_kernel, tile0=tile0),
            grid_spec=grid_spec,
            out_shape=jax.ShapeDtypeStruct((n_tiles * tm, D_MODEL), BF16),
            input_output_aliases=aliases,
            compiler_params=_cparams("arbitrary"),
            name="moe_experts",
        )(*args)
    return y


def _dispatch_plan(expert_ids, tm, n_tiles):
    t = expert_ids.shape[0]
    e_flat = expert_ids.reshape(-1)
    onehot = (e_flat[:, None] == jnp.arange(N_EXPERTS, dtype=jnp.int32)[None, :]).astype(jnp.int32)
    csum = jnp.cumsum(onehot, axis=0)
    rank = jnp.sum(csum * onehot, axis=1) - 1
    counts = csum[-1]
    padded = ((counts + tm - 1) // tm) * tm
    ends = jnp.cumsum(padded)
    dest = jnp.sum((ends - padded)[None, :] * onehot, axis=1) + rank
    filler = jnp.arange(n_tiles * tm, dtype=jnp.int32) % t
    scatter_add = lambda base, idx, vals: base.at[idx].add(vals, mode="promise_in_bounds", unique_indices=True)
    src_tok = compute_on("tpu_sparsecore")(jax.jit(scatter_add))(
        filler, dest, jnp.arange(2 * t, dtype=jnp.int32) // 2 - dest % t)
    n_active = ends[-1] // tm
    tile_start = jnp.minimum(jnp.arange(n_tiles, dtype=jnp.int32), n_active - 1) * tm
    te = jnp.sum((ends[None, :] <= tile_start[:, None]).astype(jnp.int32), axis=1)
    te = jnp.minimum(te, N_EXPERTS - 1)
    return dest.reshape(t, 2), src_tok, te, n_active.reshape(1).astype(jnp.int32)


def _bf16_const(a):
    a16 = a.astype(BF16)
    assert np.all(a16.astype(np.float32) == a)
    return jnp.asarray(a16)


def _alibi_np(n):
    slopes = np.asarray([2.0 ** (-8.0 * (i + 1) / n) for i in range(n)], np.float32)
    _bf16_const(slopes)
    return slopes


def _selection_overlap_t(seq):
    n_cmp = (seq - L_CMP) // STRIDE_CMP + 1
    n_sel = seq // L_SEL
    c_start = STRIDE_CMP * np.arange(n_cmp)
    s_start = L_SEL * np.arange(n_sel)
    inter = np.clip(np.minimum(c_start[:, None] + L_CMP, s_start[None, :] + L_SEL)
                    - np.maximum(c_start[:, None], s_start[None, :]), 0, None) / L_CMP
    out = np.zeros((LANES, LANES), np.float32)
    out[:n_sel, :n_cmp] = inter.T
    return _bf16_const(out)


def _key_aug(seq, block, pad_rows=0, width=LANES, lanes=(AUG_POS_HI, AUG_POS_LO, AUG_ONE, AUG_PAD)):
    rows = pad_rows + seq
    pos = np.arange(rows)
    key = pos - pad_rows
    real = key >= 0
    out = np.zeros((rows, width), np.float32)
    if block:
        out[pos[real], key[real] // block] = -BIG
    out[:, lanes[0]] = pos // POS_SPLIT
    out[:, lanes[1]] = pos % POS_SPLIT
    out[:, lanes[2]] = 1.0
    out[~real, lanes[3]] = -BIG
    return _bf16_const(out)


def _block_average(seq):
    out = np.zeros((2 * SUBLANES, seq), np.float32)
    for n in range(seq // BLK_C):
        out[n, n * BLK_C:(n + 1) * BLK_C] = 1.0 / BLK_C
    return _bf16_const(out)


def kernel(x, w_in, nsa_cmp_pos, nsa_cmp_w, sink_b, w_br_a, w_br_b, w_br_c, w_out, ln1_g, ln1_b,
           w_group, b_group, w_router, b_router, w_gate, w_up, w_down, ln2_g, ln2_b):
    batch, seq, d = x.shape
    t = batch * seq
    assert d == D_MODEL and w_in.shape[2] == D_IN and seq % BLK_C == 0 and t % MM_TM == 0
    assert seq // L_SEL <= AUG_POS_HI and (seq - L_CMP) // STRIDE_CMP + 1 < LANES
    assert seq // BLK_C <= SUBLANES and (WIN_A + seq) // POS_SPLIT < 256
    n_tiles = (TOPK_EXPERT * t) // EXP_TM + N_EXPERTS
    assert n_tiles % EXP_CHUNKS == 0 and (t // ROW_TM) % COMBINE_SPLIT == 0 and OFF_QA % Q_A_W == 0

    slopes_a = jnp.asarray(_alibi_np(H_A))
    slopes_c = jnp.asarray(_alibi_np(H_C))
    _alibi_np(H_B)
    inter_t = _selection_overlap_t(seq)
    kaug_slc = _key_aug(seq, L_SEL)
    kaug_win = _key_aug(seq, 0, pad_rows=WIN_A)
    kaug_swa = _key_aug(seq, 0, pad_rows=WIN_B, width=HD_B, lanes=(SW_POS_HI, SW_POS_LO, SW_ONE, SW_PAD))
    kaug_moba = _key_aug(seq, BLK_C)
    avg = _block_average(seq)
    take = lambda a, idx: a.at[idx].get(mode="promise_in_bounds")
    sc_take = compute_on("tpu_sparsecore")(jax.jit(take))

    w_in_t = jnp.swapaxes(w_in, 1, 2)
    xf = x.reshape(t, d)
    xb = xf.astype(BF16)
    for l in range(DEPTH):
        z = _matmul(xb, _prep_w_in(w_in_t, l), BF16)
        kvc = _cmp_kv(z, nsa_cmp_pos, nsa_cmp_w, l, batch, seq)
        o_cmp, nsel = _nsa_cmp(z, kvc, inter_t, batch, seq)
        o_slc = _nsa_slc(z, nsel, kaug_slc, slopes_a, batch, seq)
        o_win = _nsa_win(z, kaug_win, slopes_a, batch, seq)
        o_b = _swa(z, kaug_swa, sink_b, l, batch, seq)
        o_c = _moba(z, kaug_moba, avg, slopes_c, batch, seq)
        merged = _merge(o_cmp, o_slc, o_win, o_b, o_c, z,
                        w_br_a[l].astype(BF16), w_br_b[l].astype(BF16), w_br_c[l].astype(BF16))
        w_rt = jnp.concatenate([w_group[l], w_router[l],
                                jnp.zeros((d, LANES - N_GROUPS - N_EXPERTS), F32)], axis=1)
        b_rt = jnp.concatenate([b_group[l], b_router[l],
                                jnp.zeros((LANES - N_GROUPS - N_EXPERTS,), F32)])[None, :]
        xf, xb, ids, wts = _proj_ln(merged, w_out[l].astype(BF16), xf, ln1_g[l][None, :], ln1_b[l][None, :],
                                    w_rt, b_rt)
        dest, src_tok, tile_expert, n_active = _dispatch_plan(ids[:, :TOPK_EXPERT], EXP_TM, n_tiles)
        x_chunks = [sc_take(xb, c) for c in jnp.split(src_tok, EXP_CHUNKS)]
        y = _experts(x_chunks, tile_expert, n_active, w_gate, w_up, w_down, l)
        y_groups = [(take(y, d[:, 0]), take(y, d[:, 1])) for d in jnp.split(dest, COMBINE_SPLIT)]
        xf, xb = _moe_ln(xf, y_groups, wts, ln2_g[l][None, :], ln2_b[l][None, :])
    return xf.reshape(batch, seq, d)
```
